```python
import math
import jax, jax.numpy as jnp
from jax import lax
import numpy as np

D_MODEL = 1024
BATCH = 8
SEQ = 4096
DEPTH = 1

ATTN_HEADS = 4
HEAD_DIM = 64
ATTN_WIDTH = ATTN_HEADS * 2 * HEAD_DIM
CONV_WIDTH = D_MODEL - ATTN_WIDTH
CONV_GROUPS = 8
CONV_K = 3
D_FF = 2816
ROPE_THETA = 10000.0
Q_BLOCK = 128
NORM_EPS = 1e-6
SUBLN_EPS = 1e-5
N_MOD = 9
IN_COLS = 3 * ATTN_WIDTH + 3 * CONV_WIDTH

kernel_name = "hymba_diffattn_shortconv_macaron_adaln"


def rmsnorm(x, g, eps=NORM_EPS):
    x32 = x.astype(jnp.float32)
    y = x32 * lax.rsqrt(jnp.mean(x32 * x32, axis=-1, keepdims=True) + eps)
    return (y * g.astype(jnp.float32)).astype(x.dtype)


def modulate(h, shift, scale):
    return h * (1.0 + scale[:, None, :]) + shift[:, None, :]


def swiglu(h, w1, w3, w2):
    return (jax.nn.silu(h @ w1) * (h @ w3)) @ w2


def rope_tables(seq, dim, dtype):
    inv = 1.0 / (ROPE_THETA ** (jnp.arange(0, dim, 2, dtype=jnp.float32) / dim))
    ang = jnp.arange(seq, dtype=jnp.float32)[:, None] * inv[None, :]
    ang = jnp.concatenate([ang, ang], axis=-1)
    return jnp.cos(ang).astype(dtype), jnp.sin(ang).astype(dtype)


def apply_rope(t, cos, sin):
    half = t.shape[-1] // 2
    t1, t2 = t[..., :half], t[..., half:]
    rot = jnp.concatenate([-t2, t1], axis=-1)
    return t * cos[None, :, None, None, :] + rot * sin[None, :, None, None, :]


def diff_attention(q, k, v, lam):
    b, s, h, _, dh = q.shape
    nb = s // Q_BLOCK
    scale = 1.0 / math.sqrt(dh)
    qb = q.reshape(b, nb, Q_BLOCK, h, 2, dh).transpose(1, 0, 2, 3, 4, 5)
    kpos = jnp.arange(s)
    lam32 = lam.astype(jnp.float32)

    def block(args):
        qblk, i = args
        sc = jnp.einsum('bqhcd,bkhcd->bhcqk', qblk, k,
                        preferred_element_type=jnp.float32) * scale
        qpos = i * Q_BLOCK + jnp.arange(Q_BLOCK)
        mask = kpos[None, :] <= qpos[:, None]
        sc = jnp.where(mask[None, None, None], sc, -jnp.inf)
        p = jax.nn.softmax(sc, axis=-1)
        p = p[:, :, 0] - lam32 * p[:, :, 1]
        return jnp.einsum('bhqk,bkhe->bqhe', p.astype(v.dtype), v)

    out = lax.map(block, (qb, jnp.arange(nb)))
    return out.transpose(1, 0, 2, 3, 4).reshape(b, s, h, v.shape[-1])


def causal_short_conv(u, w):
    ch = u.shape[-1]
    return lax.conv_general_dilated(
        u, w[:, None, :].astype(u.dtype), window_strides=(1,),
        padding=[(CONV_K - 1, 0)], dimension_numbers=('NWC', 'WIO', 'NWC'),
        feature_group_count=ch)


def hybrid_mixer(h, w_in, lq1, lk1, lq2, lk2, subln_g, conv_w, conv_norm_g, w_out,
                 cos, sin, lambda_init):
    b, s, _ = h.shape
    z = h @ w_in
    cuts = [ATTN_WIDTH, 2 * ATTN_WIDTH, 3 * ATTN_WIDTH,
            3 * ATTN_WIDTH + CONV_WIDTH, 3 * ATTN_WIDTH + 2 * CONV_WIDTH]
    q, k, v, gate_b, gate_c, u = jnp.split(z, cuts, axis=-1)
    q = apply_rope(q.reshape(b, s, ATTN_HEADS, 2, HEAD_DIM), cos, sin)
    k = apply_rope(k.reshape(b, s, ATTN_HEADS, 2, HEAD_DIM), cos, sin)
    v = v.reshape(b, s, ATTN_HEADS, 2 * HEAD_DIM)
    f32 = jnp.float32
    lam = (jnp.exp(jnp.sum(lq1.astype(f32) * lk1.astype(f32)))
           - jnp.exp(jnp.sum(lq2.astype(f32) * lk2.astype(f32))) + lambda_init)
    o = diff_attention(q, k, v, lam)
    o = rmsnorm(o, subln_g, SUBLN_EPS) * (1.0 - lambda_init)
    attn_out = o.reshape(b, s, ATTN_WIDTH)
    y = gate_b * causal_short_conv(gate_c * u, conv_w)
    gsz = CONV_WIDTH // CONV_GROUPS
    y = rmsnorm(y.reshape(b, s, CONV_GROUPS, gsz),
                conv_norm_g.reshape(CONV_GROUPS, gsz)).reshape(b, s, CONV_WIDTH)
    return jnp.concatenate([attn_out, y], axis=-1) @ w_out


def setup_inputs(seed: int = 0) -> dict:
    key = jax.random.key(seed)
    ks = iter(jax.random.split(key, 32))
    nrm = lambda shape, s: jax.random.normal(next(ks), shape, jnp.float32) * s
    gain = lambda shape: 1.0 + nrm(shape, 0.02)
    D = D_MODEL
    return {
        "x": nrm((BATCH, SEQ, D), 1.0),
        "c": nrm((BATCH, D), 1.0),
        "w_ada": nrm((DEPTH, D, N_MOD * D), D ** -0.5),
        "b_ada": nrm((DEPTH, N_MOD * D), 0.02),
        "w_ada_final": nrm((D, 2 * D), D ** -0.5),
        "b_ada_final": nrm((2 * D,), 0.02),
        "g_ffn1": gain((DEPTH, D)),
        "g_mix": gain((DEPTH, D)),
        "g_ffn2": gain((DEPTH, D)),
        "g_final": gain((D,)),
        "ffn1_w1": nrm((DEPTH, D, D_FF), D ** -0.5),
        "ffn1_w3": nrm((DEPTH, D, D_FF), D ** -0.5),
        "ffn1_w2": nrm((DEPTH, D_FF, D), D_FF ** -0.5),
        "ffn2_w1": nrm((DEPTH, D, D_FF), D ** -0.5),
        "ffn2_w3": nrm((DEPTH, D, D_FF), D ** -0.5),
        "ffn2_w2": nrm((DEPTH, D_FF, D), D_FF ** -0.5),
        "w_in": nrm((DEPTH, D, IN_COLS), D ** -0.5),
        "lambda_q1": nrm((DEPTH, HEAD_DIM), 0.1),
        "lambda_k1": nrm((DEPTH, HEAD_DIM), 0.1),
        "lambda_q2": nrm((DEPTH, HEAD_DIM), 0.1),
        "lambda_k2": nrm((DEPTH, HEAD_DIM), 0.1),
        "subln_g": gain((DEPTH, 2 * HEAD_DIM)),
        "conv_w": nrm((DEPTH, CONV_K, CONV_WIDTH), CONV_K ** -0.5),
        "conv_norm_g": gain((DEPTH, CONV_WIDTH)),
        "w_out": nrm((DEPTH, D, D), D ** -0.5),
    }


def reference(x, c, w_ada, b_ada, w_ada_final, b_ada_final, g_ffn1, g_mix, g_ffn2, g_final,
              ffn1_w1, ffn1_w3, ffn1_w2, ffn2_w1, ffn2_w3, ffn2_w2, w_in,
              lambda_q1, lambda_k1, lambda_q2, lambda_k2, subln_g, conv_w, conv_norm_g, w_out):
    s = x.shape[1]
    cos, sin = rope_tables(s, HEAD_DIM, x.dtype)
    c_act = jax.nn.silu(c)
    for l in range(DEPTH):
        lambda_init = 0.8 - 0.6 * math.exp(-0.3 * l)
        mod = c_act @ w_ada[l] + b_ada[l]
        sh1, sc1, gt1, sh2, sc2, gt2, sh3, sc3, gt3 = jnp.split(mod, N_MOD, axis=-1)
        h = modulate(rmsnorm(x, g_ffn1[l]), sh1, sc1)
        x = x + 0.5 * gt1[:, None, :] * swiglu(h, ffn1_w1[l], ffn1_w3[l], ffn1_w2[l])
        h = modulate(rmsnorm(x, g_mix[l]), sh2, sc2)
        x = x + gt2[:, None, :] * hybrid_mixer(
            h, w_in[l], lambda_q1[l], lambda_k1[l], lambda_q2[l], lambda_k2[l],
            subln_g[l], conv_w[l], conv_norm_g[l], w_out[l], cos, sin, lambda_init)
        h = modulate(rmsnorm(x, g_ffn2[l]), sh3, sc3)
        x = x + 0.5 * gt3[:, None, :] * swiglu(h, ffn2_w1[l], ffn2_w3[l], ffn2_w2[l])
    modf = c_act @ w_ada_final + b_ada_final
    shf, scf = jnp.split(modf, 2, axis=-1)
    return modulate(rmsnorm(x, g_final), shf, scf)
```

```python
import functools
import math

import jax
import jax.numpy as jnp
from jax import lax
from jax.experimental import pallas as pl
from jax.experimental.pallas import tpu as pltpu

F32 = jnp.float32
BF16 = jnp.bfloat16

LANES = 128
SUBLANES = 8

D_MODEL = 1024
ATTN_HEADS = 4
HEAD_DIM = 64
ATTN_WIDTH = ATTN_HEADS * 2 * HEAD_DIM
CONV_WIDTH = D_MODEL - ATTN_WIDTH
CONV_GROUPS = 8
CONV_K = 3
ROPE_THETA = 10000.0
NORM_EPS = 1e-6
SUBLN_EPS = 1e-5
N_MOD = 9
IN_COLS = 3 * ATTN_WIDTH + 3 * CONV_WIDTH

FF_CHUNK = 256
ROW_TILE = 512
ATTN_TILE = 256
VMEM_LIMIT = 56 * 1024 * 1024


def _silu(a):
    return a / (1.0 + jnp.exp(-a))


def _rms(x, eps):
    return x * lax.rsqrt(jnp.mean(x * x, axis=-1, keepdims=True) + eps)


def _ada_kernel(c_ref, w_ref, b_ref, o_ref):
    o_ref[...] = jnp.dot(_silu(c_ref[...]), w_ref[...],
                         preferred_element_type=F32) + b_ref[...]


def _ada(c, w, b, tn):
    bsz, d = c.shape
    n = w.shape[1]
    return pl.pallas_call(
        _ada_kernel,
        grid=(n // tn,),
        in_specs=[pl.BlockSpec((bsz, d), lambda j: (0, 0)),
                  pl.BlockSpec((d, tn), lambda j: (0, j)),
                  pl.BlockSpec((1, tn), lambda j: (0, j))],
        out_specs=pl.BlockSpec((bsz, tn), lambda j: (0, j)),
        out_shape=jax.ShapeDtypeStruct((bsz, n), F32),
        name="ada",
    )(c, w, b.reshape(1, n))


def _ffn_kernel(*refs, mod_row, n_chunks, final):
    if final:
        (x_ref, mod_ref, g_ref, w13_ref, w2_ref, modf_ref, gf_ref, o_ref, act_ref) = refs
    else:
        (x_ref, mod_ref, g_ref, w13_ref, w2_ref, o_ref, act_ref) = refs
    x = x_ref[...]
    shift = mod_ref[0, mod_row:mod_row + 1, :]
    scale = mod_ref[0, mod_row + 1:mod_row + 2, :]
    gate = mod_ref[0, mod_row + 2:mod_row + 3, :]
    h = (_rms(x, NORM_EPS) * g_ref[...] * (1.0 + scale) + shift).astype(BF16)

    def chunk(c, carry):
        ab = jnp.dot(h, w13_ref[c], preferred_element_type=F32)
        act_ref[c] = (_silu(ab[:, :FF_CHUNK]) * ab[:, FF_CHUNK:]).astype(BF16)
        return carry

    lax.fori_loop(0, n_chunks, chunk, 0)
    acc = jnp.dot(act_ref[0], w2_ref[0], preferred_element_type=F32)
    for c in range(1, n_chunks):
        acc = acc + jnp.dot(act_ref[c], w2_ref[c], preferred_element_type=F32)
    y = x + 0.5 * gate * acc
    if final:
        y = (_rms(y, NORM_EPS) * gf_ref[...] * (1.0 + modf_ref[0, 1:2, :])
             + modf_ref[0, 0:1, :])
    o_ref[...] = y


def _ffn(x2d, mod3, g, w13, w2, mod_row, rows_per_batch, final_args=None):
    m, d = x2d.shape
    n_chunks = w13.shape[0]
    tm = ROW_TILE
    tiles_per_batch = rows_per_batch // tm
    final = final_args is not None
    const3 = lambda i: (0, 0, 0)
    in_specs = [pl.BlockSpec((tm, d), lambda i: (i, 0)),
                pl.BlockSpec((1, N_MOD, d), lambda i: (i // tiles_per_batch, 0, 0)),
                pl.BlockSpec((1, d), lambda i: (0, 0)),
                pl.BlockSpec(w13.shape, const3, pipeline_mode=pl.Buffered(1)),
                pl.BlockSpec(w2.shape, const3, pipeline_mode=pl.Buffered(1))]
    args = [x2d, mod3, g.reshape(1, d), w13, w2]
    if final:
        modf3, gf = final_args
        in_specs += [pl.BlockSpec((1, 2, d), lambda i: (i // tiles_per_batch, 0, 0)),
                     pl.BlockSpec((1, d), lambda i: (0, 0))]
        args += [modf3, gf.reshape(1, d)]
    return pl.pallas_call(
        functools.partial(_ffn_kernel, mod_row=mod_row, n_chunks=n_chunks, final=final),
        grid=(m // tm,),
        in_specs=in_specs,
        out_specs=pl.BlockSpec((tm, d), lambda i: (i, 0)),
        out_shape=jax.ShapeDtypeStruct((m, d), F32),
        scratch_shapes=[pltpu.VMEM((n_chunks, tm, FF_CHUNK), BF16)],
        compiler_params=pltpu.CompilerParams(
            dimension_semantics=("arbitrary",), vmem_limit_bytes=VMEM_LIMIT),
        name="ffn_final" if final else "ffn",
    )(*args)


def _inproj_kernel(x_ref, mod_ref, g_ref, w_ref, cos_ref, sinp_ref, sinm_ref,
                   convw_ref, cng_ref, gmat_ref,
                   q_ref, k_ref, v_ref, y_ref, cu_ref, *, mod_row, tiles_per_batch):
    tm = x_ref.shape[0]
    x = x_ref[...]
    shift = mod_ref[0, mod_row:mod_row + 1, :]
    scale = mod_ref[0, mod_row + 1:mod_row + 2, :]
    h = (_rms(x, NORM_EPS) * g_ref[...] * (1.0 + scale) + shift).astype(BF16)
    z = jnp.dot(h, w_ref[...], preferred_element_type=F32)

    cos = cos_ref[...]
    sinp = sinp_ref[...]
    sinm = sinm_ref[...]
    q_scale = 1.0 / math.sqrt(HEAD_DIM)
    half = HEAD_DIM // 2
    for j in range(ATTN_WIDTH // LANES):
        for base, out_ref, mult in ((0, q_ref, q_scale), (ATTN_WIDTH, k_ref, 1.0)):
            t = z[:, base + j * LANES: base + (j + 1) * LANES]
            r = (t * cos + pltpu.roll(t, half, axis=1) * sinp
                 + pltpu.roll(t, LANES - half, axis=1) * sinm)
            out_ref[:, j * LANES:(j + 1) * LANES] = (r * mult).astype(BF16)
    v_ref[...] = z[:, 2 * ATTN_WIDTH:3 * ATTN_WIDTH].astype(BF16)

    c0 = 3 * ATTN_WIDTH
    gate_b = z[:, c0:c0 + CONV_WIDTH]
    cu = z[:, c0 + CONV_WIDTH:c0 + 2 * CONV_WIDTH] * z[:, c0 + 2 * CONV_WIDTH:]

    @pl.when(pl.program_id(0) % tiles_per_batch == 0)
    def _():
        cu_ref[0:SUBLANES, :] = jnp.zeros((SUBLANES, CONV_WIDTH), F32)

    cu_ref[SUBLANES:, :] = cu
    buf = cu_ref[...]
    cu_m1 = pltpu.roll(buf, 1, axis=0)[SUBLANES:, :]
    cu_m2 = pltpu.roll(buf, 2, axis=0)[SUBLANES:, :]
    cu_ref[0:SUBLANES, :] = cu[tm - SUBLANES:, :]
    w = convw_ref[...]
    y = gate_b * (w[0:1, :] * cu_m2 + w[1:2, :] * cu_m1 + w[2:3, :] * cu)
    y2 = y * y
    y2_hi = y2.astype(BF16)
    y2_lo = (y2 - y2_hi.astype(F32)).astype(BF16)
    ms = (jnp.dot(y2_hi, gmat_ref[...], preferred_element_type=F32)
          + jnp.dot(y2_lo, gmat_ref[...], preferred_element_type=F32))
    y_ref[...] = (y * lax.rsqrt(ms + NORM_EPS) * cng_ref[...]).astype(BF16)


def _inproj(x2d, mod3, g, w_in, cos, sinp, sinm, conv_w, conv_norm_g, gmat,
            rows_per_batch):
    m, d = x2d.shape
    tm = ROW_TILE
    tiles_per_batch = rows_per_batch // tm
    row = lambda i: (i, 0)
    fix = lambda i: (0, 0)
    pos = lambda i: (i % tiles_per_batch, 0)
    out_sd = jax.ShapeDtypeStruct((m, ATTN_WIDTH), BF16)
    return pl.pallas_call(
        functools.partial(_inproj_kernel, mod_row=3, tiles_per_batch=tiles_per_batch),
        grid=(m // tm,),
        in_specs=[pl.BlockSpec((tm, d), row),
                  pl.BlockSpec((1, N_MOD, d), lambda i: (i // tiles_per_batch, 0, 0)),
                  pl.BlockSpec((1, d), fix),
                  pl.BlockSpec(w_in.shape, fix, pipeline_mode=pl.Buffered(1)),
                  pl.BlockSpec((tm, LANES), pos),
                  pl.BlockSpec((tm, LANES), pos),
                  pl.BlockSpec((tm, LANES), pos),
                  pl.BlockSpec((CONV_K, CONV_WIDTH), fix),
                  pl.BlockSpec((1, CONV_WIDTH), fix),
                  pl.BlockSpec((CONV_WIDTH, CONV_WIDTH), fix)],
        out_specs=[pl.BlockSpec((tm, ATTN_WIDTH), row)] * 3
                  + [pl.BlockSpec((tm, CONV_WIDTH), row)],
        out_shape=[out_sd, out_sd, out_sd,
                   jax.ShapeDtypeStruct((m, CONV_WIDTH), BF16)],
        scratch_shapes=[pltpu.VMEM((tm + SUBLANES, CONV_WIDTH), F32)],
        compiler_params=pltpu.CompilerParams(
            dimension_semantics=("arbitrary",), vmem_limit_bytes=VMEM_LIMIT),
        name="inproj",
    )(x2d, mod3, g.reshape(1, d), w_in, cos, sinp, sinm, conv_w,
      conv_norm_g.reshape(1, CONV_WIDTH), gmat)


def _attn_kernel(q_ref, k_ref, v_ref, lq1_ref, lk1_ref, lq2_ref, lk2_ref, sg_ref,
                 o_ref, *, lambda_init):
    seq = q_ref.shape[1]
    t = ATTN_TILE
    n_tiles = seq // t
    lam = (jnp.exp(jnp.sum(lq1_ref[...] * lk1_ref[...], axis=-1, keepdims=True))
           - jnp.exp(jnp.sum(lq2_ref[...] * lk2_ref[...], axis=-1, keepdims=True))
           + lambda_init)
    lane = lax.broadcasted_iota(jnp.int32, (t, LANES), 1)
    first_map = lane < HEAD_DIM
    row = lax.broadcasted_iota(jnp.int32, (2 * t, t), 0)
    col = lax.broadcasted_iota(jnp.int32, (2 * t, t), 1)
    causal = col <= jnp.where(row >= t, row - t, row)
    nt_dims = (((1,), (1,)), ((), ()))

    def q_tile(i, carry):
        q0 = pl.multiple_of(i * t, t)
        q = q_ref[0, pl.ds(q0, t), :]
        zero = jnp.zeros_like(q)
        qq = jnp.concatenate([jnp.where(first_map, q, zero),
                              jnp.where(first_map, zero, q)], axis=0)

        def step(k0, m, l, acc, mask):
            kt = k_ref[0, pl.ds(k0, t), :]
            vt = v_ref[0, pl.ds(k0, t), :]
            s = lax.dot_general(qq, kt, nt_dims, preferred_element_type=F32)
            if mask:
                s = jnp.where(causal, s, -jnp.inf)
            m_new = jnp.maximum(m, jnp.max(s, axis=-1, keepdims=True))
            alpha = jnp.exp(m - m_new)
            p = jnp.exp(s - m_new)
            l = alpha * l + jnp.sum(p, axis=-1, keepdims=True)
            acc = alpha * acc + jnp.dot(p.astype(BF16), vt, preferred_element_type=F32)
            return m_new, l, acc

        def body(j, mla):
            return step(pl.multiple_of(j * t, t), *mla, mask=False)

        init = (jnp.full((2 * t, 1), -jnp.inf, F32), jnp.zeros((2 * t, 1), F32),
                jnp.zeros((2 * t, LANES), F32))
        mla = lax.fori_loop(0, i, body, init)
        m, l, acc = step(q0, *mla, mask=True)
        o = acc / l
        o = o[:t, :] - lam * o[t:, :]
        o = _rms(o, SUBLN_EPS) * sg_ref[...] * (1.0 - lambda_init)
        o_ref[0, pl.ds(q0, t), :] = o.astype(BF16)
        return carry

    lax.fori_loop(0, n_tiles, q_tile, 0)


def _attn(q, k, v, lq1, lk1, lq2, lk2, subln_g, lambda_init):
    bsz, seq, _ = q.shape
    blk = pl.BlockSpec((1, seq, LANES), lambda b, h: (b, 0, h))
    vec = pl.BlockSpec((1, HEAD_DIM), lambda b, h: (0, 0))
    return pl.pallas_call(
        functools.partial(_attn_kernel, lambda_init=lambda_init),
        grid=(bsz, ATTN_HEADS),
        in_specs=[blk, blk, blk, vec, vec, vec, vec,
                  pl.BlockSpec((1, 2 * HEAD_DIM), lambda b, h: (0, 0))],
        out_specs=blk,
        out_shape=jax.ShapeDtypeStruct((bsz, seq, ATTN_WIDTH), BF16),
        compiler_params=pltpu.CompilerParams(
            dimension_semantics=("arbitrary", "arbitrary"), vmem_limit_bytes=VMEM_LIMIT),
        name="attn",
    )(q, k, v, lq1.reshape(1, -1), lk1.reshape(1, -1), lq2.reshape(1, -1),
      lk2.reshape(1, -1), subln_g.reshape(1, -1))


def _outproj_kernel(x_ref, a_ref, y_ref, mod_ref, wa_ref, wy_ref, o_ref, *, mod_row):
    gate = mod_ref[0, mod_row + 2:mod_row + 3, :]
    mix = (jnp.dot(a_ref[...], wa_ref[...], preferred_element_type=F32)
           + jnp.dot(y_ref[...], wy_ref[...], preferred_element_type=F32))
    o_ref[...] = x_ref[...] + gate * mix


def _outproj(x2d, a2d, y2d, mod3, wa, wy, rows_per_batch):
    m, d = x2d.shape
    tm = ROW_TILE
    tiles_per_batch = rows_per_batch // tm
    row = lambda i: (i, 0)
    fix = lambda i: (0, 0)
    return pl.pallas_call(
        functools.partial(_outproj_kernel, mod_row=3),
        grid=(m // tm,),
        in_specs=[pl.BlockSpec((tm, d), row),
                  pl.BlockSpec((tm, ATTN_WIDTH), row),
                  pl.BlockSpec((tm, CONV_WIDTH), row),
                  pl.BlockSpec((1, N_MOD, d), lambda i: (i // tiles_per_batch, 0, 0)),
                  pl.BlockSpec(wa.shape, fix),
                  pl.BlockSpec(wy.shape, fix)],
        out_specs=pl.BlockSpec((tm, d), row),
        out_shape=jax.ShapeDtypeStruct((m, d), F32),
        compiler_params=pltpu.CompilerParams(
            dimension_semantics=("arbitrary",), vmem_limit_bytes=VMEM_LIMIT),
        name="outproj",
    )(x2d, a2d, y2d, mod3, wa, wy)


def _ffn_weights(w1, w3, w2):
    d, f = w1.shape
    n = f // FF_CHUNK
    w13 = jnp.concatenate([w1.reshape(d, n, FF_CHUNK), w3.reshape(d, n, FF_CHUNK)], axis=-1)
    return (w13.transpose(1, 0, 2).astype(BF16),
            w2.reshape(n, FF_CHUNK, d).astype(BF16))


def _rope_tables(seq):
    inv = 1.0 / (ROPE_THETA ** (jnp.arange(0, HEAD_DIM, 2, dtype=F32) / HEAD_DIM))
    ang = jnp.arange(seq, dtype=F32)[:, None] * inv[None, :]
    ang = jnp.concatenate([ang, ang, ang, ang], axis=-1)
    cos, sin = jnp.cos(ang), jnp.sin(ang)
    upper = (jnp.arange(LANES) % HEAD_DIM) >= HEAD_DIM // 2
    return cos, jnp.where(upper, sin, 0.0), jnp.where(upper, 0.0, -sin)


def kernel(x, c, w_ada, b_ada, w_ada_final, b_ada_final, g_ffn1, g_mix, g_ffn2, g_final,
           ffn1_w1, ffn1_w3, ffn1_w2, ffn2_w1, ffn2_w3, ffn2_w2, w_in,
           lambda_q1, lambda_k1, lambda_q2, lambda_k2, subln_g, conv_w, conv_norm_g, w_out):
    bsz, seq, d = x.shape
    depth = w_ada.shape[0]
    cos, sinp, sinm = _rope_tables(seq)
    gsz = CONV_WIDTH // CONV_GROUPS
    grp = jnp.arange(CONV_WIDTH) // gsz
    gmat = jnp.where(grp[:, None] == grp[None, :], 1.0 / gsz, 0.0).astype(BF16)
    modf3 = _ada(c, w_ada_final, b_ada_final, 1024).reshape(bsz, 2, d)

    x2d = x.reshape(bsz * seq, d)
    for l in range(depth):
        lambda_init = 0.8 - 0.6 * math.exp(-0.3 * l)
        mod3 = _ada(c, w_ada[l], b_ada[l], 1024).reshape(bsz, N_MOD, d)
        w13a, w2a = _ffn_weights(ffn1_w1[l], ffn1_w3[l], ffn1_w2[l])
        w13b, w2b = _ffn_weights(ffn2_w1[l], ffn2_w3[l], ffn2_w2[l])
        w_in_b = w_in[l].astype(BF16)
        w_out_b = w_out[l].astype(BF16)

        x2d = _ffn(x2d, mod3, g_ffn1[l], w13a, w2a, 0, seq)
        q, k, v, y = _inproj(x2d, mod3, g_mix[l], w_in_b, cos, sinp, sinm,
                             conv_w[l], conv_norm_g[l], gmat, seq)
        a = _attn(q.reshape(bsz, seq, ATTN_WIDTH), k.reshape(bsz, seq, ATTN_WIDTH),
                  v.reshape(bsz, seq, ATTN_WIDTH), lambda_q1[l], lambda_k1[l],
                  lambda_q2[l], lambda_k2[l], subln_g[l], lambda_init)
        x2d = _outproj(x2d, a.reshape(bsz * seq, ATTN_WIDTH), y, mod3,
                       w_out_b[:ATTN_WIDTH], w_out_b[ATTN_WIDTH:], seq)
        final_args = (modf3, g_final) if l == depth - 1 else None
        x2d = _ffn(x2d, mod3, g_ffn2[l], w13b, w2b, 6, seq, final_args)
    return x2d.reshape(bsz, seq, d)
```

```python
import functools
import math

import jax
import jax.numpy as jnp
from jax import lax
from jax.experimental import pallas as pl
from jax.experimental.pallas import tpu as pltpu

F32 = jnp.float32
BF16 = jnp.bfloat16

LANES = 128
SUBLANES = 8

D_MODEL = 1024
ATTN_HEADS = 4
HEAD_DIM = 64
ATTN_WIDTH = ATTN_HEADS * 2 * HEAD_DIM
CONV_WIDTH = D_MODEL - ATTN_WIDTH
CONV_GROUPS = 8
CONV_K = 3
ROPE_THETA = 10000.0
NORM_EPS = 1e-6
SUBLN_EPS = 1e-5
N_MOD = 9
IN_COLS = 3 * ATTN_WIDTH + 3 * CONV_WIDTH

FF_CHUNK = 256
ROW_TILE = 512
ATTN_TILE = 512
VMEM_LIMIT = 56 * 1024 * 1024
NT_DIMS = (((1,), (1,)), ((), ()))
Q_SCALE = math.log2(math.e) / math.sqrt(HEAD_DIM)


def _silu(a):
    return a / (1.0 + jnp.exp(-a))


def _rms(x, eps):
    return x * lax.rsqrt(jnp.mean(x * x, axis=-1, keepdims=True) + eps)


def _ada_kernel(c_ref, w_ref, b_ref, o_ref):
    o_ref[...] = jnp.dot(_silu(c_ref[...]), w_ref[...],
                         preferred_element_type=F32) + b_ref[...]


def _ada(c, w, b, tn):
    bsz, d = c.shape
    n = w.shape[1]
    return pl.pallas_call(
        _ada_kernel,
        grid=(n // tn,),
        in_specs=[pl.BlockSpec((bsz, d), lambda j: (0, 0)),
                  pl.BlockSpec((d, tn), lambda j: (0, j)),
                  pl.BlockSpec((1, tn), lambda j: (0, j))],
        out_specs=pl.BlockSpec((bsz, tn), lambda j: (0, j)),
        out_shape=jax.ShapeDtypeStruct((bsz, n), F32),
        name="ada",
    )(c, w, b.reshape(1, n))


def _ffn_kernel(*refs, mod_row, n_chunks, final):
    if final:
        (x_ref, mod_ref, g_ref, w13_ref, w2_ref, modf_ref, gf_ref, o_ref, act_ref) = refs
    else:
        (x_ref, mod_ref, g_ref, w13_ref, w2_ref, o_ref, act_ref) = refs
    x = x_ref[...]
    shift = mod_ref[0, mod_row:mod_row + 1, :]
    scale = mod_ref[0, mod_row + 1:mod_row + 2, :]
    gate = mod_ref[0, mod_row + 2:mod_row + 3, :]
    h = (_rms(x, NORM_EPS) * g_ref[...] * (1.0 + scale) + shift).astype(BF16)

    def chunk(c, carry):
        ab = jnp.dot(h, w13_ref[c], preferred_element_type=F32)
        act_ref[c] = (_silu(ab[:, :FF_CHUNK]) * ab[:, FF_CHUNK:]).astype(BF16)
        return carry

    lax.fori_loop(0, n_chunks, chunk, 0)
    acc = jnp.dot(act_ref[0], w2_ref[0], preferred_element_type=F32)
    for c in range(1, n_chunks):
        acc = acc + jnp.dot(act_ref[c], w2_ref[c], preferred_element_type=F32)
    y = x + 0.5 * gate * acc
    if final:
        y = (_rms(y, NORM_EPS) * gf_ref[...] * (1.0 + modf_ref[0, 1:2, :])
             + modf_ref[0, 0:1, :])
    o_ref[...] = y


def _ffn(x2d, mod3, g, w13, w2, mod_row, rows_per_batch, final_args=None):
    m, d = x2d.shape
    n_chunks = w13.shape[0]
    tm = ROW_TILE
    tiles_per_batch = rows_per_batch // tm
    final = final_args is not None
    const3 = lambda i: (0, 0, 0)
    in_specs = [pl.BlockSpec((tm, d), lambda i: (i, 0)),
                pl.BlockSpec((1, N_MOD, d), lambda i: (i // tiles_per_batch, 0, 0)),
                pl.BlockSpec((1, d), lambda i: (0, 0)),
                pl.BlockSpec(w13.shape, const3, pipeline_mode=pl.Buffered(1)),
                pl.BlockSpec(w2.shape, const3, pipeline_mode=pl.Buffered(1))]
    args = [x2d, mod3, g.reshape(1, d), w13, w2]
    if final:
        modf3, gf = final_args
        in_specs += [pl.BlockSpec((1, 2, d), lambda i: (i // tiles_per_batch, 0, 0)),
                     pl.BlockSpec((1, d), lambda i: (0, 0))]
        args += [modf3, gf.reshape(1, d)]
    return pl.pallas_call(
        functools.partial(_ffn_kernel, mod_row=mod_row, n_chunks=n_chunks, final=final),
        grid=(m // tm,),
        in_specs=in_specs,
        out_specs=pl.BlockSpec((tm, d), lambda i: (i, 0)),
        out_shape=jax.ShapeDtypeStruct((m, d), F32),
        scratch_shapes=[pltpu.VMEM((n_chunks, tm, FF_CHUNK), BF16)],
        compiler_params=pltpu.CompilerParams(
            dimension_semantics=("arbitrary",), vmem_limit_bytes=VMEM_LIMIT),
        name="ffn_final" if final else "ffn",
    )(*args)


def _inproj_kernel(x_ref, mod_ref, g_ref, w_ref, wvt_ref, cos_ref, sinp_ref, sinm_ref,
                   convw_ref, cng_ref, gmat_ref,
                   q_ref, k_ref, vt_ref, y_ref, cu_ref, *, mod_row, tiles_per_batch):
    tm = x_ref.shape[0]
    x = x_ref[...]
    shift = mod_ref[0, mod_row:mod_row + 1, :]
    scale = mod_ref[0, mod_row + 1:mod_row + 2, :]
    h = (_rms(x, NORM_EPS) * g_ref[...] * (1.0 + scale) + shift).astype(BF16)
    z = jnp.dot(h, w_ref[...], preferred_element_type=F32)
    vt_ref[0] = lax.dot_general(wvt_ref[...], h, NT_DIMS,
                                preferred_element_type=F32).astype(BF16)

    cos = cos_ref[...]
    sinp = sinp_ref[...]
    sinm = sinm_ref[...]
    half = HEAD_DIM // 2
    for j in range(ATTN_WIDTH // LANES):
        for base, out_ref, mult in ((0, q_ref, Q_SCALE), (ATTN_WIDTH, k_ref, 1.0)):
            t = z[:, base + j * LANES: base + (j + 1) * LANES]
            r = (t * cos + pltpu.roll(t, half, axis=1) * sinp
                 + pltpu.roll(t, LANES - half, axis=1) * sinm)
            out_ref[:, j * LANES:(j + 1) * LANES] = (r * mult).astype(BF16)

    c0 = 2 * ATTN_WIDTH
    gate_b = z[:, c0:c0 + CONV_WIDTH]
    cu = z[:, c0 + CONV_WIDTH:c0 + 2 * CONV_WIDTH] * z[:, c0 + 2 * CONV_WIDTH:]

    @pl.when(pl.program_id(0) % tiles_per_batch == 0)
    def _():
        cu_ref[0:SUBLANES, :] = jnp.zeros((SUBLANES, CONV_WIDTH), F32)

    cu_ref[SUBLANES:, :] = cu
    buf = cu_ref[...]
    cu_m1 = pltpu.roll(buf, 1, axis=0)[SUBLANES:, :]
    cu_m2 = pltpu.roll(buf, 2, axis=0)[SUBLANES:, :]
    cu_ref[0:SUBLANES, :] = cu[tm - SUBLANES:, :]
    w = convw_ref[...]
    y = gate_b * (w[0:1, :] * cu_m2 + w[1:2, :] * cu_m1 + w[2:3, :] * cu)
    y2 = y * y
    y2_hi = y2.astype(BF16)
    y2_lo = (y2 - y2_hi.astype(F32)).astype(BF16)
    ms = (jnp.dot(y2_hi, gmat_ref[...], preferred_element_type=F32)
          + jnp.dot(y2_lo, gmat_ref[...], preferred_element_type=F32))
    y_ref[...] = (y * lax.rsqrt(ms + NORM_EPS) * cng_ref[...]).astype(BF16)


def _inproj(x2d, mod3, g, w_main, w_vt, cos, sinp, sinm, conv_w, conv_norm_g, gmat,
            rows_per_batch):
    m, d = x2d.shape
    tm = ROW_TILE
    tiles_per_batch = rows_per_batch // tm
    row = lambda i: (i, 0)
    fix = lambda i: (0, 0)
    pos = lambda i: (i % tiles_per_batch, 0)
    out_sd = jax.ShapeDtypeStruct((m, ATTN_WIDTH), BF16)
    return pl.pallas_call(
        functools.partial(_inproj_kernel, mod_row=3, tiles_per_batch=tiles_per_batch),
        grid=(m // tm,),
        in_specs=[pl.BlockSpec((tm, d), row),
                  pl.BlockSpec((1, N_MOD, d), lambda i: (i // tiles_per_batch, 0, 0)),
                  pl.BlockSpec((1, d), fix),
                  pl.BlockSpec(w_main.shape, fix, pipeline_mode=pl.Buffered(1)),
                  pl.BlockSpec(w_vt.shape, fix, pipeline_mode=pl.Buffered(1)),
                  pl.BlockSpec((tm, LANES), pos),
                  pl.BlockSpec((tm, LANES), pos),
                  pl.BlockSpec((tm, LANES), pos),
                  pl.BlockSpec((CONV_K, CONV_WIDTH), fix),
                  pl.BlockSpec((1, CONV_WIDTH), fix),
                  pl.BlockSpec((CONV_WIDTH, CONV_WIDTH), fix)],
        out_specs=[pl.BlockSpec((tm, ATTN_WIDTH), row),
                   pl.BlockSpec((tm, ATTN_WIDTH), row),
                   pl.BlockSpec((1, ATTN_WIDTH, tm),
                                lambda i: (i // tiles_per_batch, 0, i % tiles_per_batch)),
                   pl.BlockSpec((tm, CONV_WIDTH), row)],
        out_shape=[out_sd, out_sd,
                   jax.ShapeDtypeStruct((m // rows_per_batch, ATTN_WIDTH, rows_per_batch), BF16),
                   jax.ShapeDtypeStruct((m, CONV_WIDTH), BF16)],
        scratch_shapes=[pltpu.VMEM((tm + SUBLANES, CONV_WIDTH), F32)],
        compiler_params=pltpu.CompilerParams(
            dimension_semantics=("arbitrary",), vmem_limit_bytes=VMEM_LIMIT),
        name="inproj",
    )(x2d, mod3, g.reshape(1, d), w_main, w_vt, cos, sinp, sinm, conv_w,
      conv_norm_g.reshape(1, CONV_WIDTH), gmat)


def _attn_kernel(q_ref, k_ref, vt_ref, lq1_ref, lk1_ref, lq2_ref, lk2_ref, sg_ref,
                 o_ref, sa_ref, sb_ref, m_ref, l_ref, acc_ref, *, lambda_init):
    seq = q_ref.shape[1]
    t = ATTN_TILE
    n_tiles = seq // t
    lam = (jnp.exp(jnp.sum(lq1_ref[...] * lk1_ref[...], axis=-1, keepdims=True))
           - jnp.exp(jnp.sum(lq2_ref[...] * lk2_ref[...], axis=-1, keepdims=True))
           + lambda_init)

    def q_tile(i, carry):
        q0 = pl.multiple_of(i * t, t)
        q = q_ref[0, pl.ds(q0, t), :]
        first_map = lax.broadcasted_iota(jnp.int32, (t, LANES), 1) < HEAD_DIM
        zero = jnp.zeros_like(q)
        qq = jnp.concatenate([jnp.where(first_map, q, zero),
                              jnp.where(first_map, zero, q)], axis=0)

        def scores(k0):
            return lax.dot_general(k_ref[0, pl.ds(k0, t), :], qq, NT_DIMS,
                                   preferred_element_type=F32)

        def absorb(s_buf, k0, mask=False):
            s = s_buf[...]
            if mask:
                kpos = lax.broadcasted_iota(jnp.int32, (t, 2 * t), 0)
                qpos = lax.broadcasted_iota(jnp.int32, (t, 2 * t), 1)
                qpos = jnp.where(qpos >= t, qpos - t, qpos)
                s = jnp.where(kpos <= qpos, s, -jnp.inf)
            m = m_ref[...]
            m_new = jnp.maximum(m, jnp.max(s, axis=0, keepdims=True))
            alpha = jnp.exp2(m - m_new)
            p = jnp.exp2(s - m_new)
            m_ref[...] = m_new
            l_ref[...] = alpha * l_ref[...] + jnp.sum(p, axis=0, keepdims=True)
            pv = jnp.dot(vt_ref[0, :, pl.ds(k0, t)], p.astype(BF16),
                         preferred_element_type=F32)
            acc_ref[...] = alpha * acc_ref[...] + pv

        m_ref[...] = jnp.full(m_ref.shape, -jnp.inf, F32)
        l_ref[...] = jnp.zeros_like(l_ref)
        acc_ref[...] = jnp.zeros_like(acc_ref)
        last = jnp.maximum(i - 1, 0)
        sa_ref[...] = scores(q0)
        sb_ref[...] = scores(0)
        absorb(sa_ref, q0, mask=True)

        def pair(jj, carry2):
            k_even = pl.multiple_of(2 * jj * t, t)
            k_odd = pl.multiple_of(k_even + t, t)
            sa_ref[...] = scores(k_odd)
            absorb(sb_ref, k_even)
            sb_ref[...] = scores(pl.multiple_of(jnp.minimum(2 * jj + 2, last) * t, t))
            absorb(sa_ref, k_odd)
            return carry2

        lax.fori_loop(0, i // 2, pair, 0)

        @pl.when(i % 2 == 1)
        def _():
            absorb(sb_ref, pl.multiple_of(last * t, t))

        o = acc_ref[...] / l_ref[...]
        o = o[:, :t] - lam * o[:, t:]
        o = o * lax.rsqrt(jnp.mean(o * o, axis=0, keepdims=True) + SUBLN_EPS)
        o = o * (sg_ref[...] * (1.0 - lambda_init))
        o_ref[0, pl.ds(q0, t), :] = o.T.astype(BF16)
        return carry

    lax.fori_loop(0, n_tiles, q_tile, 0)


def _attn(q, k, vt, lq1, lk1, lq2, lk2, subln_g, lambda_init):
    bsz, seq, _ = q.shape
    t = ATTN_TILE
    blk = pl.BlockSpec((1, seq, LANES), lambda b, h: (b, 0, h))
    vec = pl.BlockSpec((1, HEAD_DIM), lambda b, h: (0, 0))
    return pl.pallas_call(
        functools.partial(_attn_kernel, lambda_init=lambda_init),
        grid=(bsz, ATTN_HEADS),
        in_specs=[blk, blk,
                  pl.BlockSpec((1, LANES, seq), lambda b, h: (b, h, 0)),
                  vec, vec, vec, vec,
                  pl.BlockSpec((2 * HEAD_DIM, 1), lambda b, h: (0, 0))],
        out_specs=blk,
        out_shape=jax.ShapeDtypeStruct((bsz, seq, ATTN_WIDTH), BF16),
        scratch_shapes=[pltpu.VMEM((t, 2 * t), F32),
                        pltpu.VMEM((t, 2 * t), F32),
                        pltpu.VMEM((1, 2 * t), F32),
                        pltpu.VMEM((1, 2 * t), F32),
                        pltpu.VMEM((2 * HEAD_DIM, 2 * t), F32)],
        compiler_params=pltpu.CompilerParams(
            dimension_semantics=("arbitrary", "arbitrary"), vmem_limit_bytes=VMEM_LIMIT),
        name="attn",
    )(q, k, vt, lq1.reshape(1, -1), lk1.reshape(1, -1), lq2.reshape(1, -1),
      lk2.reshape(1, -1), subln_g.reshape(-1, 1))


def _outproj_kernel(x_ref, a_ref, y_ref, mod_ref, wa_ref, wy_ref, o_ref, *, mod_row):
    gate = mod_ref[0, mod_row + 2:mod_row + 3, :]
    mix = (jnp.dot(a_ref[...], wa_ref[...], preferred_element_type=F32)
           + jnp.dot(y_ref[...], wy_ref[...], preferred_element_type=F32))
    o_ref[...] = x_ref[...] + gate * mix


def _outproj(x2d, a2d, y2d, mod3, wa, wy, rows_per_batch):
    m, d = x2d.shape
    tm = ROW_TILE
    tiles_per_batch = rows_per_batch // tm
    row = lambda i: (i, 0)
    fix = lambda i: (0, 0)
    return pl.pallas_call(
        functools.partial(_outproj_kernel, mod_row=3),
        grid=(m // tm,),
        in_specs=[pl.BlockSpec((tm, d), row),
                  pl.BlockSpec((tm, ATTN_WIDTH), row),
                  pl.BlockSpec((tm, CONV_WIDTH), row),
                  pl.BlockSpec((1, N_MOD, d), lambda i: (i // tiles_per_batch, 0, 0)),
                  pl.BlockSpec(wa.shape, fix),
                  pl.BlockSpec(wy.shape, fix)],
        out_specs=pl.BlockSpec((tm, d), row),
        out_shape=jax.ShapeDtypeStruct((m, d), F32),
        compiler_params=pltpu.CompilerParams(
            dimension_semantics=("arbitrary",), vmem_limit_bytes=VMEM_LIMIT),
        name="outproj",
    )(x2d, a2d, y2d, mod3, wa, wy)


def _ffn_weights(w1, w3, w2):
    d, f = w1.shape
    n = f // FF_CHUNK
    w13 = jnp.concatenate([w1.reshape(d, n, FF_CHUNK), w3.reshape(d, n, FF_CHUNK)], axis=-1)
    return (w13.transpose(1, 0, 2).astype(BF16),
            w2.reshape(n, FF_CHUNK, d).astype(BF16))


def _rope_tables(seq):
    inv = 1.0 / (ROPE_THETA ** (jnp.arange(0, HEAD_DIM, 2, dtype=F32) / HEAD_DIM))
    ang = jnp.arange(seq, dtype=F32)[:, None] * inv[None, :]
    ang = jnp.concatenate([ang, ang, ang, ang], axis=-1)
    cos, sin = jnp.cos(ang), jnp.sin(ang)
    upper = (jnp.arange(LANES) % HEAD_DIM) >= HEAD_DIM // 2
    return cos, jnp.where(upper, sin, 0.0), jnp.where(upper, 0.0, -sin)


def kernel(x, c, w_ada, b_ada, w_ada_final, b_ada_final, g_ffn1, g_mix, g_ffn2, g_final,
           ffn1_w1, ffn1_w3, ffn1_w2, ffn2_w1, ffn2_w3, ffn2_w2, w_in,
           lambda_q1, lambda_k1, lambda_q2, lambda_k2, subln_g, conv_w, conv_norm_g, w_out):
    bsz, seq, d = x.shape
    depth = w_ada.shape[0]
    cos, sinp, sinm = _rope_tables(seq)
    gsz = CONV_WIDTH // CONV_GROUPS
    grp = jnp.arange(CONV_WIDTH) // gsz
    gmat = jnp.where(grp[:, None] == grp[None, :], 1.0 / gsz, 0.0).astype(BF16)
    modf3 = _ada(c, w_ada_final, b_ada_final, 1024).reshape(bsz, 2, d)

    x2d = x.reshape(bsz * seq, d)
    for l in range(depth):
        lambda_init = 0.8 - 0.6 * math.exp(-0.3 * l)
        mod3 = _ada(c, w_ada[l], b_ada[l], 1024).reshape(bsz, N_MOD, d)
        w13a, w2a = _ffn_weights(ffn1_w1[l], ffn1_w3[l], ffn1_w2[l])
        w13b, w2b = _ffn_weights(ffn2_w1[l], ffn2_w3[l], ffn2_w2[l])
        w_in_b = w_in[l].astype(BF16)
        w_main = jnp.concatenate([w_in_b[:, :2 * ATTN_WIDTH], w_in_b[:, 3 * ATTN_WIDTH:]], axis=1)
        w_vt = w_in_b[:, 2 * ATTN_WIDTH:3 * ATTN_WIDTH].T
        w_out_b = w_out[l].astype(BF16)

        x2d = _ffn(x2d, mod3, g_ffn1[l], w13a, w2a, 0, seq)
        q, k, vt, y = _inproj(x2d, mod3, g_mix[l], w_main, w_vt, cos, sinp, sinm,
                              conv_w[l], conv_norm_g[l], gmat, seq)
        a = _attn(q.reshape(bsz, seq, ATTN_WIDTH), k.reshape(bsz, seq, ATTN_WIDTH),
                  vt, lambda_q1[l], lambda_k1[l],
                  lambda_q2[l], lambda_k2[l], subln_g[l], lambda_init)
        x2d = _outproj(x2d, a.reshape(bsz * seq, ATTN_WIDTH), y, mod3,
                       w_out_b[:ATTN_WIDTH], w_out_b[ATTN_WIDTH:], seq)
        final_args = (modf3, g_final) if l == depth - 1 else None
        x2d = _ffn(x2d, mod3, g_ffn2[l], w13b, w2b, 6, seq, final_args)
    return x2d.reshape(bsz, seq, d)
```

```python
import functools
import math

import jax
import jax.numpy as jnp
import numpy as np
from jax import lax
from jax.experimental import pallas as pl
from jax.experimental.pallas import tpu as pltpu

F32 = jnp.float32
BF16 = jnp.bfloat16

LANES = 128
SUBLANES = 8

D_MODEL = 1024
ATTN_HEADS = 4
HEAD_DIM = 64
ATTN_WIDTH = ATTN_HEADS * 2 * HEAD_DIM
CONV_WIDTH = D_MODEL - ATTN_WIDTH
CONV_GROUPS = 8
CONV_K = 3
ROPE_THETA = 10000.0
NORM_EPS = 1e-6
SUBLN_EPS = 1e-5
N_MOD = 9
IN_COLS = 3 * ATTN_WIDTH + 3 * CONV_WIDTH

FF_CHUNK = 256
ROW_TILE = 512
ATTN_TILE = 512
VMEM_LIMIT = 56 * 1024 * 1024
NT_DIMS = (((1,), (1,)), ((), ()))
Q_SCALE = math.log2(math.e) / math.sqrt(HEAD_DIM)


def _silu(a):
    return a / (1.0 + jnp.exp(-a))


def _rms(x, eps):
    return x * lax.rsqrt(jnp.mean(x * x, axis=-1, keepdims=True) + eps)


def _ada_kernel(c_ref, w_ref, b_ref, o_ref):
    o_ref[...] = jnp.dot(_silu(c_ref[...]), w_ref[...],
                         preferred_element_type=F32) + b_ref[...]


def _ada(c, w, b, tn):
    bsz, d = c.shape
    n = w.shape[1]
    return pl.pallas_call(
        _ada_kernel,
        grid=(n // tn,),
        in_specs=[pl.BlockSpec((bsz, d), lambda j: (0, 0)),
                  pl.BlockSpec((d, tn), lambda j: (0, j)),
                  pl.BlockSpec((1, tn), lambda j: (0, j))],
        out_specs=pl.BlockSpec((bsz, tn), lambda j: (0, j)),
        out_shape=jax.ShapeDtypeStruct((bsz, n), F32),
        name="ada",
    )(c, w, b.reshape(1, n))


def _ffn_kernel(*refs, mod_row, mixer, final):
    refs = list(refs)
    x_ref, mod_ref, g_ref = refs[:3]
    del refs[:3]
    if mixer:
        a_ref, y_ref, wa_ref, wy_ref = refs[:4]
        del refs[:4]
    w1_ref, w3_ref, w2_ref = refs[:3]
    del refs[:3]
    if final:
        modf_ref, gf_ref = refs[:2]
        del refs[:2]
    o_ref, act_ref = refs
    x = x_ref[...]
    if mixer:
        mix = (jnp.dot(a_ref[...], wa_ref[...], preferred_element_type=F32)
               + jnp.dot(y_ref[...], wy_ref[...], preferred_element_type=F32))
        x = x + mod_ref[0, mod_row - 1:mod_row, :] * mix
    shift = mod_ref[0, mod_row:mod_row + 1, :]
    scale = mod_ref[0, mod_row + 1:mod_row + 2, :]
    gate = mod_ref[0, mod_row + 2:mod_row + 3, :]
    h = (_rms(x, NORM_EPS) * g_ref[...] * (1.0 + scale) + shift).astype(BF16)

    n_chunks = w2_ref.shape[0] // FF_CHUNK
    for c in range(n_chunks):
        cols = slice(c * FF_CHUNK, (c + 1) * FF_CHUNK)
        a = jnp.dot(h, w1_ref[:, cols], preferred_element_type=F32)
        b = jnp.dot(h, w3_ref[:, cols], preferred_element_type=F32)
        act_ref[:, cols] = (_silu(a) * b).astype(BF16)
    y = x + 0.5 * gate * jnp.dot(act_ref[...], w2_ref[...], preferred_element_type=F32)
    if final:
        y = (_rms(y, NORM_EPS) * gf_ref[...] * (1.0 + modf_ref[0, 1:2, :])
             + modf_ref[0, 0:1, :])
    o_ref[...] = y


def _ffn(x2d, mod3, g, w1, w3, w2, mod_row, rows_per_batch, mixer_args=None, final_args=None):
    m, d = x2d.shape
    tm = ROW_TILE
    tiles_per_batch = rows_per_batch // tm
    row = lambda i: (i, 0)
    fix = lambda i: (0, 0)
    per_batch = lambda i: (i // tiles_per_batch, 0, 0)
    resident = lambda w: pl.BlockSpec(w.shape, fix, pipeline_mode=pl.Buffered(1))
    in_specs = [pl.BlockSpec((tm, d), row),
                pl.BlockSpec((1, N_MOD, d), per_batch),
                pl.BlockSpec((1, d), fix)]
    args = [x2d, mod3, g.reshape(1, d)]
    if mixer_args is not None:
        a2d, y2d, wa, wy = mixer_args
        in_specs += [pl.BlockSpec((tm, a2d.shape[1]), row),
                     pl.BlockSpec((tm, y2d.shape[1]), row), resident(wa), resident(wy)]
        args += [a2d, y2d, wa, wy]
    in_specs += [resident(w1), resident(w3), resident(w2)]
    args += [w1, w3, w2]
    if final_args is not None:
        modf3, gf = final_args
        in_specs += [pl.BlockSpec((1, 2, d), per_batch), pl.BlockSpec((1, d), fix)]
        args += [modf3, gf.reshape(1, d)]
    return pl.pallas_call(
        functools.partial(_ffn_kernel, mod_row=mod_row, mixer=mixer_args is not None,
                          final=final_args is not None),
        grid=(m // tm,),
        in_specs=in_specs,
        out_specs=pl.BlockSpec((tm, d), row),
        out_shape=jax.ShapeDtypeStruct((m, d), F32),
        scratch_shapes=[pltpu.VMEM((tm, w2.shape[0]), BF16)],
        compiler_params=pltpu.CompilerParams(
            dimension_semantics=("arbitrary",), vmem_limit_bytes=VMEM_LIMIT),
        name="ffn" + ("_mixer" if mixer_args is not None else "")
        + ("_final" if final_args is not None else ""),
    )(*args)


def _inproj_kernel(x_ref, mod_ref, g_ref, w_ref, wvt_ref, cos_ref, sinp_ref, sinm_ref,
                   convw_ref, cng_ref, gmat_ref,
                   q_ref, k_ref, vt_ref, y_ref, cu_ref, *, mod_row, tiles_per_batch):
    tm = x_ref.shape[0]
    x = x_ref[...]
    shift = mod_ref[0, mod_row:mod_row + 1, :]
    scale = mod_ref[0, mod_row + 1:mod_row + 2, :]
    h = (_rms(x, NORM_EPS) * g_ref[...] * (1.0 + scale) + shift).astype(BF16)
    z = jnp.dot(h, w_ref[...], preferred_element_type=F32)
    vt_ref[0] = lax.dot_general(wvt_ref[...], h, NT_DIMS,
                                preferred_element_type=F32).astype(BF16)

    cos = cos_ref[...]
    sinp = sinp_ref[...]
    sinm = sinm_ref[...]
    half = HEAD_DIM // 2
    for j in range(ATTN_WIDTH // LANES):
        for base, out_ref, mult in ((0, q_ref, Q_SCALE), (ATTN_WIDTH, k_ref, 1.0)):
            t = z[:, base + j * LANES: base + (j + 1) * LANES]
            r = (t * cos + pltpu.roll(t, half, axis=1) * sinp
                 + pltpu.roll(t, LANES - half, axis=1) * sinm)
            out_ref[:, j * LANES:(j + 1) * LANES] = (r * mult).astype(BF16)

    c0 = 2 * ATTN_WIDTH
    gate_b = z[:, c0:c0 + CONV_WIDTH]
    cu = z[:, c0 + CONV_WIDTH:c0 + 2 * CONV_WIDTH] * z[:, c0 + 2 * CONV_WIDTH:]

    @pl.when(pl.program_id(0) % tiles_per_batch == 0)
    def _():
        cu_ref[0:SUBLANES, :] = jnp.zeros((SUBLANES, CONV_WIDTH), F32)

    cu_ref[SUBLANES:, :] = cu
    buf = cu_ref[...]
    cu_m1 = pltpu.roll(buf, 1, axis=0)[SUBLANES:, :]
    cu_m2 = pltpu.roll(buf, 2, axis=0)[SUBLANES:, :]
    cu_ref[0:SUBLANES, :] = cu[tm - SUBLANES:, :]
    w = convw_ref[...]
    y = gate_b * (w[0:1, :] * cu_m2 + w[1:2, :] * cu_m1 + w[2:3, :] * cu)
    y2 = y * y
    y2_hi = y2.astype(BF16)
    y2_lo = (y2 - y2_hi.astype(F32)).astype(BF16)
    ms = (jnp.dot(y2_hi, gmat_ref[...], preferred_element_type=F32)
          + jnp.dot(y2_lo, gmat_ref[...], preferred_element_type=F32))
    y_ref[...] = (y * lax.rsqrt(ms + NORM_EPS) * cng_ref[...]).astype(BF16)


def _inproj(x2d, mod3, g, w_main, w_vt, cos, sinp, sinm, conv_w, conv_norm_g, gmat,
            rows_per_batch):
    m, d = x2d.shape
    tm = ROW_TILE
    tiles_per_batch = rows_per_batch // tm
    row = lambda i: (i, 0)
    fix = lambda i: (0, 0)
    pos = lambda i: (i % tiles_per_batch, 0)
    out_sd = jax.ShapeDtypeStruct((m, ATTN_WIDTH), BF16)
    return pl.pallas_call(
        functools.partial(_inproj_kernel, mod_row=3, tiles_per_batch=tiles_per_batch),
        grid=(m // tm,),
        in_specs=[pl.BlockSpec((tm, d), row),
                  pl.BlockSpec((1, N_MOD, d), lambda i: (i // tiles_per_batch, 0, 0)),
                  pl.BlockSpec((1, d), fix),
                  pl.BlockSpec(w_main.shape, fix, pipeline_mode=pl.Buffered(1)),
                  pl.BlockSpec(w_vt.shape, fix, pipeline_mode=pl.Buffered(1)),
                  pl.BlockSpec((tm, LANES), pos),
                  pl.BlockSpec((tm, LANES), pos),
                  pl.BlockSpec((tm, LANES), pos),
                  pl.BlockSpec((CONV_K, CONV_WIDTH), fix),
                  pl.BlockSpec((1, CONV_WIDTH), fix),
                  pl.BlockSpec((CONV_WIDTH, CONV_WIDTH), fix)],
        out_specs=[pl.BlockSpec((tm, ATTN_WIDTH), row),
                   pl.BlockSpec((tm, ATTN_WIDTH), row),
                   pl.BlockSpec((1, ATTN_WIDTH, tm),
                                lambda i: (i // tiles_per_batch, 0, i % tiles_per_batch)),
                   pl.BlockSpec((tm, CONV_WIDTH), row)],
        out_shape=[out_sd, out_sd,
                   jax.ShapeDtypeStruct((m // rows_per_batch, ATTN_WIDTH, rows_per_batch), BF16),
                   jax.ShapeDtypeStruct((m, CONV_WIDTH), BF16)],
        scratch_shapes=[pltpu.VMEM((tm + SUBLANES, CONV_WIDTH), F32)],
        compiler_params=pltpu.CompilerParams(
            dimension_semantics=("arbitrary",), vmem_limit_bytes=VMEM_LIMIT),
        name="inproj",
    )(x2d, mod3, g.reshape(1, d), w_main, w_vt, cos, sinp, sinm, conv_w,
      conv_norm_g.reshape(1, CONV_WIDTH), gmat)


def _attn_kernel(q_ref, k_ref, vt_ref, lq1_ref, lk1_ref, lq2_ref, lk2_ref, sg_ref,
                 o_ref, sa_ref, sb_ref, m_ref, l_ref, acc_ref, *, lambda_init):
    seq = q_ref.shape[1]
    t = ATTN_TILE
    n_tiles = seq // t
    lam = (jnp.exp(jnp.sum(lq1_ref[...] * lk1_ref[...], axis=-1, keepdims=True))
           - jnp.exp(jnp.sum(lq2_ref[...] * lk2_ref[...], axis=-1, keepdims=True))
           + lambda_init)

    def q_tile(i, carry):
        q0 = pl.multiple_of(i * t, t)
        q = q_ref[0, pl.ds(q0, t), :]
        first_map = lax.broadcasted_iota(jnp.int32, (t, LANES), 1) < HEAD_DIM
        zero = jnp.zeros_like(q)
        qq = jnp.concatenate([jnp.where(first_map, q, zero),
                              jnp.where(first_map, zero, q)], axis=0)

        def scores(k0):
            return lax.dot_general(k_ref[0, pl.ds(k0, t), :], qq, NT_DIMS,
                                   preferred_element_type=F32)

        def absorb(s_buf, k0, mask=False):
            s = s_buf[...]
            if mask:
                kpos = lax.broadcasted_iota(jnp.int32, (t, 2 * t), 0)
                qpos = lax.broadcasted_iota(jnp.int32, (t, 2 * t), 1)
                qpos = jnp.where(qpos >= t, qpos - t, qpos)
                s = jnp.where(kpos <= qpos, s, -jnp.inf)
            m = m_ref[...]
            m_new = jnp.maximum(m, jnp.max(s, axis=0, keepdims=True))
            alpha = jnp.exp2(m - m_new)
            p = jnp.exp2(s - m_new)
            m_ref[...] = m_new
            l_ref[...] = alpha * l_ref[...] + jnp.sum(p, axis=0, keepdims=True)
            pv = jnp.dot(vt_ref[0, :, pl.ds(k0, t)], p.astype(BF16),
                         preferred_element_type=F32)
            acc_ref[...] = alpha * acc_ref[...] + pv

        m_ref[...] = jnp.full(m_ref.shape, -jnp.inf, F32)
        l_ref[...] = jnp.zeros_like(l_ref)
        acc_ref[...] = jnp.zeros_like(acc_ref)
        last = jnp.maximum(i - 1, 0)
        sa_ref[...] = scores(q0)
        sb_ref[...] = scores(0)
        absorb(sa_ref, q0, mask=True)

        def pair(jj, carry2):
            k_even = pl.multiple_of(2 * jj * t, t)
            k_odd = pl.multiple_of(k_even + t, t)
            sa_ref[...] = scores(k_odd)
            absorb(sb_ref, k_even)
            sb_ref[...] = scores(pl.multiple_of(jnp.minimum(2 * jj + 2, last) * t, t))
            absorb(sa_ref, k_odd)
            return carry2

        lax.fori_loop(0, i // 2, pair, 0)

        @pl.when(i % 2 == 1)
        def _():
            absorb(sb_ref, pl.multiple_of(last * t, t))

        o = acc_ref[...] / l_ref[...]
        o = o[:, :t] - lam * o[:, t:]
        o = o * lax.rsqrt(jnp.mean(o * o, axis=0, keepdims=True) + SUBLN_EPS)
        o = o * (sg_ref[...] * (1.0 - lambda_init))
        o_ref[0, pl.ds(q0, t), :] = o.T.astype(BF16)
        return carry

    lax.fori_loop(0, n_tiles, q_tile, 0)


def _attn(q, k, vt, lq1, lk1, lq2, lk2, subln_g, lambda_init):
    bsz, seq, _ = q.shape
    t = ATTN_TILE
    blk = pl.BlockSpec((1, seq, LANES), lambda b, h: (b, 0, h))
    vec = pl.BlockSpec((1, HEAD_DIM), lambda b, h: (0, 0))
    return pl.pallas_call(
        functools.partial(_attn_kernel, lambda_init=lambda_init),
        grid=(bsz, ATTN_HEADS),
        in_specs=[blk, blk,
                  pl.BlockSpec((1, LANES, seq), lambda b, h: (b, h, 0)),
                  vec, vec, vec, vec,
                  pl.BlockSpec((2 * HEAD_DIM, 1), lambda b, h: (0, 0))],
        out_specs=blk,
        out_shape=jax.ShapeDtypeStruct((bsz, seq, ATTN_WIDTH), BF16),
        scratch_shapes=[pltpu.VMEM((t, 2 * t), F32),
                        pltpu.VMEM((t, 2 * t), F32),
                        pltpu.VMEM((1, 2 * t), F32),
                        pltpu.VMEM((1, 2 * t), F32),
                        pltpu.VMEM((2 * HEAD_DIM, 2 * t), F32)],
        compiler_params=pltpu.CompilerParams(
            dimension_semantics=("arbitrary", "arbitrary"), vmem_limit_bytes=VMEM_LIMIT),
        name="attn",
    )(q, k, vt, lq1.reshape(1, -1), lk1.reshape(1, -1), lq2.reshape(1, -1),
      lk2.reshape(1, -1), subln_g.reshape(-1, 1))


def _rope_tables(seq):
    inv = 1.0 / (ROPE_THETA ** (np.arange(0, HEAD_DIM, 2, dtype=np.float64) / HEAD_DIM))
    ang = np.arange(seq, dtype=np.float64)[:, None] * inv[None, :]
    ang = np.concatenate([ang, ang, ang, ang], axis=-1)
    cos, sin = np.cos(ang), np.sin(ang)
    upper = (np.arange(LANES) % HEAD_DIM) >= HEAD_DIM // 2
    as_f32 = lambda a: jnp.asarray(a.astype(np.float32))
    return as_f32(cos), as_f32(np.where(upper, sin, 0.0)), as_f32(np.where(upper, 0.0, -sin))


def kernel(x, c, w_ada, b_ada, w_ada_final, b_ada_final, g_ffn1, g_mix, g_ffn2, g_final,
           ffn1_w1, ffn1_w3, ffn1_w2, ffn2_w1, ffn2_w3, ffn2_w2, w_in,
           lambda_q1, lambda_k1, lambda_q2, lambda_k2, subln_g, conv_w, conv_norm_g, w_out):
    bsz, seq, d = x.shape
    depth = w_ada.shape[0]
    cos, sinp, sinm = _rope_tables(seq)
    gsz = CONV_WIDTH // CONV_GROUPS
    grp = np.arange(CONV_WIDTH) // gsz
    gmat = jnp.asarray(np.where(grp[:, None] == grp[None, :], 1.0 / gsz, 0.0), dtype=BF16)
    modf3 = _ada(c, w_ada_final, b_ada_final, 1024).reshape(bsz, 2, d)
    bf = lambda w: w.astype(BF16)

    x2d = x.reshape(bsz * seq, d)
    for l in range(depth):
        lambda_init = 0.8 - 0.6 * math.exp(-0.3 * l)
        mod3 = _ada(c, w_ada[l], b_ada[l], 1024).reshape(bsz, N_MOD, d)
        w_in_b = bf(w_in[l])
        w_main = jnp.concatenate([w_in_b[:, :2 * ATTN_WIDTH], w_in_b[:, 3 * ATTN_WIDTH:]], axis=1)
        w_vt = w_in_b[:, 2 * ATTN_WIDTH:3 * ATTN_WIDTH].T
        w_out_b = bf(w_out[l])

        x2d = _ffn(x2d, mod3, g_ffn1[l], bf(ffn1_w1[l]), bf(ffn1_w3[l]), bf(ffn1_w2[l]), 0, seq)
        q, k, vt, y = _inproj(x2d, mod3, g_mix[l], w_main, w_vt, cos, sinp, sinm,
                              conv_w[l], conv_norm_g[l], gmat, seq)
        a = _attn(q.reshape(bsz, seq, ATTN_WIDTH), k.reshape(bsz, seq, ATTN_WIDTH),
                  vt, lambda_q1[l], lambda_k1[l],
                  lambda_q2[l], lambda_k2[l], subln_g[l], lambda_init)
        mixer_args = (a.reshape(bsz * seq, ATTN_WIDTH), y,
                      w_out_b[:ATTN_WIDTH], w_out_b[ATTN_WIDTH:])
        final_args = (modf3, g_final) if l == depth - 1 else None
        x2d = _ffn(x2d, mod3, g_ffn2[l], bf(ffn2_w1[l]), bf(ffn2_w3[l]), bf(ffn2_w2[l]), 6, seq,
                   mixer_args, final_args)
    return x2d.reshape(bsz, seq, d)
```

```python
import functools
import math

import jax
import jax.numpy as jnp
import numpy as np
from jax import lax
from jax.experimental import pallas as pl
from jax.experimental.pallas import tpu as pltpu

F32 = jnp.float32
BF16 = jnp.bfloat16

LANES = 128
SUBLANES = 8

D_MODEL = 1024
ATTN_HEADS = 4
HEAD_DIM = 64
ATTN_WIDTH = ATTN_HEADS * 2 * HEAD_DIM
CONV_WIDTH = D_MODEL - ATTN_WIDTH
CONV_GROUPS = 8
CONV_K = 3
ROPE_THETA = 10000.0
NORM_EPS = 1e-6
SUBLN_EPS = 1e-5
N_MOD = 9
IN_COLS = 3 * ATTN_WIDTH + 3 * CONV_WIDTH

FF_CHUNK = 256
ROW_TILE = 512
ATTN_TILE = 512
VMEM_LIMIT = 56 * 1024 * 1024
NT_DIMS = (((1,), (1,)), ((), ()))
Q_SCALE = math.log2(math.e) / math.sqrt(HEAD_DIM)


def _silu(a):
    return a / (1.0 + jnp.exp(-a))


def _rms(x, eps):
    return x * lax.rsqrt(jnp.mean(x * x, axis=-1, keepdims=True) + eps)


def _ada_kernel(c_ref, w_ref, b_ref, o_ref):
    o_ref[...] = jnp.dot(_silu(c_ref[...]), w_ref[...],
                         preferred_element_type=F32) + b_ref[...]


def _ada(c, w, b, tn):
    bsz, d = c.shape
    n = w.shape[1]
    return pl.pallas_call(
        _ada_kernel,
        grid=(n // tn,),
        in_specs=[pl.BlockSpec((bsz, d), lambda j: (0, 0)),
                  pl.BlockSpec((d, tn), lambda j: (0, j)),
                  pl.BlockSpec((1, tn), lambda j: (0, j))],
        out_specs=pl.BlockSpec((bsz, tn), lambda j: (0, j)),
        out_shape=jax.ShapeDtypeStruct((bsz, n), F32),
        name="ada",
    )(c, w, b.reshape(1, n))


def _ffn_kernel(*refs, mod_row, mixer, final):
    refs = list(refs)
    x_ref, mod_ref, g_ref = refs[:3]
    del refs[:3]
    if mixer:
        a_ref, y_ref, wa_ref, wy_ref = refs[:4]
        del refs[:4]
    w1_ref, w3_ref, w2_ref = refs[:3]
    del refs[:3]
    if final:
        modf_ref, gf_ref = refs[:2]
        del refs[:2]
    o_ref, act_ref = refs
    x = x_ref[...]
    if mixer:
        attn = jnp.concatenate([a_ref[0, hd] for hd in range(a_ref.shape[1])], axis=-1)
        mix = (jnp.dot(attn, wa_ref[...], preferred_element_type=F32)
               + jnp.dot(y_ref[...], wy_ref[...], preferred_element_type=F32))
        x = x + mod_ref[0, mod_row - 1:mod_row, :] * mix
    shift = mod_ref[0, mod_row:mod_row + 1, :]
    scale = mod_ref[0, mod_row + 1:mod_row + 2, :]
    gate = mod_ref[0, mod_row + 2:mod_row + 3, :]
    h = (_rms(x, NORM_EPS) * g_ref[...] * (1.0 + scale) + shift).astype(BF16)

    n_chunks = w2_ref.shape[0] // FF_CHUNK
    for c in range(n_chunks):
        cols = slice(c * FF_CHUNK, (c + 1) * FF_CHUNK)
        a = jnp.dot(h, w1_ref[:, cols], preferred_element_type=F32)
        b = jnp.dot(h, w3_ref[:, cols], preferred_element_type=F32)
        act_ref[:, cols] = (_silu(a) * b).astype(BF16)
    y = x + 0.5 * gate * jnp.dot(act_ref[...], w2_ref[...], preferred_element_type=F32)
    if final:
        y = (_rms(y, NORM_EPS) * gf_ref[...] * (1.0 + modf_ref[0, 1:2, :])
             + modf_ref[0, 0:1, :])
    o_ref[...] = y


def _ffn(x2d, mod3, g, w1, w3, w2, mod_row, rows_per_batch, mixer_args=None, final_args=None):
    m, d = x2d.shape
    tm = ROW_TILE
    tiles_per_batch = rows_per_batch // tm
    row = lambda i: (i, 0)
    fix = lambda i: (0, 0)
    per_batch = lambda i: (i // tiles_per_batch, 0, 0)
    resident = lambda w: pl.BlockSpec(w.shape, fix, pipeline_mode=pl.Buffered(1))
    in_specs = [pl.BlockSpec((tm, d), row),
                pl.BlockSpec((1, N_MOD, d), per_batch),
                pl.BlockSpec((1, d), fix)]
    args = [x2d, mod3, g.reshape(1, d)]
    if mixer_args is not None:
        a4d, y2d, wa, wy = mixer_args
        in_specs += [pl.BlockSpec((1, a4d.shape[1], tm, a4d.shape[3]),
                                  lambda i: (i // tiles_per_batch, 0, i % tiles_per_batch, 0)),
                     pl.BlockSpec((tm, y2d.shape[1]), row), resident(wa), resident(wy)]
        args += [a4d, y2d, wa, wy]
    in_specs += [resident(w1), resident(w3), resident(w2)]
    args += [w1, w3, w2]
    if final_args is not None:
        modf3, gf = final_args
        in_specs += [pl.BlockSpec((1, 2, d), per_batch), pl.BlockSpec((1, d), fix)]
        args += [modf3, gf.reshape(1, d)]
    return pl.pallas_call(
        functools.partial(_ffn_kernel, mod_row=mod_row, mixer=mixer_args is not None,
                          final=final_args is not None),
        grid=(m // tm,),
        in_specs=in_specs,
        out_specs=pl.BlockSpec((tm, d), row),
        out_shape=jax.ShapeDtypeStruct((m, d), F32),
        scratch_shapes=[pltpu.VMEM((tm, w2.shape[0]), BF16)],
        compiler_params=pltpu.CompilerParams(
            dimension_semantics=("arbitrary",), vmem_limit_bytes=VMEM_LIMIT),
        name="ffn" + ("_mixer" if mixer_args is not None else "")
        + ("_final" if final_args is not None else ""),
    )(*args)


def _inproj_kernel(x_ref, mod_ref, g_ref, w_ref, wvt_ref, cos_ref, sinp_ref, sinm_ref,
                   convw_ref, cng_ref, gmat_ref,
                   q_ref, k_ref, vt_ref, y_ref, cu_ref, *, mod_row, tiles_per_batch):
    tm = x_ref.shape[0]
    x = x_ref[...]
    shift = mod_ref[0, mod_row:mod_row + 1, :]
    scale = mod_ref[0, mod_row + 1:mod_row + 2, :]
    h = (_rms(x, NORM_EPS) * g_ref[...] * (1.0 + scale) + shift).astype(BF16)
    z = jnp.dot(h, w_ref[...], preferred_element_type=F32)
    vt_ref[0] = lax.dot_general(wvt_ref[...], h, NT_DIMS,
                                preferred_element_type=F32).astype(BF16)

    cos = cos_ref[...]
    sinp = sinp_ref[...]
    sinm = sinm_ref[...]
    half = HEAD_DIM // 2
    for j in range(ATTN_WIDTH // LANES):
        for base, out_ref, mult in ((0, q_ref, Q_SCALE), (ATTN_WIDTH, k_ref, 1.0)):
            t = z[:, base + j * LANES: base + (j + 1) * LANES]
            r = (t * cos + pltpu.roll(t, half, axis=1) * sinp
                 + pltpu.roll(t, LANES - half, axis=1) * sinm)
            out_ref[0, j] = (r * mult).astype(BF16)

    c0 = 2 * ATTN_WIDTH
    gate_b = z[:, c0:c0 + CONV_WIDTH]
    cu = z[:, c0 + CONV_WIDTH:c0 + 2 * CONV_WIDTH] * z[:, c0 + 2 * CONV_WIDTH:]

    @pl.when(pl.program_id(0) % tiles_per_batch == 0)
    def _():
        cu_ref[0:SUBLANES, :] = jnp.zeros((SUBLANES, CONV_WIDTH), F32)

    cu_ref[SUBLANES:, :] = cu
    buf = cu_ref[...]
    cu_m1 = pltpu.roll(buf, 1, axis=0)[SUBLANES:, :]
    cu_m2 = pltpu.roll(buf, 2, axis=0)[SUBLANES:, :]
    cu_ref[0:SUBLANES, :] = cu[tm - SUBLANES:, :]
    w = convw_ref[...]
    y = gate_b * (w[0:1, :] * cu_m2 + w[1:2, :] * cu_m1 + w[2:3, :] * cu)
    y2 = y * y
    y2_hi = y2.astype(BF16)
    y2_lo = (y2 - y2_hi.astype(F32)).astype(BF16)
    ms = (jnp.dot(y2_hi, gmat_ref[...], preferred_element_type=F32)
          + jnp.dot(y2_lo, gmat_ref[...], preferred_element_type=F32))
    y_ref[...] = (y * lax.rsqrt(ms + NORM_EPS) * cng_ref[...]).astype(BF16)


def _inproj(x2d, mod3, g, w_main, w_vt, cos, sinp, sinm, conv_w, conv_norm_g, gmat,
            rows_per_batch):
    m, d = x2d.shape
    tm = ROW_TILE
    tiles_per_batch = rows_per_batch // tm
    row = lambda i: (i, 0)
    fix = lambda i: (0, 0)
    pos = lambda i: (i % tiles_per_batch, 0)
    n_batch = m // rows_per_batch
    head_major = pl.BlockSpec((1, ATTN_HEADS, tm, LANES),
                              lambda i: (i // tiles_per_batch, 0, i % tiles_per_batch, 0))
    out_sd = jax.ShapeDtypeStruct((n_batch, ATTN_HEADS, rows_per_batch, LANES), BF16)
    return pl.pallas_call(
        functools.partial(_inproj_kernel, mod_row=3, tiles_per_batch=tiles_per_batch),
        grid=(m // tm,),
        in_specs=[pl.BlockSpec((tm, d), row),
                  pl.BlockSpec((1, N_MOD, d), lambda i: (i // tiles_per_batch, 0, 0)),
                  pl.BlockSpec((1, d), fix),
                  pl.BlockSpec(w_main.shape, fix, pipeline_mode=pl.Buffered(1)),
                  pl.BlockSpec(w_vt.shape, fix, pipeline_mode=pl.Buffered(1)),
                  pl.BlockSpec((tm, LANES), pos),
                  pl.BlockSpec((tm, LANES), pos),
                  pl.BlockSpec((tm, LANES), pos),
                  pl.BlockSpec((CONV_K, CONV_WIDTH), fix),
                  pl.BlockSpec((1, CONV_WIDTH), fix),
                  pl.BlockSpec((CONV_WIDTH, CONV_WIDTH), fix)],
        out_specs=[head_major, head_major,
                   pl.BlockSpec((1, ATTN_WIDTH, tm),
                                lambda i: (i // tiles_per_batch, 0, i % tiles_per_batch)),
                   pl.BlockSpec((tm, CONV_WIDTH), row)],
        out_shape=[out_sd, out_sd,
                   jax.ShapeDtypeStruct((n_batch, ATTN_WIDTH, rows_per_batch), BF16),
                   jax.ShapeDtypeStruct((m, CONV_WIDTH), BF16)],
        scratch_shapes=[pltpu.VMEM((tm + SUBLANES, CONV_WIDTH), F32)],
        compiler_params=pltpu.CompilerParams(
            dimension_semantics=("arbitrary",), vmem_limit_bytes=VMEM_LIMIT),
        name="inproj",
    )(x2d, mod3, g.reshape(1, d), w_main, w_vt, cos, sinp, sinm, conv_w,
      conv_norm_g.reshape(1, CONV_WIDTH), gmat)


def _attn_kernel(q_ref, k_ref, vt_ref, lq1_ref, lk1_ref, lq2_ref, lk2_ref, sg_ref,
                 o_ref, sa_ref, sb_ref, m_ref, l_ref, acc_ref, *, lambda_init):
    seq = q_ref.shape[2]
    t = ATTN_TILE
    n_tiles = seq // t
    lam = (jnp.exp(jnp.sum(lq1_ref[...] * lk1_ref[...], axis=-1, keepdims=True))
           - jnp.exp(jnp.sum(lq2_ref[...] * lk2_ref[...], axis=-1, keepdims=True))
           + lambda_init)

    def stacked_queries(i):
        q = q_ref[0, 0, pl.ds(pl.multiple_of(i * t, t), t), :]
        first_map = lax.broadcasted_iota(jnp.int32, (t, LANES), 1) < HEAD_DIM
        zero = jnp.zeros_like(q)
        return jnp.concatenate([jnp.where(first_map, q, zero),
                                jnp.where(first_map, zero, q)], axis=0)

    def tile_scores(qq, k0):
        return lax.dot_general(k_ref[0, 0, pl.ds(k0, t), :], qq, NT_DIMS,
                               preferred_element_type=F32)

    def q_tile(i, carry):
        q0 = pl.multiple_of(i * t, t)
        qq = stacked_queries(i)
        scores = functools.partial(tile_scores, qq)

        def absorb(s_buf, k0, mask=False):
            s = s_buf[...]
            if mask:
                kpos = lax.broadcasted_iota(jnp.int32, (t, 2 * t), 0)
                qpos = lax.broadcasted_iota(jnp.int32, (t, 2 * t), 1)
                qpos = jnp.where(qpos >= t, qpos - t, qpos)
                s = jnp.where(kpos <= qpos, s, -jnp.inf)
            m = m_ref[...]
            m_new = jnp.maximum(m, jnp.max(s, axis=0, keepdims=True))
            alpha = jnp.exp2(m - m_new)
            p = jnp.exp2(s - m_new)
            m_ref[...] = m_new
            l_ref[...] = alpha * l_ref[...] + jnp.sum(p, axis=0, keepdims=True)
            pv = jnp.dot(vt_ref[0, :, pl.ds(k0, t)], p.astype(BF16),
                         preferred_element_type=F32)
            acc_ref[...] = alpha * acc_ref[...] + pv

        m_ref[...] = jnp.full(m_ref.shape, -jnp.inf, F32)
        l_ref[...] = jnp.zeros_like(l_ref)
        acc_ref[...] = jnp.zeros_like(acc_ref)
        last = jnp.maximum(i - 1, 0)
        sb_ref[...] = scores(0)
        absorb(sa_ref, q0, mask=True)

        def pair(jj, carry2):
            k_even = pl.multiple_of(2 * jj * t, t)
            k_odd = pl.multiple_of(k_even + t, t)
            sa_ref[...] = scores(k_odd)
            absorb(sb_ref, k_even)
            sb_ref[...] = scores(pl.multiple_of(jnp.minimum(2 * jj + 2, last) * t, t))
            absorb(sa_ref, k_odd)
            return carry2

        lax.fori_loop(0, i // 2, pair, 0)

        @pl.when(i % 2 == 1)
        def _():
            absorb(sb_ref, pl.multiple_of(last * t, t))

        nxt = jnp.minimum(i + 1, n_tiles - 1)
        sa_ref[...] = tile_scores(stacked_queries(nxt), pl.multiple_of(nxt * t, t))
        o = acc_ref[...] / l_ref[...]
        o = o[:, :t] - lam * o[:, t:]
        o = o * lax.rsqrt(jnp.mean(o * o, axis=0, keepdims=True) + SUBLN_EPS)
        o = o * (sg_ref[...] * (1.0 - lambda_init))
        o_ref[0, 0, pl.ds(q0, t), :] = o.T.astype(BF16)
        return carry

    sa_ref[...] = tile_scores(stacked_queries(0), 0)
    lax.fori_loop(0, n_tiles, q_tile, 0)


def _attn(q, k, vt, lq1, lk1, lq2, lk2, subln_g, lambda_init):
    bsz, _, seq, _ = q.shape
    t = ATTN_TILE
    blk = pl.BlockSpec((1, 1, seq, LANES), lambda b, h: (b, h, 0, 0))
    vec = pl.BlockSpec((1, HEAD_DIM), lambda b, h: (0, 0))
    return pl.pallas_call(
        functools.partial(_attn_kernel, lambda_init=lambda_init),
        grid=(bsz, ATTN_HEADS),
        in_specs=[blk, blk,
                  pl.BlockSpec((1, LANES, seq), lambda b, h: (b, h, 0)),
                  vec, vec, vec, vec,
                  pl.BlockSpec((2 * HEAD_DIM, 1), lambda b, h: (0, 0))],
        out_specs=blk,
        out_shape=jax.ShapeDtypeStruct((bsz, ATTN_HEADS, seq, LANES), BF16),
        scratch_shapes=[pltpu.VMEM((t, 2 * t), F32),
                        pltpu.VMEM((t, 2 * t), F32),
                        pltpu.VMEM((1, 2 * t), F32),
                        pltpu.VMEM((1, 2 * t), F32),
                        pltpu.VMEM((2 * HEAD_DIM, 2 * t), F32)],
        compiler_params=pltpu.CompilerParams(
            dimension_semantics=("arbitrary", "arbitrary"), vmem_limit_bytes=VMEM_LIMIT),
        name="attn",
    )(q, k, vt, lq1.reshape(1, -1), lk1.reshape(1, -1), lq2.reshape(1, -1),
      lk2.reshape(1, -1), subln_g.reshape(-1, 1))


def _rope_tables(seq):
    inv = 1.0 / (ROPE_THETA ** (np.arange(0, HEAD_DIM, 2, dtype=np.float64) / HEAD_DIM))
    ang = np.arange(seq, dtype=np.float64)[:, None] * inv[None, :]
    ang = np.concatenate([ang, ang, ang, ang], axis=-1)
    cos, sin = np.cos(ang), np.sin(ang)
    upper = (np.arange(LANES) % HEAD_DIM) >= HEAD_DIM // 2
    as_f32 = lambda a: jnp.asarray(a.astype(np.float32))
    return as_f32(cos), as_f32(np.where(upper, sin, 0.0)), as_f32(np.where(upper, 0.0, -sin))


def kernel(x, c, w_ada, b_ada, w_ada_final, b_ada_final, g_ffn1, g_mix, g_ffn2, g_final,
           ffn1_w1, ffn1_w3, ffn1_w2, ffn2_w1, ffn2_w3, ffn2_w2, w_in,
           lambda_q1, lambda_k1, lambda_q2, lambda_k2, subln_g, conv_w, conv_norm_g, w_out):
    bsz, seq, d = x.shape
    depth = w_ada.shape[0]
    cos, sinp, sinm = _rope_tables(seq)
    gsz = CONV_WIDTH // CONV_GROUPS
    grp = np.arange(CONV_WIDTH) // gsz
    gmat = jnp.asarray(np.where(grp[:, None] == grp[None, :], 1.0 / gsz, 0.0), dtype=BF16)
    modf3 = _ada(c, w_ada_final, b_ada_final, 1024).reshape(bsz, 2, d)
    bf = lambda w: w.astype(BF16)

    x2d = x.reshape(bsz * seq, d)
    for l in range(depth):
        lambda_init = 0.8 - 0.6 * math.exp(-0.3 * l)
        mod3 = _ada(c, w_ada[l], b_ada[l], 1024).reshape(bsz, N_MOD, d)
        w_in_b = bf(w_in[l])
        w_main = jnp.concatenate([w_in_b[:, :2 * ATTN_WIDTH], w_in_b[:, 3 * ATTN_WIDTH:]], axis=1)
        w_vt = w_in_b[:, 2 * ATTN_WIDTH:3 * ATTN_WIDTH].T
        w_out_b = bf(w_out[l])

        x2d = _ffn(x2d, mod3, g_ffn1[l], bf(ffn1_w1[l]), bf(ffn1_w3[l]), bf(ffn1_w2[l]), 0, seq)
        q, k, vt, y = _inproj(x2d, mod3, g_mix[l], w_main, w_vt, cos, sinp, sinm,
                              conv_w[l], conv_norm_g[l], gmat, seq)
        a = _attn(q, k, vt, lambda_q1[l], lambda_k1[l], lambda_q2[l], lambda_k2[l],
                  subln_g[l], lambda_init)
        mixer_args = (a, y, w_out_b[:ATTN_WIDTH], w_out_b[ATTN_WIDTH:])
        final_args = (modf3, g_final) if l == depth - 1 else None
        x2d = _ffn(x2d, mod3, g_ffn2[l], bf(ffn2_w1[l]), bf(ffn2_w3[l]), bf(ffn2_w2[l]), 6, seq,
                   mixer_args, final_args)
    return x2d.reshape(bsz, seq, d)
```

```python
import functools
import math

import jax
import jax.numpy as jnp
import numpy as np
from jax import lax
from jax.experimental import pallas as pl
from jax.experimental.pallas import tpu as pltpu

F32 = jnp.float32
BF16 = jnp.bfloat16

LANES = 128
SUBLANES = 8

D_MODEL = 1024
ATTN_HEADS = 4
HEAD_DIM = 64
ATTN_WIDTH = ATTN_HEADS * 2 * HEAD_DIM
CONV_WIDTH = D_MODEL - ATTN_WIDTH
CONV_GROUPS = 8
CONV_K = 3
ROPE_THETA = 10000.0
NORM_EPS = 1e-6
SUBLN_EPS = 1e-5
N_MOD = 9
IN_COLS = 3 * ATTN_WIDTH + 3 * CONV_WIDTH

FF_CHUNK = 256
ROW_TILE = 512
ATTN_TILE = 512
ONES_ROWS = 16
VMEM_LIMIT = 56 * 1024 * 1024
NT_DIMS = (((1,), (1,)), ((), ()))
Q_SCALE = math.log2(math.e) / math.sqrt(HEAD_DIM)


def _silu(a):
    return a / (1.0 + jnp.exp(-a))


def _rms(x, eps):
    return x * lax.rsqrt(jnp.mean(x * x, axis=-1, keepdims=True) + eps)


def _ada_kernel(c_ref, w_ref, b_ref, o_ref):
    o_ref[...] = jnp.dot(_silu(c_ref[...]), w_ref[...],
                         preferred_element_type=F32) + b_ref[...]


def _ada(c, w, b, tn):
    bsz, d = c.shape
    n = w.shape[1]
    return pl.pallas_call(
        _ada_kernel,
        grid=(n // tn,),
        in_specs=[pl.BlockSpec((bsz, d), lambda j: (0, 0)),
                  pl.BlockSpec((d, tn), lambda j: (0, j)),
                  pl.BlockSpec((1, tn), lambda j: (0, j))],
        out_specs=pl.BlockSpec((bsz, tn), lambda j: (0, j)),
        out_shape=jax.ShapeDtypeStruct((bsz, n), F32),
        name="ada",
    )(c, w, b.reshape(1, n))


def _ffn_kernel(*refs, mod_row, mixer, final):
    refs = list(refs)
    x_ref, mod_ref, g_ref = refs[:3]
    del refs[:3]
    if mixer:
        a_ref, y_ref, wa_ref, wy_ref = refs[:4]
        del refs[:4]
    w1_ref, w3_ref, w2_ref = refs[:3]
    del refs[:3]
    if final:
        modf_ref, gf_ref = refs[:2]
        del refs[:2]
    o_ref, act_ref = refs
    x = x_ref[...]
    if mixer:
        attn = jnp.concatenate([a_ref[0, hd] for hd in range(a_ref.shape[1])], axis=-1)
        mix = (jnp.dot(attn, wa_ref[...], preferred_element_type=F32)
               + jnp.dot(y_ref[...], wy_ref[...], preferred_element_type=F32))
        x = x + mod_ref[0, mod_row - 1:mod_row, :] * mix
    shift = mod_ref[0, mod_row:mod_row + 1, :]
    scale = mod_ref[0, mod_row + 1:mod_row + 2, :]
    gate = mod_ref[0, mod_row + 2:mod_row + 3, :]
    h = (_rms(x, NORM_EPS) * g_ref[...] * (1.0 + scale) + shift).astype(BF16)

    n_chunks = w2_ref.shape[0] // FF_CHUNK
    for c in range(n_chunks):
        cols = slice(c * FF_CHUNK, (c + 1) * FF_CHUNK)
        a = jnp.dot(h, w1_ref[:, cols], preferred_element_type=F32)
        b = jnp.dot(h, w3_ref[:, cols], preferred_element_type=F32)
        act_ref[:, cols] = (_silu(a) * b).astype(BF16)
    y = x + 0.5 * gate * jnp.dot(act_ref[...], w2_ref[...], preferred_element_type=F32)
    if final:
        y = (_rms(y, NORM_EPS) * gf_ref[...] * (1.0 + modf_ref[0, 1:2, :])
             + modf_ref[0, 0:1, :])
    o_ref[...] = y


def _ffn(x2d, mod3, g, w1, w3, w2, mod_row, rows_per_batch, mixer_args=None, final_args=None):
    m, d = x2d.shape
    tm = ROW_TILE
    tiles_per_batch = rows_per_batch // tm
    row = lambda i: (i, 0)
    fix = lambda i: (0, 0)
    per_batch = lambda i: (i // tiles_per_batch, 0, 0)
    resident = lambda w: pl.BlockSpec(w.shape, fix, pipeline_mode=pl.Buffered(1))
    in_specs = [pl.BlockSpec((tm, d), row),
                pl.BlockSpec((1, N_MOD, d), per_batch),
                pl.BlockSpec((1, d), fix)]
    args = [x2d, mod3, g.reshape(1, d)]
    if mixer_args is not None:
        a4d, y2d, wa, wy = mixer_args
        in_specs += [pl.BlockSpec((1, a4d.shape[1], tm, a4d.shape[3]),
                                  lambda i: (i // tiles_per_batch, 0, i % tiles_per_batch, 0)),
                     pl.BlockSpec((tm, y2d.shape[1]), row), resident(wa), resident(wy)]
        args += [a4d, y2d, wa, wy]
    in_specs += [resident(w1), resident(w3), resident(w2)]
    args += [w1, w3, w2]
    if final_args is not None:
        modf3, gf = final_args
        in_specs += [pl.BlockSpec((1, 2, d), per_batch), pl.BlockSpec((1, d), fix)]
        args += [modf3, gf.reshape(1, d)]
    return pl.pallas_call(
        functools.partial(_ffn_kernel, mod_row=mod_row, mixer=mixer_args is not None,
                          final=final_args is not None),
        grid=(m // tm,),
        in_specs=in_specs,
        out_specs=pl.BlockSpec((tm, d), row),
        out_shape=jax.ShapeDtypeStruct((m, d), F32),
        scratch_shapes=[pltpu.VMEM((tm, w2.shape[0]), BF16)],
        compiler_params=pltpu.CompilerParams(
            dimension_semantics=("arbitrary",), vmem_limit_bytes=VMEM_LIMIT),
        name="ffn" + ("_mixer" if mixer_args is not None else "")
        + ("_final" if final_args is not None else ""),
    )(*args)


def _inproj_kernel(x_ref, mod_ref, g_ref, w_ref, wvt_ref, cos_ref, sinp_ref, sinm_ref,
                   convw_ref, cng_ref, gmat_ref,
                   q_ref, k_ref, vt_ref, y_ref, cu_ref, *, mod_row, tiles_per_batch):
    tm = x_ref.shape[0]
    x = x_ref[...]
    shift = mod_ref[0, mod_row:mod_row + 1, :]
    scale = mod_ref[0, mod_row + 1:mod_row + 2, :]
    h = (_rms(x, NORM_EPS) * g_ref[...] * (1.0 + scale) + shift).astype(BF16)
    z = jnp.dot(h, w_ref[...], preferred_element_type=F32)
    vt_ref[0] = lax.dot_general(wvt_ref[...], h, NT_DIMS,
                                preferred_element_type=F32).astype(BF16)

    cos = cos_ref[...]
    sinp = sinp_ref[...]
    sinm = sinm_ref[...]
    half = HEAD_DIM // 2
    for j in range(ATTN_WIDTH // LANES):
        for base, out_ref, mult in ((0, q_ref, Q_SCALE), (ATTN_WIDTH, k_ref, 1.0)):
            t = z[:, base + j * LANES: base + (j + 1) * LANES]
            r = (t * cos + pltpu.roll(t, half, axis=1) * sinp
                 + pltpu.roll(t, LANES - half, axis=1) * sinm)
            out_ref[0, j] = (r * mult).astype(BF16)

    c0 = 2 * ATTN_WIDTH
    gate_b = z[:, c0:c0 + CONV_WIDTH]
    cu = z[:, c0 + CONV_WIDTH:c0 + 2 * CONV_WIDTH] * z[:, c0 + 2 * CONV_WIDTH:]

    @pl.when(pl.program_id(0) % tiles_per_batch == 0)
    def _():
        cu_ref[0:SUBLANES, :] = jnp.zeros((SUBLANES, CONV_WIDTH), F32)

    cu_ref[SUBLANES:, :] = cu
    buf = cu_ref[...]
    cu_m1 = pltpu.roll(buf, 1, axis=0)[SUBLANES:, :]
    cu_m2 = pltpu.roll(buf, 2, axis=0)[SUBLANES:, :]
    cu_ref[0:SUBLANES, :] = cu[tm - SUBLANES:, :]
    w = convw_ref[...]
    y = gate_b * (w[0:1, :] * cu_m2 + w[1:2, :] * cu_m1 + w[2:3, :] * cu)
    y2 = y * y
    y2_hi = y2.astype(BF16)
    y2_lo = (y2 - y2_hi.astype(F32)).astype(BF16)
    ms = (jnp.dot(y2_hi, gmat_ref[...], preferred_element_type=F32)
          + jnp.dot(y2_lo, gmat_ref[...], preferred_element_type=F32))
    y_ref[...] = (y * lax.rsqrt(ms + NORM_EPS) * cng_ref[...]).astype(BF16)


def _inproj(x2d, mod3, g, w_main, w_vt, cos, sinp, sinm, conv_w, conv_norm_g, gmat,
            rows_per_batch):
    m, d = x2d.shape
    tm = ROW_TILE
    tiles_per_batch = rows_per_batch // tm
    row = lambda i: (i, 0)
    fix = lambda i: (0, 0)
    pos = lambda i: (i % tiles_per_batch, 0)
    n_batch = m // rows_per_batch
    head_major = pl.BlockSpec((1, ATTN_HEADS, tm, LANES),
                              lambda i: (i // tiles_per_batch, 0, i % tiles_per_batch, 0))
    out_sd = jax.ShapeDtypeStruct((n_batch, ATTN_HEADS, rows_per_batch, LANES), BF16)
    return pl.pallas_call(
        functools.partial(_inproj_kernel, mod_row=3, tiles_per_batch=tiles_per_batch),
        grid=(m // tm,),
        in_specs=[pl.BlockSpec((tm, d), row),
                  pl.BlockSpec((1, N_MOD, d), lambda i: (i // tiles_per_batch, 0, 0)),
                  pl.BlockSpec((1, d), fix),
                  pl.BlockSpec(w_main.shape, fix, pipeline_mode=pl.Buffered(1)),
                  pl.BlockSpec(w_vt.shape, fix, pipeline_mode=pl.Buffered(1)),
                  pl.BlockSpec((tm, LANES), pos),
                  pl.BlockSpec((tm, LANES), pos),
                  pl.BlockSpec((tm, LANES), pos),
                  pl.BlockSpec((CONV_K, CONV_WIDTH), fix),
                  pl.BlockSpec((1, CONV_WIDTH), fix),
                  pl.BlockSpec((CONV_WIDTH, CONV_WIDTH), fix)],
        out_specs=[head_major, head_major,
                   pl.BlockSpec((1, ATTN_WIDTH, tm),
                                lambda i: (i // tiles_per_batch, 0, i % tiles_per_batch)),
                   pl.BlockSpec((tm, CONV_WIDTH), row)],
        out_shape=[out_sd, out_sd,
                   jax.ShapeDtypeStruct((n_batch, ATTN_WIDTH, rows_per_batch), BF16),
                   jax.ShapeDtypeStruct((m, CONV_WIDTH), BF16)],
        scratch_shapes=[pltpu.VMEM((tm + SUBLANES, CONV_WIDTH), F32)],
        compiler_params=pltpu.CompilerParams(
            dimension_semantics=("arbitrary",), vmem_limit_bytes=VMEM_LIMIT),
        name="inproj",
    )(x2d, mod3, g.reshape(1, d), w_main, w_vt, cos, sinp, sinm, conv_w,
      conv_norm_g.reshape(1, CONV_WIDTH), gmat)


def _attn_kernel(q_ref, k_ref, vt_ref, lq1_ref, lk1_ref, lq2_ref, lk2_ref, sg_ref,
                 o_ref, sa_ref, sb_ref, cma_ref, cmb_ref, m_ref, vta_ref, acc_ref, *, lambda_init):
    seq = q_ref.shape[2]
    t = ATTN_TILE
    dv = vt_ref.shape[1]
    n_tiles = seq // t
    lam = (jnp.exp(jnp.sum(lq1_ref[...] * lk1_ref[...], axis=-1, keepdims=True))
           - jnp.exp(jnp.sum(lq2_ref[...] * lk2_ref[...], axis=-1, keepdims=True))
           + lambda_init)
    vta_ref[0:dv, :] = vt_ref[0]
    vta_ref[dv:, :] = jnp.ones((ONES_ROWS, seq), BF16)

    def stacked_queries(i):
        q = q_ref[0, 0, pl.ds(pl.multiple_of(i * t, t), t), :]
        first_map = lax.broadcasted_iota(jnp.int32, (t, LANES), 1) < HEAD_DIM
        zero = jnp.zeros_like(q)
        return jnp.concatenate([jnp.where(first_map, q, zero),
                                jnp.where(first_map, zero, q)], axis=0)

    def scores_into(s_buf, cm_buf, qq, k0):
        s = lax.dot_general(k_ref[0, 0, pl.ds(k0, t), :], qq, NT_DIMS,
                            preferred_element_type=F32)
        s_buf[...] = s
        cm_buf[...] = jnp.max(s, axis=0, keepdims=True)

    def q_tile(i, carry):
        q0 = pl.multiple_of(i * t, t)
        qq = stacked_queries(i)

        def absorb(s_buf, cm_buf, k0, mask=False):
            s = s_buf[...]
            if mask:
                kpos = lax.broadcasted_iota(jnp.int32, (t, 2 * t), 0)
                qpos = lax.broadcasted_iota(jnp.int32, (t, 2 * t), 1)
                qpos = jnp.where(qpos >= t, qpos - t, qpos)
                s = jnp.where(kpos <= qpos, s, -jnp.inf)
                cm = jnp.max(s, axis=0, keepdims=True)
            else:
                cm = cm_buf[...]
            m = m_ref[...]
            m_new = jnp.maximum(m, cm)
            alpha = jnp.exp2(m - m_new)
            p = jnp.exp2(s - m_new).astype(BF16)
            m_ref[...] = m_new
            pv = jnp.dot(vta_ref[:, pl.ds(k0, t)], p, preferred_element_type=F32)
            acc_ref[...] = alpha * acc_ref[...] + pv

        m_ref[...] = jnp.full(m_ref.shape, -jnp.inf, F32)
        acc_ref[...] = jnp.zeros_like(acc_ref)
        last = jnp.maximum(i - 1, 0)
        scores_into(sb_ref, cmb_ref, qq, 0)
        absorb(sa_ref, cma_ref, q0, mask=True)

        def pair(jj, carry2):
            k_even = pl.multiple_of(2 * jj * t, t)
            k_odd = pl.multiple_of(k_even + t, t)
            scores_into(sa_ref, cma_ref, qq, k_odd)
            absorb(sb_ref, cmb_ref, k_even)
            scores_into(sb_ref, cmb_ref, qq,
                        pl.multiple_of(jnp.minimum(2 * jj + 2, last) * t, t))
            absorb(sa_ref, cma_ref, k_odd)
            return carry2

        lax.fori_loop(0, i // 2, pair, 0)

        @pl.when(i % 2 == 1)
        def _():
            absorb(sb_ref, cmb_ref, pl.multiple_of(last * t, t))

        nxt = jnp.minimum(i + 1, n_tiles - 1)
        scores_into(sa_ref, cma_ref, stacked_queries(nxt), pl.multiple_of(nxt * t, t))
        acc = acc_ref[...]
        o = acc[:dv, :] / acc[dv:dv + 1, :]
        o = o[:, :t] - lam * o[:, t:]
        o = o * lax.rsqrt(jnp.mean(o * o, axis=0, keepdims=True) + SUBLN_EPS)
        o = o * (sg_ref[...] * (1.0 - lambda_init))
        o_ref[0, 0, pl.ds(q0, t), :] = o.T.astype(BF16)
        return carry

    scores_into(sa_ref, cma_ref, stacked_queries(0), 0)
    lax.fori_loop(0, n_tiles, q_tile, 0)


def _attn(q, k, vt, lq1, lk1, lq2, lk2, subln_g, lambda_init):
    bsz, _, seq, _ = q.shape
    t = ATTN_TILE
    blk = pl.BlockSpec((1, 1, seq, LANES), lambda b, h: (b, h, 0, 0))
    vec = pl.BlockSpec((1, HEAD_DIM), lambda b, h: (0, 0))
    return pl.pallas_call(
        functools.partial(_attn_kernel, lambda_init=lambda_init),
        grid=(bsz, ATTN_HEADS),
        in_specs=[blk, blk,
                  pl.BlockSpec((1, LANES, seq), lambda b, h: (b, h, 0)),
                  vec, vec, vec, vec,
                  pl.BlockSpec((2 * HEAD_DIM, 1), lambda b, h: (0, 0))],
        out_specs=blk,
        out_shape=jax.ShapeDtypeStruct((bsz, ATTN_HEADS, seq, LANES), BF16),
        scratch_shapes=[pltpu.VMEM((t, 2 * t), F32),
                        pltpu.VMEM((t, 2 * t), F32),
                        pltpu.VMEM((1, 2 * t), F32),
                        pltpu.VMEM((1, 2 * t), F32),
                        pltpu.VMEM((1, 2 * t), F32),
                        pltpu.VMEM((2 * HEAD_DIM + ONES_ROWS, seq), BF16),
                        pltpu.VMEM((2 * HEAD_DIM + ONES_ROWS, 2 * t), F32)],
        compiler_params=pltpu.CompilerParams(
            dimension_semantics=("arbitrary", "arbitrary"), vmem_limit_bytes=VMEM_LIMIT),
        name="attn",
    )(q, k, vt, lq1.reshape(1, -1), lk1.reshape(1, -1), lq2.reshape(1, -1),
      lk2.reshape(1, -1), subln_g.reshape(-1, 1))


def _rope_tables(seq):
    inv = 1.0 / (ROPE_THETA ** (np.arange(0, HEAD_DIM, 2, dtype=np.float64) / HEAD_DIM))
    ang = np.arange(seq, dtype=np.float64)[:, None] * inv[None, :]
    ang = np.concatenate([ang, ang, ang, ang], axis=-1)
    cos, sin = np.cos(ang), np.sin(ang)
    upper = (np.arange(LANES) % HEAD_DIM) >= HEAD_DIM // 2
    as_f32 = lambda a: jnp.asarray(a.astype(np.float32))
    return as_f32(cos), as_f32(np.where(upper, sin, 0.0)), as_f32(np.where(upper, 0.0, -sin))


def kernel(x, c, w_ada, b_ada, w_ada_final, b_ada_final, g_ffn1, g_mix, g_ffn2, g_final,
           ffn1_w1, ffn1_w3, ffn1_w2, ffn2_w1, ffn2_w3, ffn2_w2, w_in,
           lambda_q1, lambda_k1, lambda_q2, lambda_k2, subln_g, conv_w, conv_norm_g, w_out):
    bsz, seq, d = x.shape
    depth = w_ada.shape[0]
    cos, sinp, sinm = _rope_tables(seq)
    gsz = CONV_WIDTH // CONV_GROUPS
    grp = np.arange(CONV_WIDTH) // gsz
    gmat = jnp.asarray(np.where(grp[:, None] == grp[None, :], 1.0 / gsz, 0.0), dtype=BF16)
    modf3 = _ada(c, w_ada_final, b_ada_final, 1024).reshape(bsz, 2, d)
    bf = lambda w: w.astype(BF16)

    x2d = x.reshape(bsz * seq, d)
    for l in range(depth):
        lambda_init = 0.8 - 0.6 * math.exp(-0.3 * l)
        mod3 = _ada(c, w_ada[l], b_ada[l], 1024).reshape(bsz, N_MOD, d)
        w_in_b = bf(w_in[l])
        w_main = jnp.concatenate([w_in_b[:, :2 * ATTN_WIDTH], w_in_b[:, 3 * ATTN_WIDTH:]], axis=1)
        w_vt = w_in_b[:, 2 * ATTN_WIDTH:3 * ATTN_WIDTH].T
        w_out_b = bf(w_out[l])

        x2d = _ffn(x2d, mod3, g_ffn1[l], bf(ffn1_w1[l]), bf(ffn1_w3[l]), bf(ffn1_w2[l]), 0, seq)
        q, k, vt, y = _inproj(x2d, mod3, g_mix[l], w_main, w_vt, cos, sinp, sinm,
                              conv_w[l], conv_norm_g[l], gmat, seq)
        a = _attn(q, k, vt, lambda_q1[l], lambda_k1[l], lambda_q2[l], lambda_k2[l],
                  subln_g[l], lambda_init)
        mixer_args = (a, y, w_out_b[:ATTN_WIDTH], w_out_b[ATTN_WIDTH:])
        final_args = (modf3, g_final) if l == depth - 1 else None
        x2d = _ffn(x2d, mod3, g_ffn2[l], bf(ffn2_w1[l]), bf(ffn2_w3[l]), bf(ffn2_w2[l]), 6, seq,
                   mixer_args, final_args)
    return x2d.reshape(bsz, seq, d)
```

```python
import functools
import math

import jax
import jax.numpy as jnp
import numpy as np
from jax import lax
from jax.experimental import pallas as pl
from jax.experimental.pallas import tpu as pltpu

F32 = jnp.float32
BF16 = jnp.bfloat16

LANES = 128
SUBLANES = 8

D_MODEL = 1024
ATTN_HEADS = 4
HEAD_DIM = 64
ATTN_WIDTH = ATTN_HEADS * 2 * HEAD_DIM
CONV_WIDTH = D_MODEL - ATTN_WIDTH
CONV_GROUPS = 8
CONV_K = 3
ROPE_THETA = 10000.0
NORM_EPS = 1e-6
SUBLN_EPS = 1e-5
N_MOD = 9
IN_COLS = 3 * ATTN_WIDTH + 3 * CONV_WIDTH

FF_CHUNK = 256
ROW_TILE = 1024
ROW_SUBTILE = 512
ATTN_TILE = 512
ONES_ROWS = 16
VMEM_LIMIT = 56 * 1024 * 1024
NT_DIMS = (((1,), (1,)), ((), ()))
Q_SCALE = math.log2(math.e) / math.sqrt(HEAD_DIM)


def _silu(a):
    return a / (1.0 + jnp.exp(-a))


def _rms(x, eps):
    return x * lax.rsqrt(jnp.mean(x * x, axis=-1, keepdims=True) + eps)


def _ada_kernel(c_ref, w_ref, b_ref, o_ref):
    o_ref[...] = jnp.dot(_silu(c_ref[...]), w_ref[...],
                         preferred_element_type=F32) + b_ref[...]


def _ada(c, w, b, tn):
    bsz, d = c.shape
    n = w.shape[1]
    return pl.pallas_call(
        _ada_kernel,
        grid=(n // tn,),
        in_specs=[pl.BlockSpec((bsz, d), lambda j: (0, 0)),
                  pl.BlockSpec((d, tn), lambda j: (0, j)),
                  pl.BlockSpec((1, tn), lambda j: (0, j))],
        out_specs=pl.BlockSpec((bsz, tn), lambda j: (0, j)),
        out_shape=jax.ShapeDtypeStruct((bsz, n), F32),
        name="ada",
    )(c, w, b.reshape(1, n))


def _ffn_kernel(*refs, mod_row, mixer, final):
    refs = list(refs)
    x_ref, mod_ref, g_ref = refs[:3]
    del refs[:3]
    if mixer:
        a_ref, y_ref, wa_ref, wy_ref = refs[:4]
        del refs[:4]
    w1_ref, w3_ref, w2_ref = refs[:3]
    del refs[:3]
    if final:
        modf_ref, gf_ref = refs[:2]
        del refs[:2]
    o_ref, act_ref = refs
    shift = mod_ref[0, mod_row:mod_row + 1, :]
    scale = mod_ref[0, mod_row + 1:mod_row + 2, :]
    gate = mod_ref[0, mod_row + 2:mod_row + 3, :]
    n_chunks = w2_ref.shape[0] // FF_CHUNK
    for r in range(x_ref.shape[0] // ROW_SUBTILE):
        rows = slice(r * ROW_SUBTILE, (r + 1) * ROW_SUBTILE)
        x = x_ref[rows, :]
        if mixer:
            attn = jnp.concatenate([a_ref[0, hd, rows, :] for hd in range(a_ref.shape[1])],
                                   axis=-1)
            mix = (jnp.dot(attn, wa_ref[...], preferred_element_type=F32)
                   + jnp.dot(y_ref[rows, :], wy_ref[...], preferred_element_type=F32))
            x = x + mod_ref[0, mod_row - 1:mod_row, :] * mix
        h = (_rms(x, NORM_EPS) * g_ref[...] * (1.0 + scale) + shift).astype(BF16)
        for c in range(n_chunks):
            cols = slice(c * FF_CHUNK, (c + 1) * FF_CHUNK)
            a = jnp.dot(h, w1_ref[:, cols], preferred_element_type=F32)
            b = jnp.dot(h, w3_ref[:, cols], preferred_element_type=F32)
            act_ref[rows, cols] = (_silu(a) * b).astype(BF16)
        y = x + 0.5 * gate * jnp.dot(act_ref[rows, :], w2_ref[...],
                                     preferred_element_type=F32)
        if final:
            y = (_rms(y, NORM_EPS) * gf_ref[...] * (1.0 + modf_ref[0, 1:2, :])
                 + modf_ref[0, 0:1, :])
        o_ref[rows, :] = y


def _ffn(x2d, mod3, g, w1, w3, w2, mod_row, rows_per_batch, mixer_args=None, final_args=None):
    m, d = x2d.shape
    tm = ROW_TILE
    tiles_per_batch = rows_per_batch // tm
    row = lambda i: (i, 0)
    fix = lambda i: (0, 0)
    per_batch = lambda i: (i // tiles_per_batch, 0, 0)
    resident = lambda w: pl.BlockSpec(w.shape, fix, pipeline_mode=pl.Buffered(1))
    in_specs = [pl.BlockSpec((tm, d), row),
                pl.BlockSpec((1, N_MOD, d), per_batch),
                pl.BlockSpec((1, d), fix)]
    args = [x2d, mod3, g.reshape(1, d)]
    if mixer_args is not None:
        a4d, y2d, wa, wy = mixer_args
        in_specs += [pl.BlockSpec((1, a4d.shape[1], tm, a4d.shape[3]),
                                  lambda i: (i // tiles_per_batch, 0, i % tiles_per_batch, 0)),
                     pl.BlockSpec((tm, y2d.shape[1]), row), resident(wa), resident(wy)]
        args += [a4d, y2d, wa, wy]
    in_specs += [resident(w1), resident(w3), resident(w2)]
    args += [w1, w3, w2]
    if final_args is not None:
        modf3, gf = final_args
        in_specs += [pl.BlockSpec((1, 2, d), per_batch), pl.BlockSpec((1, d), fix)]
        args += [modf3, gf.reshape(1, d)]
    return pl.pallas_call(
        functools.partial(_ffn_kernel, mod_row=mod_row, mixer=mixer_args is not None,
                          final=final_args is not None),
        grid=(m // tm,),
        in_specs=in_specs,
        out_specs=pl.BlockSpec((tm, d), row),
        out_shape=jax.ShapeDtypeStruct((m, d), F32),
        scratch_shapes=[pltpu.VMEM((tm, w2.shape[0]), BF16)],
        compiler_params=pltpu.CompilerParams(
            dimension_semantics=("arbitrary",), vmem_limit_bytes=VMEM_LIMIT),
        name="ffn" + ("_mixer" if mixer_args is not None else "")
        + ("_final" if final_args is not None else ""),
    )(*args)


def _inproj_kernel(x_ref, mod_ref, g_ref, w_ref, wvt_ref, cos_ref, sinp_ref, sinm_ref,
                   convw_ref, cng_ref, gmat_ref,
                   q_ref, k_ref, vt_ref, y_ref, halo_ref, *, mod_row, tiles_per_batch):
    shift = mod_ref[0, mod_row:mod_row + 1, :]
    scale = mod_ref[0, mod_row + 1:mod_row + 2, :]
    half = HEAD_DIM // 2
    w = convw_ref[...]

    @pl.when(pl.program_id(0) % tiles_per_batch == 0)
    def _():
        halo_ref[...] = jnp.zeros_like(halo_ref)

    prev = halo_ref[...]
    for r in range(x_ref.shape[0] // ROW_SUBTILE):
        rows = slice(r * ROW_SUBTILE, (r + 1) * ROW_SUBTILE)
        h = (_rms(x_ref[rows, :], NORM_EPS) * g_ref[...] * (1.0 + scale) + shift).astype(BF16)
        z = jnp.dot(h, w_ref[...], preferred_element_type=F32)
        vt_ref[0, :, rows] = lax.dot_general(wvt_ref[...], h, NT_DIMS,
                                             preferred_element_type=F32).astype(BF16)
        cos = cos_ref[rows, :]
        sinp = sinp_ref[rows, :]
        sinm = sinm_ref[rows, :]
        for j in range(ATTN_WIDTH // LANES):
            for base, out_ref, mult in ((0, q_ref, Q_SCALE), (ATTN_WIDTH, k_ref, 1.0)):
                t = z[:, base + j * LANES: base + (j + 1) * LANES]
                rot = (t * cos + pltpu.roll(t, half, axis=1) * sinp
                       + pltpu.roll(t, LANES - half, axis=1) * sinm)
                out_ref[0, j, rows, :] = (rot * mult).astype(BF16)

        c0 = 2 * ATTN_WIDTH
        gate_b = z[:, c0:c0 + CONV_WIDTH]
        cu = z[:, c0 + CONV_WIDTH:c0 + 2 * CONV_WIDTH] * z[:, c0 + 2 * CONV_WIDTH:]
        buf = jnp.concatenate([prev, cu], axis=0)
        cu_m1 = pltpu.roll(buf, 1, axis=0)[SUBLANES:, :]
        cu_m2 = pltpu.roll(buf, 2, axis=0)[SUBLANES:, :]
        prev = cu[ROW_SUBTILE - SUBLANES:, :]
        y = gate_b * (w[0:1, :] * cu_m2 + w[1:2, :] * cu_m1 + w[2:3, :] * cu)
        y2 = y * y
        y2_hi = y2.astype(BF16)
        y2_lo = (y2 - y2_hi.astype(F32)).astype(BF16)
        ms = (jnp.dot(y2_hi, gmat_ref[...], preferred_element_type=F32)
              + jnp.dot(y2_lo, gmat_ref[...], preferred_element_type=F32))
        y_ref[rows, :] = (y * lax.rsqrt(ms + NORM_EPS) * cng_ref[...]).astype(BF16)
    halo_ref[...] = prev


def _inproj(x2d, mod3, g, w_main, w_vt, cos, sinp, sinm, conv_w, conv_norm_g, gmat,
            rows_per_batch):
    m, d = x2d.shape
    tm = ROW_TILE
    tiles_per_batch = rows_per_batch // tm
    row = lambda i: (i, 0)
    fix = lambda i: (0, 0)
    pos = lambda i: (i % tiles_per_batch, 0)
    n_batch = m // rows_per_batch
    head_major = pl.BlockSpec((1, ATTN_HEADS, tm, LANES),
                              lambda i: (i // tiles_per_batch, 0, i % tiles_per_batch, 0))
    out_sd = jax.ShapeDtypeStruct((n_batch, ATTN_HEADS, rows_per_batch, LANES), BF16)
    return pl.pallas_call(
        functools.partial(_inproj_kernel, mod_row=3, tiles_per_batch=tiles_per_batch),
        grid=(m // tm,),
        in_specs=[pl.BlockSpec((tm, d), row),
                  pl.BlockSpec((1, N_MOD, d), lambda i: (i // tiles_per_batch, 0, 0)),
                  pl.BlockSpec((1, d), fix),
                  pl.BlockSpec(w_main.shape, fix, pipeline_mode=pl.Buffered(1)),
                  pl.BlockSpec(w_vt.shape, fix, pipeline_mode=pl.Buffered(1)),
                  pl.BlockSpec((tm, LANES), pos),
                  pl.BlockSpec((tm, LANES), pos),
                  pl.BlockSpec((tm, LANES), pos),
                  pl.BlockSpec((CONV_K, CONV_WIDTH), fix),
                  pl.BlockSpec((1, CONV_WIDTH), fix),
                  pl.BlockSpec((CONV_WIDTH, CONV_WIDTH), fix)],
        out_specs=[head_major, head_major,
                   pl.BlockSpec((1, ATTN_WIDTH, tm),
                                lambda i: (i // tiles_per_batch, 0, i % tiles_per_batch)),
                   pl.BlockSpec((tm, CONV_WIDTH), row)],
        out_shape=[out_sd, out_sd,
                   jax.ShapeDtypeStruct((n_batch, ATTN_WIDTH, rows_per_batch), BF16),
                   jax.ShapeDtypeStruct((m, CONV_WIDTH), BF16)],
        scratch_shapes=[pltpu.VMEM((SUBLANES, CONV_WIDTH), F32)],
        compiler_params=pltpu.CompilerParams(
            dimension_semantics=("arbitrary",), vmem_limit_bytes=VMEM_LIMIT),
        name="inproj",
    )(x2d, mod3, g.reshape(1, d), w_main, w_vt, cos, sinp, sinm, conv_w,
      conv_norm_g.reshape(1, CONV_WIDTH), gmat)


def _attn_kernel(q_ref, k_ref, vt_ref, lq1_ref, lk1_ref, lq2_ref, lk2_ref, sg_ref,
                 o_ref, sa_ref, sb_ref, cma_ref, cmb_ref, m_ref, vta_ref, acc_ref, *, lambda_init):
    seq = q_ref.shape[2]
    t = ATTN_TILE
    dv = vt_ref.shape[1]
    n_tiles = seq // t
    lam = (jnp.exp(jnp.sum(lq1_ref[...] * lk1_ref[...], axis=-1, keepdims=True))
           - jnp.exp(jnp.sum(lq2_ref[...] * lk2_ref[...], axis=-1, keepdims=True))
           + lambda_init)
    vta_ref[0:dv, :] = vt_ref[0]
    vta_ref[dv:, :] = jnp.ones((ONES_ROWS, seq), BF16)

    def stacked_queries(i):
        q = q_ref[0, 0, pl.ds(pl.multiple_of(i * t, t), t), :]
        first_map = lax.broadcasted_iota(jnp.int32, (t, LANES), 1) < HEAD_DIM
        zero = jnp.zeros_like(q)
        return jnp.concatenate([jnp.where(first_map, q, zero),
                                jnp.where(first_map, zero, q)], axis=0)

    def scores_into(s_buf, cm_buf, qq, k0):
        s = lax.dot_general(k_ref[0, 0, pl.ds(k0, t), :], qq, NT_DIMS,
                            preferred_element_type=F32)
        s_buf[...] = s
        cm_buf[...] = jnp.max(s, axis=0, keepdims=True)

    def q_tile(i, carry):
        q0 = pl.multiple_of(i * t, t)
        qq = stacked_queries(i)

        def absorb(s_buf, cm_buf, k0, mask=False):
            s = s_buf[...]
            if mask:
                kpos = lax.broadcasted_iota(jnp.int32, (t, 2 * t), 0)
                qpos = lax.broadcasted_iota(jnp.int32, (t, 2 * t), 1)
                qpos = jnp.where(qpos >= t, qpos - t, qpos)
                s = jnp.where(kpos <= qpos, s, -jnp.inf)
                cm = jnp.max(s, axis=0, keepdims=True)
            else:
                cm = cm_buf[...]
            m = m_ref[...]
            m_new = jnp.maximum(m, cm)
            alpha = jnp.exp2(m - m_new)
            p = jnp.exp2(s - m_new).astype(BF16)
            m_ref[...] = m_new
            pv = jnp.dot(vta_ref[:, pl.ds(k0, t)], p, preferred_element_type=F32)
            acc_ref[...] = alpha * acc_ref[...] + pv

        m_ref[...] = jnp.full(m_ref.shape, -jnp.inf, F32)
        acc_ref[...] = jnp.zeros_like(acc_ref)
        last = jnp.maximum(i - 1, 0)
        scores_into(sb_ref, cmb_ref, qq, 0)
        absorb(sa_ref, cma_ref, q0, mask=True)

        def pair(jj, carry2):
            k_even = pl.multiple_of(2 * jj * t, t)
            k_odd = pl.multiple_of(k_even + t, t)
            scores_into(sa_ref, cma_ref, qq, k_odd)
            absorb(sb_ref, cmb_ref, k_even)
            scores_into(sb_ref, cmb_ref, qq,
                        pl.multiple_of(jnp.minimum(2 * jj + 2, last) * t, t))
            absorb(sa_ref, cma_ref, k_odd)
            return carry2

        lax.fori_loop(0, i // 2, pair, 0)

        @pl.when(i % 2 == 1)
        def _():
            absorb(sb_ref, cmb_ref, pl.multiple_of(last * t, t))

        nxt = jnp.minimum(i + 1, n_tiles - 1)
        scores_into(sa_ref, cma_ref, stacked_queries(nxt), pl.multiple_of(nxt * t, t))
        acc = acc_ref[...]
        o = acc[:dv, :] / acc[dv:dv + 1, :]
        o = o[:, :t] - lam * o[:, t:]
        o = o * lax.rsqrt(jnp.mean(o * o, axis=0, keepdims=True) + SUBLN_EPS)
        o = o * (sg_ref[...] * (1.0 - lambda_init))
        o_ref[0, 0, pl.ds(q0, t), :] = o.T.astype(BF16)
        return carry

    scores_into(sa_ref, cma_ref, stacked_queries(0), 0)
    lax.fori_loop(0, n_tiles, q_tile, 0)


def _attn(q, k, vt, lq1, lk1, lq2, lk2, subln_g, lambda_init):
    bsz, _, seq, _ = q.shape
    t = ATTN_TILE
    blk = pl.BlockSpec((1, 1, seq, LANES), lambda b, h: (b, h, 0, 0))
    vec = pl.BlockSpec((1, HEAD_DIM), lambda b, h: (0, 0))
    return pl.pallas_call(
        functools.partial(_attn_kernel, lambda_init=lambda_init),
        grid=(bsz, ATTN_HEADS),
        in_specs=[blk, blk,
                  pl.BlockSpec((1, LANES, seq), lambda b, h: (b, h, 0)),
                  vec, vec, vec, vec,
                  pl.BlockSpec((2 * HEAD_DIM, 1), lambda b, h: (0, 0))],
        out_specs=blk,
        out_shape=jax.ShapeDtypeStruct((bsz, ATTN_HEADS, seq, LANES), BF16),
        scratch_shapes=[pltpu.VMEM((t, 2 * t), F32),
                        pltpu.VMEM((t, 2 * t), F32),
                        pltpu.VMEM((1, 2 * t), F32),
                        pltpu.VMEM((1, 2 * t), F32),
                        pltpu.VMEM((1, 2 * t), F32),
                        pltpu.VMEM((2 * HEAD_DIM + ONES_ROWS, seq), BF16),
                        pltpu.VMEM((2 * HEAD_DIM + ONES_ROWS, 2 * t), F32)],
        compiler_params=pltpu.CompilerParams(
            dimension_semantics=("arbitrary", "arbitrary"), vmem_limit_bytes=VMEM_LIMIT),
        name="attn",
    )(q, k, vt, lq1.reshape(1, -1), lk1.reshape(1, -1), lq2.reshape(1, -1),
      lk2.reshape(1, -1), subln_g.reshape(-1, 1))


def _rope_tables(seq):
    inv = 1.0 / (ROPE_THETA ** (np.arange(0, HEAD_DIM, 2, dtype=np.float64) / HEAD_DIM))
    ang = np.arange(seq, dtype=np.float64)[:, None] * inv[None, :]
    ang = np.concatenate([ang, ang, ang, ang], axis=-1)
    cos, sin = np.cos(ang), np.sin(ang)
    upper = (np.arange(LANES) % HEAD_DIM) >= HEAD_DIM // 2
    as_f32 = lambda a: jnp.asarray(a.astype(np.float32))
    return as_f32(cos), as_f32(np.where(upper, sin, 0.0)), as_f32(np.where(upper, 0.0, -sin))


def kernel(x, c, w_ada, b_ada, w_ada_final, b_ada_final, g_ffn1, g_mix, g_ffn2, g_final,
           ffn1_w1, ffn1_w3, ffn1_w2, ffn2_w1, ffn2_w3, ffn2_w2, w_in,
           lambda_q1, lambda_k1, lambda_q2, lambda_k2, subln_g, conv_w, conv_norm_g, w_out):
    bsz, seq, d = x.shape
    depth = w_ada.shape[0]
    cos, sinp, sinm = _rope_tables(seq)
    gsz = CONV_WIDTH // CONV_GROUPS
    grp = np.arange(CONV_WIDTH) // gsz
    gmat = jnp.asarray(np.where(grp[:, None] == grp[None, :], 1.0 / gsz, 0.0), dtype=BF16)
    modf3 = _ada(c, w_ada_final, b_ada_final, 1024).reshape(bsz, 2, d)
    bf = lambda w: w.astype(BF16)

    x2d = x.reshape(bsz * seq, d)
    for l in range(depth):
        lambda_init = 0.8 - 0.6 * math.exp(-0.3 * l)
        mod3 = _ada(c, w_ada[l], b_ada[l], 1024).reshape(bsz, N_MOD, d)
        w_in_b = bf(w_in[l])
        w_main = jnp.concatenate([w_in_b[:, :2 * ATTN_WIDTH], w_in_b[:, 3 * ATTN_WIDTH:]], axis=1)
        w_vt = w_in_b[:, 2 * ATTN_WIDTH:3 * ATTN_WIDTH].T
        w_out_b = bf(w_out[l])

        x2d = _ffn(x2d, mod3, g_ffn1[l], bf(ffn1_w1[l]), bf(ffn1_w3[l]), bf(ffn1_w2[l]), 0, seq)
        q, k, vt, y = _inproj(x2d, mod3, g_mix[l], w_main, w_vt, cos, sinp, sinm,
                              conv_w[l], conv_norm_g[l], gmat, seq)
        a = _attn(q, k, vt, lambda_q1[l], lambda_k1[l], lambda_q2[l], lambda_k2[l],
                  subln_g[l], lambda_init)
        mixer_args = (a, y, w_out_b[:ATTN_WIDTH], w_out_b[ATTN_WIDTH:])
        final_args = (modf3, g_final) if l == depth - 1 else None
        x2d = _ffn(x2d, mod3, g_ffn2[l], bf(ffn2_w1[l]), bf(ffn2_w3[l]), bf(ffn2_w2[l]), 6, seq,
                   mixer_args, final_args)
    return x2d.reshape(bsz, seq, d)
```

```python
import functools
import math

import jax
import jax.numpy as jnp
import numpy as np
from jax import lax
from jax.experimental import pallas as pl
from jax.experimental.pallas import tpu as pltpu

F32 = jnp.float32
BF16 = jnp.bfloat16

LANES = 128
SUBLANES = 8

D_MODEL = 1024
ATTN_HEADS = 4
HEAD_DIM = 64
ATTN_WIDTH = ATTN_HEADS * 2 * HEAD_DIM
CONV_WIDTH = D_MODEL - ATTN_WIDTH
CONV_GROUPS = 8
CONV_K = 3
ROPE_THETA = 10000.0
NORM_EPS = 1e-6
SUBLN_EPS = 1e-5
N_MOD = 9
IN_COLS = 3 * ATTN_WIDTH + 3 * CONV_WIDTH

FF_CHUNK = 256
ROW_TILE = 1024
ROW_SUBTILE = 512
ATTN_TILE = 512
ONES_ROWS = 16
VT_ROWS = 2 * HEAD_DIM + ONES_ROWS
VMEM_LIMIT = 56 * 1024 * 1024
NT_DIMS = (((1,), (1,)), ((), ()))
Q_SCALE = math.log2(math.e) / math.sqrt(HEAD_DIM)


def _silu(a):
    return a / (1.0 + jnp.exp(-a))


def _rms(x, eps):
    return x * lax.rsqrt(jnp.mean(x * x, axis=-1, keepdims=True) + eps)


def _ada_kernel(c_ref, w_ref, b_ref, o_ref):
    o_ref[...] = jnp.dot(_silu(c_ref[...]), w_ref[...],
                         preferred_element_type=F32) + b_ref[...]


def _ada(c, w, b, tn):
    bsz, d = c.shape
    n = w.shape[1]
    return pl.pallas_call(
        _ada_kernel,
        grid=(n // tn,),
        in_specs=[pl.BlockSpec((bsz, d), lambda j: (0, 0)),
                  pl.BlockSpec((d, tn), lambda j: (0, j)),
                  pl.BlockSpec((1, tn), lambda j: (0, j))],
        out_specs=pl.BlockSpec((bsz, tn), lambda j: (0, j)),
        out_shape=jax.ShapeDtypeStruct((bsz, n), F32),
        name="ada",
    )(c, w, b.reshape(1, n))


def _ffn_kernel(*refs, mod_row, mixer, final):
    refs = list(refs)
    x_ref, mod_ref, g_ref = refs[:3]
    del refs[:3]
    if mixer:
        a_ref, y_ref, wa_ref, wy_ref = refs[:4]
        del refs[:4]
    w1_ref, w3_ref, w2_ref = refs[:3]
    del refs[:3]
    if final:
        modf_ref, gf_ref = refs[:2]
        del refs[:2]
    o_ref, act_ref = refs
    shift = mod_ref[0, mod_row:mod_row + 1, :]
    scale = mod_ref[0, mod_row + 1:mod_row + 2, :]
    gate = mod_ref[0, mod_row + 2:mod_row + 3, :]
    n_chunks = w2_ref.shape[0] // FF_CHUNK
    for r in range(x_ref.shape[0] // ROW_SUBTILE):
        rows = slice(r * ROW_SUBTILE, (r + 1) * ROW_SUBTILE)
        x = x_ref[rows, :]
        if mixer:
            attn = jnp.concatenate([a_ref[0, hd, rows, :] for hd in range(a_ref.shape[1])],
                                   axis=-1)
            mix = (jnp.dot(attn, wa_ref[...], preferred_element_type=F32)
                   + jnp.dot(y_ref[rows, :], wy_ref[...], preferred_element_type=F32))
            x = x + mod_ref[0, mod_row - 1:mod_row, :] * mix
        h = (_rms(x, NORM_EPS) * g_ref[...] * (1.0 + scale) + shift).astype(BF16)
        for c in range(n_chunks):
            cols = slice(c * FF_CHUNK, (c + 1) * FF_CHUNK)
            a = jnp.dot(h, w1_ref[:, cols], preferred_element_type=F32)
            b = jnp.dot(h, w3_ref[:, cols], preferred_element_type=F32)
            act_ref[rows, cols] = (_silu(a) * b).astype(BF16)
        y = x + 0.5 * gate * jnp.dot(act_ref[rows, :], w2_ref[...],
                                     preferred_element_type=F32)
        if final:
            y = (_rms(y, NORM_EPS) * gf_ref[...] * (1.0 + modf_ref[0, 1:2, :])
                 + modf_ref[0, 0:1, :])
        o_ref[rows, :] = y


def _ffn(x2d, mod3, g, w1, w3, w2, mod_row, rows_per_batch, mixer_args=None, final_args=None):
    m, d = x2d.shape
    tm = ROW_TILE
    tiles_per_batch = rows_per_batch // tm
    row = lambda i: (i, 0)
    fix = lambda i: (0, 0)
    per_batch = lambda i: (i // tiles_per_batch, 0, 0)
    resident = lambda w: pl.BlockSpec(w.shape, fix, pipeline_mode=pl.Buffered(1))
    in_specs = [pl.BlockSpec((tm, d), row),
                pl.BlockSpec((1, N_MOD, d), per_batch),
                pl.BlockSpec((1, d), fix)]
    args = [x2d, mod3, g.reshape(1, d)]
    if mixer_args is not None:
        a4d, y2d, wa, wy = mixer_args
        in_specs += [pl.BlockSpec((1, a4d.shape[1], tm, a4d.shape[3]),
                                  lambda i: (i // tiles_per_batch, 0, i % tiles_per_batch, 0)),
                     pl.BlockSpec((tm, y2d.shape[1]), row), resident(wa), resident(wy)]
        args += [a4d, y2d, wa, wy]
    in_specs += [resident(w1), resident(w3), resident(w2)]
    args += [w1, w3, w2]
    if final_args is not None:
        modf3, gf = final_args
        in_specs += [pl.BlockSpec((1, 2, d), per_batch), pl.BlockSpec((1, d), fix)]
        args += [modf3, gf.reshape(1, d)]
    return pl.pallas_call(
        functools.partial(_ffn_kernel, mod_row=mod_row, mixer=mixer_args is not None,
                          final=final_args is not None),
        grid=(m // tm,),
        in_specs=in_specs,
        out_specs=pl.BlockSpec((tm, d), row),
        out_shape=jax.ShapeDtypeStruct((m, d), F32),
        scratch_shapes=[pltpu.VMEM((tm, w2.shape[0]), BF16)],
        compiler_params=pltpu.CompilerParams(
            dimension_semantics=("arbitrary",), vmem_limit_bytes=VMEM_LIMIT),
        name="ffn" + ("_mixer" if mixer_args is not None else "")
        + ("_final" if final_args is not None else ""),
    )(*args)


def _inproj_kernel(x_ref, mod_ref, g_ref, w_ref, wvt_ref, cos_ref, sinp_ref, sinm_ref,
                   convw_ref, cng_ref, gmat_ref,
                   q_ref, k_ref, vt_ref, y_ref, halo_ref, *, mod_row, tiles_per_batch):
    shift = mod_ref[0, mod_row:mod_row + 1, :]
    scale = mod_ref[0, mod_row + 1:mod_row + 2, :]
    half = HEAD_DIM // 2
    w = convw_ref[...]

    @pl.when(pl.program_id(0) % tiles_per_batch == 0)
    def _():
        halo_ref[...] = jnp.zeros_like(halo_ref)

    prev = halo_ref[...]
    for r in range(x_ref.shape[0] // ROW_SUBTILE):
        rows = slice(r * ROW_SUBTILE, (r + 1) * ROW_SUBTILE)
        h = (_rms(x_ref[rows, :], NORM_EPS) * g_ref[...] * (1.0 + scale) + shift).astype(BF16)
        z = jnp.dot(h, w_ref[...], preferred_element_type=F32)
        vt = lax.dot_general(wvt_ref[...], h, NT_DIMS, preferred_element_type=F32).astype(BF16)
        for j in range(ATTN_HEADS):
            vt_ref[0, j * VT_ROWS:j * VT_ROWS + LANES, rows] = vt[j * LANES:(j + 1) * LANES, :]
            vt_ref[0, j * VT_ROWS + LANES:(j + 1) * VT_ROWS, rows] = jnp.ones(
                (ONES_ROWS, ROW_SUBTILE), BF16)
        cos = cos_ref[rows, :]
        sinp = sinp_ref[rows, :]
        sinm = sinm_ref[rows, :]
        for j in range(ATTN_WIDTH // LANES):
            for base, out_ref, mult in ((0, q_ref, Q_SCALE), (ATTN_WIDTH, k_ref, 1.0)):
                t = z[:, base + j * LANES: base + (j + 1) * LANES]
                rot = (t * cos + pltpu.roll(t, half, axis=1) * sinp
                       + pltpu.roll(t, LANES - half, axis=1) * sinm)
                out_ref[0, j, rows, :] = (rot * mult).astype(BF16)

        c0 = 2 * ATTN_WIDTH
        gate_b = z[:, c0:c0 + CONV_WIDTH]
        cu = z[:, c0 + CONV_WIDTH:c0 + 2 * CONV_WIDTH] * z[:, c0 + 2 * CONV_WIDTH:]
        buf = jnp.concatenate([prev, cu], axis=0)
        cu_m1 = pltpu.roll(buf, 1, axis=0)[SUBLANES:, :]
        cu_m2 = pltpu.roll(buf, 2, axis=0)[SUBLANES:, :]
        prev = cu[ROW_SUBTILE - SUBLANES:, :]
        y = gate_b * (w[0:1, :] * cu_m2 + w[1:2, :] * cu_m1 + w[2:3, :] * cu)
        y2 = y * y
        y2_hi = y2.astype(BF16)
        y2_lo = (y2 - y2_hi.astype(F32)).astype(BF16)
        ms = (jnp.dot(y2_hi, gmat_ref[...], preferred_element_type=F32)
              + jnp.dot(y2_lo, gmat_ref[...], preferred_element_type=F32))
        y_ref[rows, :] = (y * lax.rsqrt(ms + NORM_EPS) * cng_ref[...]).astype(BF16)
    halo_ref[...] = prev


def _inproj(x2d, mod3, g, w_main, w_vt, cos, sinp, sinm, conv_w, conv_norm_g, gmat,
            rows_per_batch):
    m, d = x2d.shape
    tm = ROW_TILE
    tiles_per_batch = rows_per_batch // tm
    row = lambda i: (i, 0)
    fix = lambda i: (0, 0)
    pos = lambda i: (i % tiles_per_batch, 0)
    n_batch = m // rows_per_batch
    head_major = pl.BlockSpec((1, ATTN_HEADS, tm, LANES),
                              lambda i: (i // tiles_per_batch, 0, i % tiles_per_batch, 0))
    out_sd = jax.ShapeDtypeStruct((n_batch, ATTN_HEADS, rows_per_batch, LANES), BF16)
    return pl.pallas_call(
        functools.partial(_inproj_kernel, mod_row=3, tiles_per_batch=tiles_per_batch),
        grid=(m // tm,),
        in_specs=[pl.BlockSpec((tm, d), row),
                  pl.BlockSpec((1, N_MOD, d), lambda i: (i // tiles_per_batch, 0, 0)),
                  pl.BlockSpec((1, d), fix),
                  pl.BlockSpec(w_main.shape, fix, pipeline_mode=pl.Buffered(1)),
                  pl.BlockSpec(w_vt.shape, fix, pipeline_mode=pl.Buffered(1)),
                  pl.BlockSpec((tm, LANES), pos),
                  pl.BlockSpec((tm, LANES), pos),
                  pl.BlockSpec((tm, LANES), pos),
                  pl.BlockSpec((CONV_K, CONV_WIDTH), fix),
                  pl.BlockSpec((1, CONV_WIDTH), fix),
                  pl.BlockSpec((CONV_WIDTH, CONV_WIDTH), fix)],
        out_specs=[head_major, head_major,
                   pl.BlockSpec((1, ATTN_HEADS * VT_ROWS, tm),
                                lambda i: (i // tiles_per_batch, 0, i % tiles_per_batch)),
                   pl.BlockSpec((tm, CONV_WIDTH), row)],
        out_shape=[out_sd, out_sd,
                   jax.ShapeDtypeStruct((n_batch, ATTN_HEADS * VT_ROWS, rows_per_batch), BF16),
                   jax.ShapeDtypeStruct((m, CONV_WIDTH), BF16)],
        scratch_shapes=[pltpu.VMEM((SUBLANES, CONV_WIDTH), F32)],
        compiler_params=pltpu.CompilerParams(
            dimension_semantics=("arbitrary",), vmem_limit_bytes=VMEM_LIMIT),
        name="inproj",
    )(x2d, mod3, g.reshape(1, d), w_main, w_vt, cos, sinp, sinm, conv_w,
      conv_norm_g.reshape(1, CONV_WIDTH), gmat)


def _attn_kernel(q_ref, k_ref, vta_ref, lq1_ref, lk1_ref, lq2_ref, lk2_ref, sg_ref,
                 o_ref, sa_ref, sb_ref, cma_ref, cmb_ref, m_ref, acc_ref, *, lambda_init):
    seq = q_ref.shape[2]
    t = ATTN_TILE
    dv = 2 * HEAD_DIM
    n_tiles = seq // t
    lam = (jnp.exp(jnp.sum(lq1_ref[...] * lk1_ref[...], axis=-1, keepdims=True))
           - jnp.exp(jnp.sum(lq2_ref[...] * lk2_ref[...], axis=-1, keepdims=True))
           + lambda_init)

    def stacked_queries(i):
        q = q_ref[0, 0, pl.ds(pl.multiple_of(i * t, t), t), :]
        first_map = lax.broadcasted_iota(jnp.int32, (t, LANES), 1) < HEAD_DIM
        zero = jnp.zeros_like(q)
        return jnp.concatenate([jnp.where(first_map, q, zero),
                                jnp.where(first_map, zero, q)], axis=0)

    def scores_into(s_buf, cm_buf, qq, k0):
        s = lax.dot_general(k_ref[0, 0, pl.ds(k0, t), :], qq, NT_DIMS,
                            preferred_element_type=F32)
        s_buf[...] = s
        cm_buf[...] = jnp.max(s, axis=0, keepdims=True)

    def absorb(s_buf, cm_buf, k0, mask=False):
        s = s_buf[...]
        if mask:
            kpos = lax.broadcasted_iota(jnp.int32, (t, 2 * t), 0)
            qpos = lax.broadcasted_iota(jnp.int32, (t, 2 * t), 1)
            qpos = jnp.where(qpos >= t, qpos - t, qpos)
            s = jnp.where(kpos <= qpos, s, -jnp.inf)
            cm = jnp.max(s, axis=0, keepdims=True)
        else:
            cm = cm_buf[...]
        m = m_ref[...]
        m_new = jnp.maximum(m, cm)
        alpha = jnp.exp2(m - m_new)
        p = jnp.exp2(s - m_new).astype(BF16)
        m_ref[...] = m_new
        pv = jnp.dot(vta_ref[0, :, pl.ds(k0, t)], p, preferred_element_type=F32)
        acc_ref[...] = alpha * acc_ref[...] + pv

    def q_tile(i, odd):
        q0 = pl.multiple_of(i * t, t)
        qq = stacked_queries(i)
        qq_next = stacked_queries(jnp.minimum(i + 1, n_tiles - 1))
        m_ref[...] = jnp.full(m_ref.shape, -jnp.inf, F32)
        acc_ref[...] = jnp.zeros_like(acc_ref)
        if not odd:
            scores_into(sb_ref, cmb_ref, jnp.where(i == 0, qq_next, qq), 0)
        absorb(sa_ref, cma_ref, q0, mask=True)

        def pair(jj, carry):
            k_even = pl.multiple_of(2 * jj * t, t)
            k_odd = pl.multiple_of(k_even + t, t)
            scores_into(sa_ref, cma_ref, qq, k_odd)
            absorb(sb_ref, cmb_ref, k_even)
            if odd:
                scores_into(sb_ref, cmb_ref, qq, pl.multiple_of(k_odd + t, t))
            else:
                done = 2 * jj + 2 >= i
                scores_into(sb_ref, cmb_ref, jnp.where(done, qq_next, qq),
                            pl.multiple_of(jnp.where(done, 0, 2 * jj + 2) * t, t))
            absorb(sa_ref, cma_ref, k_odd)
            return carry

        lax.fori_loop(0, i // 2, pair, 0)
        nxt = jnp.minimum(i + 1, n_tiles - 1)
        scores_into(sa_ref, cma_ref, qq_next, pl.multiple_of(nxt * t, t))
        if odd:
            absorb(sb_ref, cmb_ref, pl.multiple_of((i - 1) * t, t))
        acc = acc_ref[...]
        o = acc[:dv, :] / acc[dv:dv + 1, :]
        o = o[:, :t] - lam * o[:, t:]
        o = o * lax.rsqrt(jnp.mean(o * o, axis=0, keepdims=True) + SUBLN_EPS)
        o = o * (sg_ref[...] * (1.0 - lambda_init))
        o_ref[0, 0, pl.ds(q0, t), :] = o.T.astype(BF16)

    def q_tile_pair(a, carry):
        q_tile(2 * a, odd=False)
        q_tile(2 * a + 1, odd=True)
        return carry

    scores_into(sa_ref, cma_ref, stacked_queries(0), 0)
    lax.fori_loop(0, n_tiles // 2, q_tile_pair, 0)


def _attn(q, k, vta, lq1, lk1, lq2, lk2, subln_g, lambda_init):
    bsz, _, seq, _ = q.shape
    t = ATTN_TILE
    assert (seq // t) % 2 == 0
    rows = vta.shape[1] // ATTN_HEADS
    blk = pl.BlockSpec((1, 1, seq, LANES), lambda b, h: (b, h, 0, 0))
    vec = pl.BlockSpec((1, HEAD_DIM), lambda b, h: (0, 0))
    return pl.pallas_call(
        functools.partial(_attn_kernel, lambda_init=lambda_init),
        grid=(bsz, ATTN_HEADS),
        in_specs=[blk, blk,
                  pl.BlockSpec((1, rows, seq), lambda b, h: (b, h, 0)),
                  vec, vec, vec, vec,
                  pl.BlockSpec((2 * HEAD_DIM, 1), lambda b, h: (0, 0))],
        out_specs=blk,
        out_shape=jax.ShapeDtypeStruct((bsz, ATTN_HEADS, seq, LANES), BF16),
        scratch_shapes=[pltpu.VMEM((t, 2 * t), F32),
                        pltpu.VMEM((t, 2 * t), F32),
                        pltpu.VMEM((1, 2 * t), F32),
                        pltpu.VMEM((1, 2 * t), F32),
                        pltpu.VMEM((1, 2 * t), F32),
                        pltpu.VMEM((rows, 2 * t), F32)],
        compiler_params=pltpu.CompilerParams(
            dimension_semantics=("arbitrary", "arbitrary"), vmem_limit_bytes=VMEM_LIMIT),
        name="attn",
    )(q, k, vta, lq1.reshape(1, -1), lk1.reshape(1, -1), lq2.reshape(1, -1),
      lk2.reshape(1, -1), subln_g.reshape(-1, 1))


def _rope_tables(seq):
    inv = 1.0 / (ROPE_THETA ** (np.arange(0, HEAD_DIM, 2, dtype=np.float64) / HEAD_DIM))
    ang = np.arange(seq, dtype=np.float64)[:, None] * inv[None, :]
    ang = np.concatenate([ang, ang, ang, ang], axis=-1)
    cos, sin = np.cos(ang), np.sin(ang)
    upper = (np.arange(LANES) % HEAD_DIM) >= HEAD_DIM // 2
    as_f32 = lambda a: jnp.asarray(a.astype(np.float32))
    return as_f32(cos), as_f32(np.where(upper, sin, 0.0)), as_f32(np.where(upper, 0.0, -sin))


def kernel(x, c, w_ada, b_ada, w_ada_final, b_ada_final, g_ffn1, g_mix, g_ffn2, g_final,
           ffn1_w1, ffn1_w3, ffn1_w2, ffn2_w1, ffn2_w3, ffn2_w2, w_in,
           lambda_q1, lambda_k1, lambda_q2, lambda_k2, subln_g, conv_w, conv_norm_g, w_out):
    bsz, seq, d = x.shape
    depth = w_ada.shape[0]
    cos, sinp, sinm = _rope_tables(seq)
    gsz = CONV_WIDTH // CONV_GROUPS
    grp = np.arange(CONV_WIDTH) // gsz
    gmat = jnp.asarray(np.where(grp[:, None] == grp[None, :], 1.0 / gsz, 0.0), dtype=BF16)
    modf3 = _ada(c, w_ada_final, b_ada_final, 1024).reshape(bsz, 2, d)
    bf = lambda w: w.astype(BF16)

    x2d = x.reshape(bsz * seq, d)
    for l in range(depth):
        lambda_init = 0.8 - 0.6 * math.exp(-0.3 * l)
        mod3 = _ada(c, w_ada[l], b_ada[l], 1024).reshape(bsz, N_MOD, d)
        w_in_b = bf(w_in[l])
        w_main = jnp.concatenate([w_in_b[:, :2 * ATTN_WIDTH], w_in_b[:, 3 * ATTN_WIDTH:]], axis=1)
        w_vt = w_in_b[:, 2 * ATTN_WIDTH:3 * ATTN_WIDTH].T
        w_out_b = bf(w_out[l])

        x2d = _ffn(x2d, mod3, g_ffn1[l], bf(ffn1_w1[l]), bf(ffn1_w3[l]), bf(ffn1_w2[l]), 0, seq)
        q, k, vt, y = _inproj(x2d, mod3, g_mix[l], w_main, w_vt, cos, sinp, sinm,
                              conv_w[l], conv_norm_g[l], gmat, seq)
        a = _attn(q, k, vt, lambda_q1[l], lambda_k1[l], lambda_q2[l], lambda_k2[l],
                  subln_g[l], lambda_init)
        mixer_args = (a, y, w_out_b[:ATTN_WIDTH], w_out_b[ATTN_WIDTH:])
        final_args = (modf3, g_final) if l == depth - 1 else None
        x2d = _ffn(x2d, mod3, g_ffn2[l], bf(ffn2_w1[l]), bf(ffn2_w3[l]), bf(ffn2_w2[l]), 6, seq,
                   mixer_args, final_args)
    return x2d.reshape(bsz, seq, d)
```

```python
import functools
import math

import jax
import jax.numpy as jnp
import numpy as np
from jax import lax
from jax.experimental import pallas as pl
from jax.experimental.pallas import tpu as pltpu

F32 = jnp.float32
BF16 = jnp.bfloat16

LANES = 128
SUBLANES = 8

D_MODEL = 1024
ATTN_HEADS = 4
HEAD_DIM = 64
ATTN_WIDTH = ATTN_HEADS * 2 * HEAD_DIM
CONV_WIDTH = D_MODEL - ATTN_WIDTH
CONV_GROUPS = 8
CONV_K = 3
ROPE_THETA = 10000.0
NORM_EPS = 1e-6
SUBLN_EPS = 1e-5
N_MOD = 9
IN_COLS = 3 * ATTN_WIDTH + 3 * CONV_WIDTH

FF_CHUNK = 256
ROW_TILE = 1024
ROW_SUBTILE = 512
ATTN_TILE = 512
ONES_ROWS = 16
VT_ROWS = 2 * HEAD_DIM + ONES_ROWS
VMEM_LIMIT = 56 * 1024 * 1024
NT_DIMS = (((1,), (1,)), ((), ()))
Q_SCALE = math.log2(math.e) / math.sqrt(HEAD_DIM)


def _silu(a):
    return a / (1.0 + jnp.exp(-a))


def _rms(x, eps):
    return x * lax.rsqrt(jnp.mean(x * x, axis=-1, keepdims=True) + eps)


def _ada_kernel(c_ref, w_ref, b_ref, o_ref):
    o_ref[...] = jnp.dot(_silu(c_ref[...]), w_ref[...],
                         preferred_element_type=F32) + b_ref[...]


def _ada(c, w, b, tn):
    bsz, d = c.shape
    n = w.shape[1]
    return pl.pallas_call(
        _ada_kernel,
        grid=(n // tn,),
        in_specs=[pl.BlockSpec((bsz, d), lambda j: (0, 0)),
                  pl.BlockSpec((d, tn), lambda j: (0, j)),
                  pl.BlockSpec((1, tn), lambda j: (0, j))],
        out_specs=pl.BlockSpec((bsz, tn), lambda j: (0, j)),
        out_shape=jax.ShapeDtypeStruct((bsz, n), F32),
        name="ada",
    )(c, w, b.reshape(1, n))


def _ffn_kernel(*refs, mod_row, mixer, final):
    refs = list(refs)
    x_ref, mod_ref, g_ref = refs[:3]
    del refs[:3]
    if mixer:
        a_ref, y_ref, wa_ref, wy_ref = refs[:4]
        del refs[:4]
    w1_ref, w3_ref, w2_ref = refs[:3]
    del refs[:3]
    if final:
        modf_ref, gf_ref = refs[:2]
        del refs[:2]
    o_ref, act_ref = refs
    shift = mod_ref[0, mod_row:mod_row + 1, :]
    scale = mod_ref[0, mod_row + 1:mod_row + 2, :]
    gate = mod_ref[0, mod_row + 2:mod_row + 3, :]
    n_chunks = w2_ref.shape[0] // FF_CHUNK
    for r in range(x_ref.shape[0] // ROW_SUBTILE):
        rows = slice(r * ROW_SUBTILE, (r + 1) * ROW_SUBTILE)
        x = x_ref[rows, :]
        if mixer:
            attn = jnp.concatenate([a_ref[0, hd, rows, :] for hd in range(a_ref.shape[1])],
                                   axis=-1)
            mix = (jnp.dot(attn, wa_ref[...], preferred_element_type=F32)
                   + jnp.dot(y_ref[rows, :], wy_ref[...], preferred_element_type=F32))
            x = x + mod_ref[0, mod_row - 1:mod_row, :] * mix
        h = (_rms(x, NORM_EPS) * g_ref[...] * (1.0 + scale) + shift).astype(BF16)
        for c in range(n_chunks):
            cols = slice(c * FF_CHUNK, (c + 1) * FF_CHUNK)
            a = jnp.dot(h, w1_ref[:, cols], preferred_element_type=F32)
            b = jnp.dot(h, w3_ref[:, cols], preferred_element_type=F32)
            act_ref[rows, cols] = (_silu(a) * b).astype(BF16)
        y = x + 0.5 * gate * jnp.dot(act_ref[rows, :], w2_ref[...],
                                     preferred_element_type=F32)
        if final:
            y = (_rms(y, NORM_EPS) * gf_ref[...] * (1.0 + modf_ref[0, 1:2, :])
                 + modf_ref[0, 0:1, :])
        o_ref[rows, :] = y


def _ffn(x2d, mod3, g, w1, w3, w2, mod_row, rows_per_batch, mixer_args=None, final_args=None):
    m, d = x2d.shape
    tm = ROW_TILE
    tiles_per_batch = rows_per_batch // tm
    row = lambda i: (i, 0)
    fix = lambda i: (0, 0)
    per_batch = lambda i: (i // tiles_per_batch, 0, 0)
    resident = lambda w: pl.BlockSpec(w.shape, fix, pipeline_mode=pl.Buffered(1))
    in_specs = [pl.BlockSpec((tm, d), row),
                pl.BlockSpec((1, N_MOD, d), per_batch),
                pl.BlockSpec((1, d), fix)]
    args = [x2d, mod3, g.reshape(1, d)]
    if mixer_args is not None:
        a4d, y2d, wa, wy = mixer_args
        in_specs += [pl.BlockSpec((1, a4d.shape[1], tm, a4d.shape[3]),
                                  lambda i: (i // tiles_per_batch, 0, i % tiles_per_batch, 0)),
                     pl.BlockSpec((tm, y2d.shape[1]), row), resident(wa), resident(wy)]
        args += [a4d, y2d, wa, wy]
    in_specs += [resident(w1), resident(w3), resident(w2)]
    args += [w1, w3, w2]
    if final_args is not None:
        modf3, gf = final_args
        in_specs += [pl.BlockSpec((1, 2, d), per_batch), pl.BlockSpec((1, d), fix)]
        args += [modf3, gf.reshape(1, d)]
    return pl.pallas_call(
        functools.partial(_ffn_kernel, mod_row=mod_row, mixer=mixer_args is not None,
                          final=final_args is not None),
        grid=(m // tm,),
        in_specs=in_specs,
        out_specs=pl.BlockSpec((tm, d), row),
        out_shape=jax.ShapeDtypeStruct((m, d), F32),
        scratch_shapes=[pltpu.VMEM((tm, w2.shape[0]), BF16)],
        compiler_params=pltpu.CompilerParams(
            dimension_semantics=("arbitrary",), vmem_limit_bytes=VMEM_LIMIT),
        name="ffn" + ("_mixer" if mixer_args is not None else "")
        + ("_final" if final_args is not None else ""),
    )(*args)


def _inproj_kernel(x_ref, mod_ref, g_ref, w_ref, wvt_ref, cos_ref, sinp_ref, sinm_ref,
                   convw_ref, cng_ref, gmat_ref,
                   q_ref, k_ref, vt_ref, y_ref, halo_ref, *, mod_row, tiles_per_batch):
    shift = mod_ref[0, mod_row:mod_row + 1, :]
    scale = mod_ref[0, mod_row + 1:mod_row + 2, :]
    half = HEAD_DIM // 2
    w = convw_ref[...]

    @pl.when(pl.program_id(0) % tiles_per_batch == 0)
    def _():
        halo_ref[...] = jnp.zeros_like(halo_ref)

    prev = halo_ref[...]
    for r in range(x_ref.shape[0] // ROW_SUBTILE):
        rows = slice(r * ROW_SUBTILE, (r + 1) * ROW_SUBTILE)
        h = (_rms(x_ref[rows, :], NORM_EPS) * g_ref[...] * (1.0 + scale) + shift).astype(BF16)
        z = jnp.dot(h, w_ref[...], preferred_element_type=F32)
        vt = lax.dot_general(wvt_ref[...], h, NT_DIMS, preferred_element_type=F32).astype(BF16)
        for j in range(ATTN_HEADS):
            vt_ref[0, j * VT_ROWS:j * VT_ROWS + LANES, rows] = vt[j * LANES:(j + 1) * LANES, :]
            vt_ref[0, j * VT_ROWS + LANES:(j + 1) * VT_ROWS, rows] = jnp.ones(
                (ONES_ROWS, ROW_SUBTILE), BF16)
        cos = cos_ref[rows, :]
        sinp = sinp_ref[rows, :]
        sinm = sinm_ref[rows, :]
        for j in range(ATTN_WIDTH // LANES):
            for base, out_ref, mult in ((0, q_ref, Q_SCALE), (ATTN_WIDTH, k_ref, 1.0)):
                t = z[:, base + j * LANES: base + (j + 1) * LANES]
                rot = (t * cos + pltpu.roll(t, half, axis=1) * sinp
                       + pltpu.roll(t, LANES - half, axis=1) * sinm)
                out_ref[0, j, rows, :] = (rot * mult).astype(BF16)

        c0 = 2 * ATTN_WIDTH
        gate_b = z[:, c0:c0 + CONV_WIDTH]
        cu = z[:, c0 + CONV_WIDTH:c0 + 2 * CONV_WIDTH] * z[:, c0 + 2 * CONV_WIDTH:]
        buf = jnp.concatenate([prev, cu], axis=0)
        cu_m1 = pltpu.roll(buf, 1, axis=0)[SUBLANES:, :]
        cu_m2 = pltpu.roll(buf, 2, axis=0)[SUBLANES:, :]
        prev = cu[ROW_SUBTILE - SUBLANES:, :]
        y = gate_b * (w[0:1, :] * cu_m2 + w[1:2, :] * cu_m1 + w[2:3, :] * cu)
        y2 = y * y
        y2_hi = y2.astype(BF16)
        y2_lo = (y2 - y2_hi.astype(F32)).astype(BF16)
        ms = (jnp.dot(y2_hi, gmat_ref[...], preferred_element_type=F32)
              + jnp.dot(y2_lo, gmat_ref[...], preferred_element_type=F32))
        y_ref[rows, :] = (y * lax.rsqrt(ms + NORM_EPS) * cng_ref[...]).astype(BF16)
    halo_ref[...] = prev


def _inproj(x2d, mod3, g, w_main, w_vt, cos, sinp, sinm, conv_w, conv_norm_g, gmat,
            rows_per_batch):
    m, d = x2d.shape
    tm = ROW_TILE
    tiles_per_batch = rows_per_batch // tm
    row = lambda i: (i, 0)
    fix = lambda i: (0, 0)
    pos = lambda i: (i % tiles_per_batch, 0)
    n_batch = m // rows_per_batch
    head_major = pl.BlockSpec((1, ATTN_HEADS, tm, LANES),
                              lambda i: (i // tiles_per_batch, 0, i % tiles_per_batch, 0))
    out_sd = jax.ShapeDtypeStruct((n_batch, ATTN_HEADS, rows_per_batch, LANES), BF16)
    return pl.pallas_call(
        functools.partial(_inproj_kernel, mod_row=3, tiles_per_batch=tiles_per_batch),
        grid=(m // tm,),
        in_specs=[pl.BlockSpec((tm, d), row),
                  pl.BlockSpec((1, N_MOD, d), lambda i: (i // tiles_per_batch, 0, 0)),
                  pl.BlockSpec((1, d), fix),
                  pl.BlockSpec(w_main.shape, fix, pipeline_mode=pl.Buffered(1)),
                  pl.BlockSpec(w_vt.shape, fix, pipeline_mode=pl.Buffered(1)),
                  pl.BlockSpec((tm, LANES), pos),
                  pl.BlockSpec((tm, LANES), pos),
                  pl.BlockSpec((tm, LANES), pos),
                  pl.BlockSpec((CONV_K, CONV_WIDTH), fix),
                  pl.BlockSpec((1, CONV_WIDTH), fix),
                  pl.BlockSpec((CONV_WIDTH, CONV_WIDTH), fix)],
        out_specs=[head_major, head_major,
                   pl.BlockSpec((1, ATTN_HEADS * VT_ROWS, tm),
                                lambda i: (i // tiles_per_batch, 0, i % tiles_per_batch)),
                   pl.BlockSpec((tm, CONV_WIDTH), row)],
        out_shape=[out_sd, out_sd,
                   jax.ShapeDtypeStruct((n_batch, ATTN_HEADS * VT_ROWS, rows_per_batch), BF16),
                   jax.ShapeDtypeStruct((m, CONV_WIDTH), BF16)],
        scratch_shapes=[pltpu.VMEM((SUBLANES, CONV_WIDTH), F32)],
        compiler_params=pltpu.CompilerParams(
            dimension_semantics=("arbitrary",), vmem_limit_bytes=VMEM_LIMIT),
        name="inproj",
    )(x2d, mod3, g.reshape(1, d), w_main, w_vt, cos, sinp, sinm, conv_w,
      conv_norm_g.reshape(1, CONV_WIDTH), gmat)


def _attn_kernel(q_ref, k_ref, vta_ref, lq1_ref, lk1_ref, lq2_ref, lk2_ref, sg_ref,
                 o_ref, sa_ref, sb_ref, cma_ref, cmb_ref, m_ref, acc_ref, *, lambda_init):
    seq = q_ref.shape[2]
    t = ATTN_TILE
    dv = 2 * HEAD_DIM
    n_tiles = seq // t
    lam = (jnp.exp(jnp.sum(lq1_ref[...] * lk1_ref[...], axis=-1, keepdims=True))
           - jnp.exp(jnp.sum(lq2_ref[...] * lk2_ref[...], axis=-1, keepdims=True))
           + lambda_init)

    def stacked_queries(i):
        q = q_ref[0, 0, pl.ds(pl.multiple_of(i * t, t), t), :]
        first_map = lax.broadcasted_iota(jnp.int32, (t, LANES), 1) < HEAD_DIM
        zero = jnp.zeros_like(q)
        return jnp.concatenate([jnp.where(first_map, q, zero),
                                jnp.where(first_map, zero, q)], axis=0)

    def scores_into(s_buf, cm_buf, qq, k0):
        s = lax.dot_general(k_ref[0, 0, pl.ds(k0, t), :], qq, NT_DIMS,
                            preferred_element_type=F32)
        s_buf[...] = s
        cm_buf[...] = jnp.max(s, axis=0, keepdims=True)

    def absorb(s_buf, cm_buf, k0, mask=False):
        s = s_buf[...]
        if mask:
            kpos = lax.broadcasted_iota(jnp.int32, (t, 2 * t), 0)
            qpos = lax.broadcasted_iota(jnp.int32, (t, 2 * t), 1)
            qpos = jnp.where(qpos >= t, qpos - t, qpos)
            s = jnp.where(kpos <= qpos, s, -jnp.inf)
            cm = jnp.max(s, axis=0, keepdims=True)
        else:
            cm = cm_buf[...]
        m = m_ref[...]
        m_new = jnp.maximum(m, cm)
        alpha = jnp.exp2(m - m_new)
        p = jnp.exp2(s - m_new).astype(BF16)
        m_ref[...] = m_new
        if mask:
            h = t // 2
            pv = jnp.dot(vta_ref[0, :, pl.ds(k0, h)], p[:h, :], preferred_element_type=F32)
            vt_hi = vta_ref[0, :, pl.ds(k0 + h, h)]
            zero = jnp.zeros((pv.shape[0], h), F32)
            pv = pv + jnp.concatenate(
                [zero, jnp.dot(vt_hi, p[h:, h:t], preferred_element_type=F32),
                 zero, jnp.dot(vt_hi, p[h:, t + h:], preferred_element_type=F32)], axis=1)
        else:
            pv = jnp.dot(vta_ref[0, :, pl.ds(k0, t)], p, preferred_element_type=F32)
        acc_ref[...] = alpha * acc_ref[...] + pv

    def q_tile(i, odd):
        q0 = pl.multiple_of(i * t, t)
        qq = stacked_queries(i)
        qq_next = stacked_queries(jnp.minimum(i + 1, n_tiles - 1))
        m_ref[...] = jnp.full(m_ref.shape, -jnp.inf, F32)
        acc_ref[...] = jnp.zeros_like(acc_ref)
        if not odd:
            scores_into(sb_ref, cmb_ref, jnp.where(i == 0, qq_next, qq), 0)
        absorb(sa_ref, cma_ref, q0, mask=True)

        def pair(jj, carry):
            k_even = pl.multiple_of(2 * jj * t, t)
            k_odd = pl.multiple_of(k_even + t, t)
            scores_into(sa_ref, cma_ref, qq, k_odd)
            absorb(sb_ref, cmb_ref, k_even)
            if odd:
                scores_into(sb_ref, cmb_ref, qq, pl.multiple_of(k_odd + t, t))
            else:
                done = 2 * jj + 2 >= i
                scores_into(sb_ref, cmb_ref, jnp.where(done, qq_next, qq),
                            pl.multiple_of(jnp.where(done, 0, 2 * jj + 2) * t, t))
            absorb(sa_ref, cma_ref, k_odd)
            return carry

        lax.fori_loop(0, i // 2, pair, 0)
        nxt = jnp.minimum(i + 1, n_tiles - 1)
        scores_into(sa_ref, cma_ref, qq_next, pl.multiple_of(nxt * t, t))
        if odd:
            absorb(sb_ref, cmb_ref, pl.multiple_of((i - 1) * t, t))
        acc = acc_ref[...]
        o = acc[:dv, :] / acc[dv:dv + 1, :]
        o = o[:, :t] - lam * o[:, t:]
        o = o * lax.rsqrt(jnp.mean(o * o, axis=0, keepdims=True) + SUBLN_EPS)
        o = o * (sg_ref[...] * (1.0 - lambda_init))
        o_ref[0, 0, pl.ds(q0, t), :] = o.T.astype(BF16)

    def q_tile_pair(a, carry):
        q_tile(2 * a, odd=False)
        q_tile(2 * a + 1, odd=True)
        return carry

    scores_into(sa_ref, cma_ref, stacked_queries(0), 0)
    lax.fori_loop(0, n_tiles // 2, q_tile_pair, 0)


def _attn(q, k, vta, lq1, lk1, lq2, lk2, subln_g, lambda_init):
    bsz, _, seq, _ = q.shape
    t = ATTN_TILE
    assert (seq // t) % 2 == 0
    rows = vta.shape[1] // ATTN_HEADS
    blk = pl.BlockSpec((1, 1, seq, LANES), lambda b, h: (b, h, 0, 0))
    vec = pl.BlockSpec((1, HEAD_DIM), lambda b, h: (0, 0))
    return pl.pallas_call(
        functools.partial(_attn_kernel, lambda_init=lambda_init),
        grid=(bsz, ATTN_HEADS),
        in_specs=[blk, blk,
                  pl.BlockSpec((1, rows, seq), lambda b, h: (b, h, 0)),
                  vec, vec, vec, vec,
                  pl.BlockSpec((2 * HEAD_DIM, 1), lambda b, h: (0, 0))],
        out_specs=blk,
        out_shape=jax.ShapeDtypeStruct((bsz, ATTN_HEADS, seq, LANES), BF16),
        scratch_shapes=[pltpu.VMEM((t, 2 * t), F32),
                        pltpu.VMEM((t, 2 * t), F32),
                        pltpu.VMEM((1, 2 * t), F32),
                        pltpu.VMEM((1, 2 * t), F32),
                        pltpu.VMEM((1, 2 * t), F32),
                        pltpu.VMEM((rows, 2 * t), F32)],
        compiler_params=pltpu.CompilerParams(
            dimension_semantics=("arbitrary", "arbitrary"), vmem_limit_bytes=VMEM_LIMIT),
        name="attn",
    )(q, k, vta, lq1.reshape(1, -1), lk1.reshape(1, -1), lq2.reshape(1, -1),
      lk2.reshape(1, -1), subln_g.reshape(-1, 1))


def _rope_tables(seq):
    inv = 1.0 / (ROPE_THETA ** (np.arange(0, HEAD_DIM, 2, dtype=np.float64) / HEAD_DIM))
    ang = np.arange(seq, dtype=np.float64)[:, None] * inv[None, :]
    ang = np.concatenate([ang, ang, ang, ang], axis=-1)
    cos, sin = np.cos(ang), np.sin(ang)
    upper = (np.arange(LANES) % HEAD_DIM) >= HEAD_DIM // 2
    as_f32 = lambda a: jnp.asarray(a.astype(np.float32))
    return as_f32(cos), as_f32(np.where(upper, sin, 0.0)), as_f32(np.where(upper, 0.0, -sin))


def kernel(x, c, w_ada, b_ada, w_ada_final, b_ada_final, g_ffn1, g_mix, g_ffn2, g_final,
           ffn1_w1, ffn1_w3, ffn1_w2, ffn2_w1, ffn2_w3, ffn2_w2, w_in,
           lambda_q1, lambda_k1, lambda_q2, lambda_k2, subln_g, conv_w, conv_norm_g, w_out):
    bsz, seq, d = x.shape
    depth = w_ada.shape[0]
    cos, sinp, sinm = _rope_tables(seq)
    gsz = CONV_WIDTH // CONV_GROUPS
    grp = np.arange(CONV_WIDTH) // gsz
    gmat = jnp.asarray(np.where(grp[:, None] == grp[None, :], 1.0 / gsz, 0.0), dtype=BF16)
    modf3 = _ada(c, w_ada_final, b_ada_final, 1024).reshape(bsz, 2, d)
    bf = lambda w: w.astype(BF16)

    x2d = x.reshape(bsz * seq, d)
    for l in range(depth):
        lambda_init = 0.8 - 0.6 * math.exp(-0.3 * l)
        mod3 = _ada(c, w_ada[l], b_ada[l], 1024).reshape(bsz, N_MOD, d)
        w_in_b = bf(w_in[l])
        w_main = jnp.concatenate([w_in_b[:, :2 * ATTN_WIDTH], w_in_b[:, 3 * ATTN_WIDTH:]], axis=1)
        w_vt = w_in_b[:, 2 * ATTN_WIDTH:3 * ATTN_WIDTH].T
        w_out_b = bf(w_out[l])

        x2d = _ffn(x2d, mod3, g_ffn1[l], bf(ffn1_w1[l]), bf(ffn1_w3[l]), bf(ffn1_w2[l]), 0, seq)
        q, k, vt, y = _inproj(x2d, mod3, g_mix[l], w_main, w_vt, cos, sinp, sinm,
                              conv_w[l], conv_norm_g[l], gmat, seq)
        a = _attn(q, k, vt, lambda_q1[l], lambda_k1[l], lambda_q2[l], lambda_k2[l],
                  subln_g[l], lambda_init)
        mixer_args = (a, y, w_out_b[:ATTN_WIDTH], w_out_b[ATTN_WIDTH:])
        final_args = (modf3, g_final) if l == depth - 1 else None
        x2d = _ffn(x2d, mod3, g_ffn2[l], bf(ffn2_w1[l]), bf(ffn2_w3[l]), bf(ffn2_w2[l]), 6, seq,
                   mixer_args, final_args)
    return x2d.reshape(bsz, seq, d)
```

```python
import functools
import math

import jax
import jax.numpy as jnp
import numpy as np
from jax import lax
from jax.experimental import pallas as pl
from jax.experimental.pallas import tpu as pltpu

F32 = jnp.float32
BF16 = jnp.bfloat16

LANES = 128
SUBLANES = 8

D_MODEL = 1024
ATTN_HEADS = 4
HEAD_DIM = 64
ATTN_WIDTH = ATTN_HEADS * 2 * HEAD_DIM
CONV_WIDTH = D_MODEL - ATTN_WIDTH
CONV_GROUPS = 8
CONV_K = 3
ROPE_THETA = 10000.0
NORM_EPS = 1e-6
SUBLN_EPS = 1e-5
N_MOD = 9
IN_COLS = 3 * ATTN_WIDTH + 3 * CONV_WIDTH

FF_CHUNK = 256
ROW_TILE = 1024
ROW_SUBTILE = 512
ATTN_TILE = 512
ONES_ROWS = 16
VT_ROWS = 2 * HEAD_DIM + ONES_ROWS
HEADS_PER_STEP = 2
VMEM_LIMIT = 56 * 1024 * 1024
NT_DIMS = (((1,), (1,)), ((), ()))
Q_SCALE = math.log2(math.e) / math.sqrt(HEAD_DIM)


def _silu(a):
    return a / (1.0 + jnp.exp(-a))


def _rms(x, eps):
    return x * lax.rsqrt(jnp.mean(x * x, axis=-1, keepdims=True) + eps)


def _ada_kernel(c_ref, w_ref, b_ref, o_ref):
    o_ref[...] = jnp.dot(_silu(c_ref[...]), w_ref[...],
                         preferred_element_type=F32) + b_ref[...]


def _ada(c, w, b, tn):
    bsz, d = c.shape
    n = w.shape[1]
    return pl.pallas_call(
        _ada_kernel,
        grid=(n // tn,),
        in_specs=[pl.BlockSpec((bsz, d), lambda j: (0, 0)),
                  pl.BlockSpec((d, tn), lambda j: (0, j)),
                  pl.BlockSpec((1, tn), lambda j: (0, j))],
        out_specs=pl.BlockSpec((bsz, tn), lambda j: (0, j)),
        out_shape=jax.ShapeDtypeStruct((bsz, n), F32),
        name="ada",
    )(c, w, b.reshape(1, n))


def _ffn_kernel(*refs, mod_row, mixer, final):
    refs = list(refs)
    x_ref, mod_ref, g_ref = refs[:3]
    del refs[:3]
    if mixer:
        a_ref, y_ref, wa_ref, wy_ref = refs[:4]
        del refs[:4]
    w1_ref, w3_ref, w2_ref = refs[:3]
    del refs[:3]
    if final:
        modf_ref, gf_ref = refs[:2]
        del refs[:2]
    o_ref, act_ref = refs
    shift = mod_ref[0, mod_row:mod_row + 1, :]
    scale = mod_ref[0, mod_row + 1:mod_row + 2, :]
    gate = mod_ref[0, mod_row + 2:mod_row + 3, :]
    n_chunks = w2_ref.shape[0] // FF_CHUNK
    for r in range(x_ref.shape[0] // ROW_SUBTILE):
        rows = slice(r * ROW_SUBTILE, (r + 1) * ROW_SUBTILE)
        x = x_ref[rows, :]
        if mixer:
            attn = jnp.concatenate([a_ref[0, hd, rows, :] for hd in range(a_ref.shape[1])],
                                   axis=-1)
            mix = (jnp.dot(attn, wa_ref[...], preferred_element_type=F32)
                   + jnp.dot(y_ref[rows, :], wy_ref[...], preferred_element_type=F32))
            x = x + mod_ref[0, mod_row - 1:mod_row, :] * mix
        h = (_rms(x, NORM_EPS) * g_ref[...] * (1.0 + scale) + shift).astype(BF16)
        for c in range(n_chunks):
            cols = slice(c * FF_CHUNK, (c + 1) * FF_CHUNK)
            a = jnp.dot(h, w1_ref[:, cols], preferred_element_type=F32)
            b = jnp.dot(h, w3_ref[:, cols], preferred_element_type=F32)
            act_ref[rows, cols] = (_silu(a) * b).astype(BF16)
        y = x + 0.5 * gate * jnp.dot(act_ref[rows, :], w2_ref[...],
                                     preferred_element_type=F32)
        if final:
            y = (_rms(y, NORM_EPS) * gf_ref[...] * (1.0 + modf_ref[0, 1:2, :])
                 + modf_ref[0, 0:1, :])
        o_ref[rows, :] = y


def _ffn(x2d, mod3, g, w1, w3, w2, mod_row, rows_per_batch, mixer_args=None, final_args=None):
    m, d = x2d.shape
    tm = ROW_TILE
    tiles_per_batch = rows_per_batch // tm
    row = lambda i: (i, 0)
    fix = lambda i: (0, 0)
    per_batch = lambda i: (i // tiles_per_batch, 0, 0)
    resident = lambda w: pl.BlockSpec(w.shape, fix, pipeline_mode=pl.Buffered(1))
    in_specs = [pl.BlockSpec((tm, d), row),
                pl.BlockSpec((1, N_MOD, d), per_batch),
                pl.BlockSpec((1, d), fix)]
    args = [x2d, mod3, g.reshape(1, d)]
    if mixer_args is not None:
        a4d, y2d, wa, wy = mixer_args
        in_specs += [pl.BlockSpec((1, a4d.shape[1], tm, a4d.shape[3]),
                                  lambda i: (i // tiles_per_batch, 0, i % tiles_per_batch, 0)),
                     pl.BlockSpec((tm, y2d.shape[1]), row), resident(wa), resident(wy)]
        args += [a4d, y2d, wa, wy]
    in_specs += [resident(w1), resident(w3), resident(w2)]
    args += [w1, w3, w2]
    if final_args is not None:
        modf3, gf = final_args
        in_specs += [pl.BlockSpec((1, 2, d), per_batch), pl.BlockSpec((1, d), fix)]
        args += [modf3, gf.reshape(1, d)]
    return pl.pallas_call(
        functools.partial(_ffn_kernel, mod_row=mod_row, mixer=mixer_args is not None,
                          final=final_args is not None),
        grid=(m // tm,),
        in_specs=in_specs,
        out_specs=pl.BlockSpec((tm, d), row),
        out_shape=jax.ShapeDtypeStruct((m, d), F32),
        scratch_shapes=[pltpu.VMEM((tm, w2.shape[0]), BF16)],
        compiler_params=pltpu.CompilerParams(
            dimension_semantics=("arbitrary",), vmem_limit_bytes=VMEM_LIMIT),
        name="ffn" + ("_mixer" if mixer_args is not None else "")
        + ("_final" if final_args is not None else ""),
    )(*args)


def _inproj_kernel(x_ref, mod_ref, g_ref, w_ref, wvt_ref, cos_ref, sinp_ref, sinm_ref,
                   convw_ref, cng_ref, gmat_ref,
                   q_ref, k_ref, vt_ref, y_ref, halo_ref, *, mod_row, tiles_per_batch):
    shift = mod_ref[0, mod_row:mod_row + 1, :]
    scale = mod_ref[0, mod_row + 1:mod_row + 2, :]
    half = HEAD_DIM // 2
    w = convw_ref[...]

    @pl.when(pl.program_id(0) % tiles_per_batch == 0)
    def _():
        halo_ref[...] = jnp.zeros_like(halo_ref)

    prev = halo_ref[...]
    for r in range(x_ref.shape[0] // ROW_SUBTILE):
        rows = slice(r * ROW_SUBTILE, (r + 1) * ROW_SUBTILE)
        h = (_rms(x_ref[rows, :], NORM_EPS) * g_ref[...] * (1.0 + scale) + shift).astype(BF16)
        z = jnp.dot(h, w_ref[...], preferred_element_type=F32)
        vt = lax.dot_general(wvt_ref[...], h, NT_DIMS, preferred_element_type=F32).astype(BF16)
        for j in range(ATTN_HEADS):
            vt_ref[0, j * VT_ROWS:j * VT_ROWS + LANES, rows] = vt[j * LANES:(j + 1) * LANES, :]
            vt_ref[0, j * VT_ROWS + LANES:(j + 1) * VT_ROWS, rows] = jnp.ones(
                (ONES_ROWS, ROW_SUBTILE), BF16)
        cos = cos_ref[rows, :]
        sinp = sinp_ref[rows, :]
        sinm = sinm_ref[rows, :]
        for j in range(ATTN_WIDTH // LANES):
            for base, out_ref, mult in ((0, q_ref, Q_SCALE), (ATTN_WIDTH, k_ref, 1.0)):
                t = z[:, base + j * LANES: base + (j + 1) * LANES]
                rot = (t * cos + pltpu.roll(t, half, axis=1) * sinp
                       + pltpu.roll(t, LANES - half, axis=1) * sinm)
                out_ref[0, j, rows, :] = (rot * mult).astype(BF16)

        c0 = 2 * ATTN_WIDTH
        gate_b = z[:, c0:c0 + CONV_WIDTH]
        cu = z[:, c0 + CONV_WIDTH:c0 + 2 * CONV_WIDTH] * z[:, c0 + 2 * CONV_WIDTH:]
        buf = jnp.concatenate([prev, cu], axis=0)
        cu_m1 = pltpu.roll(buf, 1, axis=0)[SUBLANES:, :]
        cu_m2 = pltpu.roll(buf, 2, axis=0)[SUBLANES:, :]
        prev = cu[ROW_SUBTILE - SUBLANES:, :]
        y = gate_b * (w[0:1, :] * cu_m2 + w[1:2, :] * cu_m1 + w[2:3, :] * cu)
        y2 = y * y
        y2_hi = y2.astype(BF16)
        y2_lo = (y2 - y2_hi.astype(F32)).astype(BF16)
        ms = (jnp.dot(y2_hi, gmat_ref[...], preferred_element_type=F32)
              + jnp.dot(y2_lo, gmat_ref[...], preferred_element_type=F32))
        y_ref[rows, :] = (y * lax.rsqrt(ms + NORM_EPS) * cng_ref[...]).astype(BF16)
    halo_ref[...] = prev


def _inproj(x2d, mod3, g, w_main, w_vt, cos, sinp, sinm, conv_w, conv_norm_g, gmat,
            rows_per_batch):
    m, d = x2d.shape
    tm = ROW_TILE
    tiles_per_batch = rows_per_batch // tm
    row = lambda i: (i, 0)
    fix = lambda i: (0, 0)
    pos = lambda i: (i % tiles_per_batch, 0)
    n_batch = m // rows_per_batch
    head_major = pl.BlockSpec((1, ATTN_HEADS, tm, LANES),
                              lambda i: (i // tiles_per_batch, 0, i % tiles_per_batch, 0))
    out_sd = jax.ShapeDtypeStruct((n_batch, ATTN_HEADS, rows_per_batch, LANES), BF16)
    return pl.pallas_call(
        functools.partial(_inproj_kernel, mod_row=3, tiles_per_batch=tiles_per_batch),
        grid=(m // tm,),
        in_specs=[pl.BlockSpec((tm, d), row),
                  pl.BlockSpec((1, N_MOD, d), lambda i: (i // tiles_per_batch, 0, 0)),
                  pl.BlockSpec((1, d), fix),
                  pl.BlockSpec(w_main.shape, fix, pipeline_mode=pl.Buffered(1)),
                  pl.BlockSpec(w_vt.shape, fix, pipeline_mode=pl.Buffered(1)),
                  pl.BlockSpec((tm, LANES), pos),
                  pl.BlockSpec((tm, LANES), pos),
                  pl.BlockSpec((tm, LANES), pos),
                  pl.BlockSpec((CONV_K, CONV_WIDTH), fix),
                  pl.BlockSpec((1, CONV_WIDTH), fix),
                  pl.BlockSpec((CONV_WIDTH, CONV_WIDTH), fix)],
        out_specs=[head_major, head_major,
                   pl.BlockSpec((1, ATTN_HEADS * VT_ROWS, tm),
                                lambda i: (i // tiles_per_batch, 0, i % tiles_per_batch)),
                   pl.BlockSpec((tm, CONV_WIDTH), row)],
        out_shape=[out_sd, out_sd,
                   jax.ShapeDtypeStruct((n_batch, ATTN_HEADS * VT_ROWS, rows_per_batch), BF16),
                   jax.ShapeDtypeStruct((m, CONV_WIDTH), BF16)],
        scratch_shapes=[pltpu.VMEM((SUBLANES, CONV_WIDTH), F32)],
        compiler_params=pltpu.CompilerParams(
            dimension_semantics=("arbitrary",), vmem_limit_bytes=VMEM_LIMIT),
        name="inproj",
    )(x2d, mod3, g.reshape(1, d), w_main, w_vt, cos, sinp, sinm, conv_w,
      conv_norm_g.reshape(1, CONV_WIDTH), gmat)


def _attn_kernel(q_ref, k_ref, vta_ref, lq1_ref, lk1_ref, lq2_ref, lk2_ref, sg_ref,
                 o_ref, sa_ref, sb_ref, cma_ref, cmb_ref, m_ref, acc_ref, *, lambda_init):
    seq = q_ref.shape[2]
    t = ATTN_TILE
    dv = 2 * HEAD_DIM
    n_tiles = seq // t
    heads = range(HEADS_PER_STEP)
    lam = (jnp.exp(jnp.sum(lq1_ref[...] * lk1_ref[...], axis=-1, keepdims=True))
           - jnp.exp(jnp.sum(lq2_ref[...] * lk2_ref[...], axis=-1, keepdims=True))
           + lambda_init)

    def stacked_queries(hd, i):
        q = q_ref[0, hd, pl.ds(pl.multiple_of(i * t, t), t), :]
        first_map = lax.broadcasted_iota(jnp.int32, (t, LANES), 1) < HEAD_DIM
        zero = jnp.zeros_like(q)
        return jnp.concatenate([jnp.where(first_map, q, zero),
                                jnp.where(first_map, zero, q)], axis=0)

    def scores_into(s_buf, cm_buf, qq, k0):
        for hd in heads:
            s = lax.dot_general(k_ref[0, hd, pl.ds(k0, t), :], qq[hd], NT_DIMS,
                                preferred_element_type=F32)
            s_buf[hd] = s
            cm_buf[hd] = jnp.max(s, axis=0, keepdims=True)

    def absorb(s_buf, cm_buf, k0, mask=False):
        for hd in heads:
            s = s_buf[hd]
            if mask:
                kpos = lax.broadcasted_iota(jnp.int32, (t, 2 * t), 0)
                qpos = lax.broadcasted_iota(jnp.int32, (t, 2 * t), 1)
                qpos = jnp.where(qpos >= t, qpos - t, qpos)
                s = jnp.where(kpos <= qpos, s, -jnp.inf)
                cm = jnp.max(s, axis=0, keepdims=True)
            else:
                cm = cm_buf[hd]
            m = m_ref[hd]
            m_new = jnp.maximum(m, cm)
            alpha = jnp.exp2(m - m_new)
            p = jnp.exp2(s - m_new).astype(BF16)
            m_ref[hd] = m_new
            pv = jnp.dot(vta_ref[0, hd * VT_ROWS:(hd + 1) * VT_ROWS, pl.ds(k0, t)], p,
                         preferred_element_type=F32)
            acc_ref[hd] = alpha * acc_ref[hd] + pv

    def q_tile(i, odd):
        q0 = pl.multiple_of(i * t, t)
        nxt = jnp.minimum(i + 1, n_tiles - 1)
        qq = [stacked_queries(hd, i) for hd in heads]
        qq_next = [stacked_queries(hd, nxt) for hd in heads]
        m_ref[...] = jnp.full(m_ref.shape, -jnp.inf, F32)
        acc_ref[...] = jnp.zeros_like(acc_ref)
        if not odd:
            scores_into(sb_ref, cmb_ref, [jnp.where(i == 0, b, a) for a, b in zip(qq, qq_next)], 0)
        absorb(sa_ref, cma_ref, q0, mask=True)

        def pair(jj, carry):
            k_even = pl.multiple_of(2 * jj * t, t)
            k_odd = pl.multiple_of(k_even + t, t)
            scores_into(sa_ref, cma_ref, qq, k_odd)
            absorb(sb_ref, cmb_ref, k_even)
            if odd:
                scores_into(sb_ref, cmb_ref, qq, pl.multiple_of(k_odd + t, t))
            else:
                done = 2 * jj + 2 >= i
                scores_into(sb_ref, cmb_ref,
                            [jnp.where(done, b, a) for a, b in zip(qq, qq_next)],
                            pl.multiple_of(jnp.where(done, 0, 2 * jj + 2) * t, t))
            absorb(sa_ref, cma_ref, k_odd)
            return carry

        lax.fori_loop(0, i // 2, pair, 0)
        scores_into(sa_ref, cma_ref, qq_next, pl.multiple_of(nxt * t, t))
        if odd:
            absorb(sb_ref, cmb_ref, pl.multiple_of((i - 1) * t, t))
        for hd in heads:
            acc = acc_ref[hd]
            o = acc[:dv, :] / acc[dv:dv + 1, :]
            o = o[:, :t] - lam * o[:, t:]
            o = o * lax.rsqrt(jnp.mean(o * o, axis=0, keepdims=True) + SUBLN_EPS)
            o = o * (sg_ref[...] * (1.0 - lambda_init))
            o_ref[0, hd, pl.ds(q0, t), :] = o.T.astype(BF16)

    def q_tile_pair(a, carry):
        q_tile(2 * a, odd=False)
        q_tile(2 * a + 1, odd=True)
        return carry

    scores_into(sa_ref, cma_ref, [stacked_queries(hd, 0) for hd in heads], 0)
    lax.fori_loop(0, n_tiles // 2, q_tile_pair, 0)


def _attn(q, k, vta, lq1, lk1, lq2, lk2, subln_g, lambda_init):
    bsz, n_heads, seq, _ = q.shape
    t = ATTN_TILE
    nh = HEADS_PER_STEP
    assert (seq // t) % 2 == 0 and n_heads % nh == 0 and vta.shape[1] == n_heads * VT_ROWS
    blk = pl.BlockSpec((1, nh, seq, LANES), lambda b, h: (b, h, 0, 0))
    vec = pl.BlockSpec((1, HEAD_DIM), lambda b, h: (0, 0))
    return pl.pallas_call(
        functools.partial(_attn_kernel, lambda_init=lambda_init),
        grid=(bsz, n_heads // nh),
        in_specs=[blk, blk,
                  pl.BlockSpec((1, nh * VT_ROWS, seq), lambda b, h: (b, h, 0)),
                  vec, vec, vec, vec,
                  pl.BlockSpec((2 * HEAD_DIM, 1), lambda b, h: (0, 0))],
        out_specs=blk,
        out_shape=jax.ShapeDtypeStruct((bsz, n_heads, seq, LANES), BF16),
        scratch_shapes=[pltpu.VMEM((nh, t, 2 * t), F32),
                        pltpu.VMEM((nh, t, 2 * t), F32),
                        pltpu.VMEM((nh, 1, 2 * t), F32),
                        pltpu.VMEM((nh, 1, 2 * t), F32),
                        pltpu.VMEM((nh, 1, 2 * t), F32),
                        pltpu.VMEM((nh, VT_ROWS, 2 * t), F32)],
        compiler_params=pltpu.CompilerParams(
            dimension_semantics=("arbitrary", "arbitrary"), vmem_limit_bytes=VMEM_LIMIT),
        name="attn",
    )(q, k, vta, lq1.reshape(1, -1), lk1.reshape(1, -1), lq2.reshape(1, -1),
      lk2.reshape(1, -1), subln_g.reshape(-1, 1))


def _rope_tables(seq):
    inv = 1.0 / (ROPE_THETA ** (np.arange(0, HEAD_DIM, 2, dtype=np.float64) / HEAD_DIM))
    ang = np.arange(seq, dtype=np.float64)[:, None] * inv[None, :]
    ang = np.concatenate([ang, ang, ang, ang], axis=-1)
    cos, sin = np.cos(ang), np.sin(ang)
    upper = (np.arange(LANES) % HEAD_DIM) >= HEAD_DIM // 2
    as_f32 = lambda a: jnp.asarray(a.astype(np.float32))
    return as_f32(cos), as_f32(np.where(upper, sin, 0.0)), as_f32(np.where(upper, 0.0, -sin))


def kernel(x, c, w_ada, b_ada, w_ada_final, b_ada_final, g_ffn1, g_mix, g_ffn2, g_final,
           ffn1_w1, ffn1_w3, ffn1_w2, ffn2_w1, ffn2_w3, ffn2_w2, w_in,
           lambda_q1, lambda_k1, lambda_q2, lambda_k2, subln_g, conv_w, conv_norm_g, w_out):
    bsz, seq, d = x.shape
    depth = w_ada.shape[0]
    cos, sinp, sinm = _rope_tables(seq)
    gsz = CONV_WIDTH // CONV_GROUPS
    grp = np.arange(CONV_WIDTH) // gsz
    gmat = jnp.asarray(np.where(grp[:, None] == grp[None, :], 1.0 / gsz, 0.0), dtype=BF16)
    modf3 = _ada(c, w_ada_final, b_ada_final, 1024).reshape(bsz, 2, d)
    bf = lambda w: w.astype(BF16)

    x2d = x.reshape(bsz * seq, d)
    for l in range(depth):
        lambda_init = 0.8 - 0.6 * math.exp(-0.3 * l)
        mod3 = _ada(c, w_ada[l], b_ada[l], 1024).reshape(bsz, N_MOD, d)
        w_in_b = bf(w_in[l])
        w_main = jnp.concatenate([w_in_b[:, :2 * ATTN_WIDTH], w_in_b[:, 3 * ATTN_WIDTH:]], axis=1)
        w_vt = w_in_b[:, 2 * ATTN_WIDTH:3 * ATTN_WIDTH].T
        w_out_b = bf(w_out[l])

        x2d = _ffn(x2d, mod3, g_ffn1[l], bf(ffn1_w1[l]), bf(ffn1_w3[l]), bf(ffn1_w2[l]), 0, seq)
        q, k, vt, y = _inproj(x2d, mod3, g_mix[l], w_main, w_vt, cos, sinp, sinm,
                              conv_w[l], conv_norm_g[l], gmat, seq)
        a = _attn(q, k, vt, lambda_q1[l], lambda_k1[l], lambda_q2[l], lambda_k2[l],
                  subln_g[l], lambda_init)
        mixer_args = (a, y, w_out_b[:ATTN_WIDTH], w_out_b[ATTN_WIDTH:])
        final_args = (modf3, g_final) if l == depth - 1 else None
        x2d = _ffn(x2d, mod3, g_ffn2[l], bf(ffn2_w1[l]), bf(ffn2_w3[l]), bf(ffn2_w2[l]), 6, seq,
                   mixer_args, final_args)
    return x2d.reshape(bsz, seq, d)
```

```python
import functools
import math

import jax
import jax.numpy as jnp
import numpy as np
from jax import lax
from jax.experimental import pallas as pl
from jax.experimental.pallas import tpu as pltpu

F32 = jnp.float32
BF16 = jnp.bfloat16

LANES = 128
SUBLANES = 8

D_MODEL = 1024
ATTN_HEADS = 4
HEAD_DIM = 64
ATTN_WIDTH = ATTN_HEADS * 2 * HEAD_DIM
CONV_WIDTH = D_MODEL - ATTN_WIDTH
CONV_GROUPS = 8
CONV_K = 3
ROPE_THETA = 10000.0
NORM_EPS = 1e-6
SUBLN_EPS = 1e-5
N_MOD = 9
IN_COLS = 3 * ATTN_WIDTH + 3 * CONV_WIDTH

FF_CHUNK = 256
ROW_TILE = 1024
ROW_SUBTILE = 512
ATTN_TILE = 512
ONES_ROWS = 16
VT_ROWS = 2 * HEAD_DIM + ONES_ROWS
HEADS_PER_STEP = 4
VMEM_LIMIT = 56 * 1024 * 1024
NT_DIMS = (((1,), (1,)), ((), ()))
Q_SCALE = math.log2(math.e) / math.sqrt(HEAD_DIM)


def _silu(a):
    return a / (1.0 + jnp.exp(-a))


def _rms(x, eps):
    return x * lax.rsqrt(jnp.mean(x * x, axis=-1, keepdims=True) + eps)


def _ada_kernel(c_ref, w_ref, b_ref, o_ref):
    o_ref[...] = jnp.dot(_silu(c_ref[...]), w_ref[...],
                         preferred_element_type=F32) + b_ref[...]


def _ada(c, w, b, tn):
    bsz, d = c.shape
    n = w.shape[1]
    return pl.pallas_call(
        _ada_kernel,
        grid=(n // tn,),
        in_specs=[pl.BlockSpec((bsz, d), lambda j: (0, 0)),
                  pl.BlockSpec((d, tn), lambda j: (0, j)),
                  pl.BlockSpec((1, tn), lambda j: (0, j))],
        out_specs=pl.BlockSpec((bsz, tn), lambda j: (0, j)),
        out_shape=jax.ShapeDtypeStruct((bsz, n), F32),
        name="ada",
    )(c, w, b.reshape(1, n))


def _ffn_kernel(*refs, mod_row, mixer, final):
    refs = list(refs)
    x_ref, mod_ref, g_ref = refs[:3]
    del refs[:3]
    if mixer:
        a_ref, y_ref, wa_ref, wy_ref = refs[:4]
        del refs[:4]
    w1_ref, w3_ref, w2_ref = refs[:3]
    del refs[:3]
    if final:
        modf_ref, gf_ref = refs[:2]
        del refs[:2]
    o_ref, act_ref = refs
    shift = mod_ref[0, mod_row:mod_row + 1, :]
    scale = mod_ref[0, mod_row + 1:mod_row + 2, :]
    gate = mod_ref[0, mod_row + 2:mod_row + 3, :]
    n_chunks = w2_ref.shape[0] // FF_CHUNK
    for r in range(x_ref.shape[0] // ROW_SUBTILE):
        rows = slice(r * ROW_SUBTILE, (r + 1) * ROW_SUBTILE)
        x = x_ref[rows, :]
        if mixer:
            attn = jnp.concatenate([a_ref[0, hd, rows, :] for hd in range(a_ref.shape[1])],
                                   axis=-1)
            mix = (jnp.dot(attn, wa_ref[...], preferred_element_type=F32)
                   + jnp.dot(y_ref[rows, :], wy_ref[...], preferred_element_type=F32))
            x = x + mod_ref[0, mod_row - 1:mod_row, :] * mix
        h = (_rms(x, NORM_EPS) * g_ref[...] * (1.0 + scale) + shift).astype(BF16)
        for c in range(n_chunks):
            cols = slice(c * FF_CHUNK, (c + 1) * FF_CHUNK)
            a = jnp.dot(h, w1_ref[:, cols], preferred_element_type=F32)
            b = jnp.dot(h, w3_ref[:, cols], preferred_element_type=F32)
            act_ref[rows, cols] = (_silu(a) * b).astype(BF16)
        y = x + 0.5 * gate * jnp.dot(act_ref[rows, :], w2_ref[...],
                                     preferred_element_type=F32)
        if final:
            y = (_rms(y, NORM_EPS) * gf_ref[...] * (1.0 + modf_ref[0, 1:2, :])
                 + modf_ref[0, 0:1, :])
        o_ref[rows, :] = y


def _ffn(x2d, mod3, g, w1, w3, w2, mod_row, rows_per_batch, mixer_args=None, final_args=None):
    m, d = x2d.shape
    tm = ROW_TILE
    tiles_per_batch = rows_per_batch // tm
    row = lambda i: (i, 0)
    fix = lambda i: (0, 0)
    per_batch = lambda i: (i // tiles_per_batch, 0, 0)
    resident = lambda w: pl.BlockSpec(w.shape, fix, pipeline_mode=pl.Buffered(1))
    in_specs = [pl.BlockSpec((tm, d), row),
                pl.BlockSpec((1, N_MOD, d), per_batch),
                pl.BlockSpec((1, d), fix)]
    args = [x2d, mod3, g.reshape(1, d)]
    if mixer_args is not None:
        a4d, y2d, wa, wy = mixer_args
        in_specs += [pl.BlockSpec((1, a4d.shape[1], tm, a4d.shape[3]),
                                  lambda i: (i // tiles_per_batch, 0, i % tiles_per_batch, 0)),
                     pl.BlockSpec((tm, y2d.shape[1]), row), resident(wa), resident(wy)]
        args += [a4d, y2d, wa, wy]
    in_specs += [resident(w1), resident(w3), resident(w2)]
    args += [w1, w3, w2]
    if final_args is not None:
        modf3, gf = final_args
        in_specs += [pl.BlockSpec((1, 2, d), per_batch), pl.BlockSpec((1, d), fix)]
        args += [modf3, gf.reshape(1, d)]
    return pl.pallas_call(
        functools.partial(_ffn_kernel, mod_row=mod_row, mixer=mixer_args is not None,
                          final=final_args is not None),
        grid=(m // tm,),
        in_specs=in_specs,
        out_specs=pl.BlockSpec((tm, d), row),
        out_shape=jax.ShapeDtypeStruct((m, d), F32),
        scratch_shapes=[pltpu.VMEM((tm, w2.shape[0]), BF16)],
        compiler_params=pltpu.CompilerParams(
            dimension_semantics=("arbitrary",), vmem_limit_bytes=VMEM_LIMIT),
        name="ffn" + ("_mixer" if mixer_args is not None else "")
        + ("_final" if final_args is not None else ""),
    )(*args)


def _inproj_kernel(x_ref, mod_ref, g_ref, w_ref, wvt_ref, cos_ref, sinp_ref, sinm_ref,
                   convw_ref, cng_ref, gmat_ref,
                   q_ref, k_ref, vt_ref, y_ref, halo_ref, *, mod_row, tiles_per_batch):
    shift = mod_ref[0, mod_row:mod_row + 1, :]
    scale = mod_ref[0, mod_row + 1:mod_row + 2, :]
    half = HEAD_DIM // 2
    w = convw_ref[...]

    @pl.when(pl.program_id(0) % tiles_per_batch == 0)
    def _():
        halo_ref[...] = jnp.zeros_like(halo_ref)

    prev = halo_ref[...]
    for r in range(x_ref.shape[0] // ROW_SUBTILE):
        rows = slice(r * ROW_SUBTILE, (r + 1) * ROW_SUBTILE)
        h = (_rms(x_ref[rows, :], NORM_EPS) * g_ref[...] * (1.0 + scale) + shift).astype(BF16)
        z = jnp.dot(h, w_ref[...], preferred_element_type=F32)
        vt = lax.dot_general(wvt_ref[...], h, NT_DIMS, preferred_element_type=F32).astype(BF16)
        for j in range(ATTN_HEADS):
            vt_ref[0, j * VT_ROWS:j * VT_ROWS + LANES, rows] = vt[j * LANES:(j + 1) * LANES, :]
            vt_ref[0, j * VT_ROWS + LANES:(j + 1) * VT_ROWS, rows] = jnp.ones(
                (ONES_ROWS, ROW_SUBTILE), BF16)
        cos = cos_ref[rows, :]
        sinp = sinp_ref[rows, :]
        sinm = sinm_ref[rows, :]
        for j in range(ATTN_WIDTH // LANES):
            for base, out_ref, mult in ((0, q_ref, Q_SCALE), (ATTN_WIDTH, k_ref, 1.0)):
                t = z[:, base + j * LANES: base + (j + 1) * LANES]
                rot = (t * cos + pltpu.roll(t, half, axis=1) * sinp
                       + pltpu.roll(t, LANES - half, axis=1) * sinm)
                out_ref[0, j, rows, :] = (rot * mult).astype(BF16)

        c0 = 2 * ATTN_WIDTH
        gate_b = z[:, c0:c0 + CONV_WIDTH]
        cu = z[:, c0 + CONV_WIDTH:c0 + 2 * CONV_WIDTH] * z[:, c0 + 2 * CONV_WIDTH:]
        buf = jnp.concatenate([prev, cu], axis=0)
        cu_m1 = pltpu.roll(buf, 1, axis=0)[SUBLANES:, :]
        cu_m2 = pltpu.roll(buf, 2, axis=0)[SUBLANES:, :]
        prev = cu[ROW_SUBTILE - SUBLANES:, :]
        y = gate_b * (w[0:1, :] * cu_m2 + w[1:2, :] * cu_m1 + w[2:3, :] * cu)
        y2 = y * y
        y2_hi = y2.astype(BF16)
        y2_lo = (y2 - y2_hi.astype(F32)).astype(BF16)
        ms = (jnp.dot(y2_hi, gmat_ref[...], preferred_element_type=F32)
              + jnp.dot(y2_lo, gmat_ref[...], preferred_element_type=F32))
        y_ref[rows, :] = (y * lax.rsqrt(ms + NORM_EPS) * cng_ref[...]).astype(BF16)
    halo_ref[...] = prev


def _inproj(x2d, mod3, g, w_main, w_vt, cos, sinp, sinm, conv_w, conv_norm_g, gmat,
            rows_per_batch):
    m, d = x2d.shape
    tm = ROW_TILE
    tiles_per_batch = rows_per_batch // tm
    row = lambda i: (i, 0)
    fix = lambda i: (0, 0)
    pos = lambda i: (i % tiles_per_batch, 0)
    n_batch = m // rows_per_batch
    head_major = pl.BlockSpec((1, ATTN_HEADS, tm, LANES),
                              lambda i: (i // tiles_per_batch, 0, i % tiles_per_batch, 0))
    out_sd = jax.ShapeDtypeStruct((n_batch, ATTN_HEADS, rows_per_batch, LANES), BF16)
    return pl.pallas_call(
        functools.partial(_inproj_kernel, mod_row=3, tiles_per_batch=tiles_per_batch),
        grid=(m // tm,),
        in_specs=[pl.BlockSpec((tm, d), row),
                  pl.BlockSpec((1, N_MOD, d), lambda i: (i // tiles_per_batch, 0, 0)),
                  pl.BlockSpec((1, d), fix),
                  pl.BlockSpec(w_main.shape, fix, pipeline_mode=pl.Buffered(1)),
                  pl.BlockSpec(w_vt.shape, fix, pipeline_mode=pl.Buffered(1)),
                  pl.BlockSpec((tm, LANES), pos),
                  pl.BlockSpec((tm, LANES), pos),
                  pl.BlockSpec((tm, LANES), pos),
                  pl.BlockSpec((CONV_K, CONV_WIDTH), fix),
                  pl.BlockSpec((1, CONV_WIDTH), fix),
                  pl.BlockSpec((CONV_WIDTH, CONV_WIDTH), fix)],
        out_specs=[head_major, head_major,
                   pl.BlockSpec((1, ATTN_HEADS * VT_ROWS, tm),
                                lambda i: (i // tiles_per_batch, 0, i % tiles_per_batch)),
                   pl.BlockSpec((tm, CONV_WIDTH), row)],
        out_shape=[out_sd, out_sd,
                   jax.ShapeDtypeStruct((n_batch, ATTN_HEADS * VT_ROWS, rows_per_batch), BF16),
                   jax.ShapeDtypeStruct((m, CONV_WIDTH), BF16)],
        scratch_shapes=[pltpu.VMEM((SUBLANES, CONV_WIDTH), F32)],
        compiler_params=pltpu.CompilerParams(
            dimension_semantics=("arbitrary",), vmem_limit_bytes=VMEM_LIMIT),
        name="inproj",
    )(x2d, mod3, g.reshape(1, d), w_main, w_vt, cos, sinp, sinm, conv_w,
      conv_norm_g.reshape(1, CONV_WIDTH), gmat)


def _attn_kernel(q_ref, k_ref, vta_ref, lq1_ref, lk1_ref, lq2_ref, lk2_ref, sg_ref,
                 o_ref, sa_ref, sb_ref, cma_ref, cmb_ref, m_ref, acc_ref, *, lambda_init):
    seq = q_ref.shape[2]
    t = ATTN_TILE
    dv = 2 * HEAD_DIM
    n_tiles = seq // t
    heads = range(HEADS_PER_STEP)
    lam = (jnp.exp(jnp.sum(lq1_ref[...] * lk1_ref[...], axis=-1, keepdims=True))
           - jnp.exp(jnp.sum(lq2_ref[...] * lk2_ref[...], axis=-1, keepdims=True))
           + lambda_init)

    def stacked_queries(hd, i):
        q = q_ref[0, hd, pl.ds(pl.multiple_of(i * t, t), t), :]
        first_map = lax.broadcasted_iota(jnp.int32, (t, LANES), 1) < HEAD_DIM
        zero = jnp.zeros_like(q)
        return jnp.concatenate([jnp.where(first_map, q, zero),
                                jnp.where(first_map, zero, q)], axis=0)

    def scores_into(s_buf, cm_buf, qq, k0):
        for hd in heads:
            s = lax.dot_general(k_ref[0, hd, pl.ds(k0, t), :], qq[hd], NT_DIMS,
                                preferred_element_type=F32)
            s_buf[hd] = s
            cm_buf[hd] = jnp.max(s, axis=0, keepdims=True)

    def absorb(s_buf, cm_buf, k0, mask=False):
        for hd in heads:
            s = s_buf[hd]
            if mask:
                kpos = lax.broadcasted_iota(jnp.int32, (t, 2 * t), 0)
                qpos = lax.broadcasted_iota(jnp.int32, (t, 2 * t), 1)
                qpos = jnp.where(qpos >= t, qpos - t, qpos)
                s = jnp.where(kpos <= qpos, s, -jnp.inf)
                cm = jnp.max(s, axis=0, keepdims=True)
            else:
                cm = cm_buf[hd]
            m = m_ref[hd]
            m_new = jnp.maximum(m, cm)
            alpha = jnp.exp2(m - m_new)
            p = jnp.exp2(s - m_new).astype(BF16)
            m_ref[hd] = m_new
            pv = jnp.dot(vta_ref[0, hd * VT_ROWS:(hd + 1) * VT_ROWS, pl.ds(k0, t)], p,
                         preferred_element_type=F32)
            acc_ref[hd] = alpha * acc_ref[hd] + pv

    def q_tile(i, odd):
        q0 = pl.multiple_of(i * t, t)
        nxt = jnp.minimum(i + 1, n_tiles - 1)
        qq = [stacked_queries(hd, i) for hd in heads]
        qq_next = [stacked_queries(hd, nxt) for hd in heads]
        m_ref[...] = jnp.full(m_ref.shape, -jnp.inf, F32)
        acc_ref[...] = jnp.zeros_like(acc_ref)
        if not odd:
            scores_into(sb_ref, cmb_ref, [jnp.where(i == 0, b, a) for a, b in zip(qq, qq_next)], 0)
        absorb(sa_ref, cma_ref, q0, mask=True)

        def pair(jj, carry):
            k_even = pl.multiple_of(2 * jj * t, t)
            k_odd = pl.multiple_of(k_even + t, t)
            scores_into(sa_ref, cma_ref, qq, k_odd)
            absorb(sb_ref, cmb_ref, k_even)
            if odd:
                scores_into(sb_ref, cmb_ref, qq, pl.multiple_of(k_odd + t, t))
            else:
                done = 2 * jj + 2 >= i
                scores_into(sb_ref, cmb_ref,
                            [jnp.where(done, b, a) for a, b in zip(qq, qq_next)],
                            pl.multiple_of(jnp.where(done, 0, 2 * jj + 2) * t, t))
            absorb(sa_ref, cma_ref, k_odd)
            return carry

        lax.fori_loop(0, i // 2, pair, 0)
        scores_into(sa_ref, cma_ref, qq_next, pl.multiple_of(nxt * t, t))
        if odd:
            absorb(sb_ref, cmb_ref, pl.multiple_of((i - 1) * t, t))
        for hd in heads:
            acc = acc_ref[hd]
            o = acc[:dv, :] / acc[dv:dv + 1, :]
            o = o[:, :t] - lam * o[:, t:]
            o = o * lax.rsqrt(jnp.mean(o * o, axis=0, keepdims=True) + SUBLN_EPS)
            o = o * (sg_ref[...] * (1.0 - lambda_init))
            o_ref[0, hd, pl.ds(q0, t), :] = o.T.astype(BF16)

    def q_tile_pair(a, carry):
        q_tile(2 * a, odd=False)
        q_tile(2 * a + 1, odd=True)
        return carry

    scores_into(sa_ref, cma_ref, [stacked_queries(hd, 0) for hd in heads], 0)
    lax.fori_loop(0, n_tiles // 2, q_tile_pair, 0)


def _attn(q, k, vta, lq1, lk1, lq2, lk2, subln_g, lambda_init):
    bsz, n_heads, seq, _ = q.shape
    t = ATTN_TILE
    nh = HEADS_PER_STEP
    assert (seq // t) % 2 == 0 and n_heads % nh == 0 and vta.shape[1] == n_heads * VT_ROWS
    blk = pl.BlockSpec((1, nh, seq, LANES), lambda b, h: (b, h, 0, 0))
    vec = pl.BlockSpec((1, HEAD_DIM), lambda b, h: (0, 0))
    return pl.pallas_call(
        functools.partial(_attn_kernel, lambda_init=lambda_init),
        grid=(bsz, n_heads // nh),
        in_specs=[blk, blk,
                  pl.BlockSpec((1, nh * VT_ROWS, seq), lambda b, h: (b, h, 0)),
                  vec, vec, vec, vec,
                  pl.BlockSpec((2 * HEAD_DIM, 1), lambda b, h: (0, 0))],
        out_specs=blk,
        out_shape=jax.ShapeDtypeStruct((bsz, n_heads, seq, LANES), BF16),
        scratch_shapes=[pltpu.VMEM((nh, t, 2 * t), F32),
                        pltpu.VMEM((nh, t, 2 * t), F32),
                        pltpu.VMEM((nh, 1, 2 * t), F32),
                        pltpu.VMEM((nh, 1, 2 * t), F32),
                        pltpu.VMEM((nh, 1, 2 * t), F32),
                        pltpu.VMEM((nh, VT_ROWS, 2 * t), F32)],
        compiler_params=pltpu.CompilerParams(
            dimension_semantics=("arbitrary", "arbitrary"),
            vmem_limit_bytes=60 * 1024 * 1024),
        name="attn",
    )(q, k, vta, lq1.reshape(1, -1), lk1.reshape(1, -1), lq2.reshape(1, -1),
      lk2.reshape(1, -1), subln_g.reshape(-1, 1))


def _rope_tables(seq):
    inv = 1.0 / (ROPE_THETA ** (np.arange(0, HEAD_DIM, 2, dtype=np.float64) / HEAD_DIM))
    ang = np.arange(seq, dtype=np.float64)[:, None] * inv[None, :]
    ang = np.concatenate([ang, ang, ang, ang], axis=-1)
    cos, sin = np.cos(ang), np.sin(ang)
    upper = (np.arange(LANES) % HEAD_DIM) >= HEAD_DIM // 2
    as_f32 = lambda a: jnp.asarray(a.astype(np.float32))
    return as_f32(cos), as_f32(np.where(upper, sin, 0.0)), as_f32(np.where(upper, 0.0, -sin))


def kernel(x, c, w_ada, b_ada, w_ada_final, b_ada_final, g_ffn1, g_mix, g_ffn2, g_final,
           ffn1_w1, ffn1_w3, ffn1_w2, ffn2_w1, ffn2_w3, ffn2_w2, w_in,
           lambda_q1, lambda_k1, lambda_q2, lambda_k2, subln_g, conv_w, conv_norm_g, w_out):
    bsz, seq, d = x.shape
    depth = w_ada.shape[0]
    cos, sinp, sinm = _rope_tables(seq)
    gsz = CONV_WIDTH // CONV_GROUPS
    grp = np.arange(CONV_WIDTH) // gsz
    gmat = jnp.asarray(np.where(grp[:, None] == grp[None, :], 1.0 / gsz, 0.0), dtype=BF16)
    modf3 = _ada(c, w_ada_final, b_ada_final, 1024).reshape(bsz, 2, d)
    bf = lambda w: w.astype(BF16)

    x2d = x.reshape(bsz * seq, d)
    for l in range(depth):
        lambda_init = 0.8 - 0.6 * math.exp(-0.3 * l)
        mod3 = _ada(c, w_ada[l], b_ada[l], 1024).reshape(bsz, N_MOD, d)
        w_in_b = bf(w_in[l])
        w_main = jnp.concatenate([w_in_b[:, :2 * ATTN_WIDTH], w_in_b[:, 3 * ATTN_WIDTH:]], axis=1)
        w_vt = w_in_b[:, 2 * ATTN_WIDTH:3 * ATTN_WIDTH].T
        w_out_b = bf(w_out[l])

        x2d = _ffn(x2d, mod3, g_ffn1[l], bf(ffn1_w1[l]), bf(ffn1_w3[l]), bf(ffn1_w2[l]), 0, seq)
        q, k, vt, y = _inproj(x2d, mod3, g_mix[l], w_main, w_vt, cos, sinp, sinm,
                              conv_w[l], conv_norm_g[l], gmat, seq)
        a = _attn(q, k, vt, lambda_q1[l], lambda_k1[l], lambda_q2[l], lambda_k2[l],
                  subln_g[l], lambda_init)
        mixer_args = (a, y, w_out_b[:ATTN_WIDTH], w_out_b[ATTN_WIDTH:])
        final_args = (modf3, g_final) if l == depth - 1 else None
        x2d = _ffn(x2d, mod3, g_ffn2[l], bf(ffn2_w1[l]), bf(ffn2_w3[l]), bf(ffn2_w2[l]), 6, seq,
                   mixer_args, final_args)
    return x2d.reshape(bsz, seq, d)
```

```python
import functools
import math

import jax
import jax.numpy as jnp
import numpy as np
from jax import lax
from jax.experimental import pallas as pl
from jax.experimental.pallas import tpu as pltpu

F32 = jnp.float32
BF16 = jnp.bfloat16

LANES = 128
SUBLANES = 8

D_MODEL = 1024
ATTN_HEADS = 4
HEAD_DIM = 64
ATTN_WIDTH = ATTN_HEADS * 2 * HEAD_DIM
CONV_WIDTH = D_MODEL - ATTN_WIDTH
CONV_GROUPS = 8
CONV_K = 3
ROPE_THETA = 10000.0
NORM_EPS = 1e-6
SUBLN_EPS = 1e-5
N_MOD = 9
IN_COLS = 3 * ATTN_WIDTH + 3 * CONV_WIDTH

FF_CHUNK = 256
ROW_TILE = 1024
ROW_SUBTILE = 512
ATTN_TILE = 512
ONES_ROWS = 16
VT_ROWS = 2 * HEAD_DIM + ONES_ROWS
HEADS_PER_STEP = 4
VMEM_LIMIT = 56 * 1024 * 1024
CAST_BLOCKS = 4
NT_DIMS = (((1,), (1,)), ((), ()))
Q_SCALE = math.log2(math.e) / math.sqrt(HEAD_DIM)


def _silu(a):
    return a / (1.0 + jnp.exp(-a))


def _rms(x, eps):
    return x * lax.rsqrt(jnp.mean(x * x, axis=-1, keepdims=True) + eps)


def _ada_kernel(c_ref, w_ref, b_ref, o_ref):
    o_ref[...] = jnp.dot(_silu(c_ref[...]), w_ref[...],
                         preferred_element_type=F32) + b_ref[...]


def _ada(c, w, b, tn):
    bsz, d = c.shape
    n = w.shape[1]
    return pl.pallas_call(
        _ada_kernel,
        grid=(n // tn,),
        in_specs=[pl.BlockSpec((bsz, d), lambda j: (0, 0)),
                  pl.BlockSpec((d, tn), lambda j: (0, j)),
                  pl.BlockSpec((1, tn), lambda j: (0, j))],
        out_specs=pl.BlockSpec((bsz, tn), lambda j: (0, j)),
        out_shape=jax.ShapeDtypeStruct((bsz, n), F32),
        name="ada",
    )(c, w, b.reshape(1, n))


def _cast_kernel(w_ref, o_ref):
    o_ref[...] = w_ref[...].astype(o_ref.dtype)


def _to_bf16(w):
    rows, cols = w.shape
    rb = rows // CAST_BLOCKS
    return pl.pallas_call(
        _cast_kernel,
        grid=(CAST_BLOCKS,),
        in_specs=[pl.BlockSpec((rb, cols), lambda i: (i, 0))],
        out_specs=pl.BlockSpec((rb, cols), lambda i: (i, 0)),
        out_shape=jax.ShapeDtypeStruct((rows, cols), BF16),
        name="cast",
    )(w)


def _ffn_kernel(*refs, mod_row, mixer, final):
    refs = list(refs)
    x_ref, mod_ref, g_ref = refs[:3]
    del refs[:3]
    if mixer:
        a_ref, y_ref, wo_ref = refs[:3]
        del refs[:3]
    w1_ref, w3_ref, w2_ref = refs[:3]
    del refs[:3]
    if final:
        modf_ref, gf_ref = refs[:2]
        del refs[:2]
    o_ref, act_ref = refs
    shift = mod_ref[0, mod_row:mod_row + 1, :]
    scale = mod_ref[0, mod_row + 1:mod_row + 2, :]
    gate = mod_ref[0, mod_row + 2:mod_row + 3, :]
    n_chunks = w2_ref.shape[0] // FF_CHUNK
    for r in range(x_ref.shape[0] // ROW_SUBTILE):
        rows = slice(r * ROW_SUBTILE, (r + 1) * ROW_SUBTILE)
        x = x_ref[rows, :]
        if mixer:
            attn = jnp.concatenate([a_ref[0, hd, rows, :] for hd in range(a_ref.shape[1])],
                                   axis=-1)
            mix = (jnp.dot(attn, wo_ref[:ATTN_WIDTH, :], preferred_element_type=F32)
                   + jnp.dot(y_ref[rows, :], wo_ref[ATTN_WIDTH:, :], preferred_element_type=F32))
            x = x + mod_ref[0, mod_row - 1:mod_row, :] * mix
        h = (_rms(x, NORM_EPS) * g_ref[...] * (1.0 + scale) + shift).astype(BF16)
        for c in range(n_chunks):
            cols = slice(c * FF_CHUNK, (c + 1) * FF_CHUNK)
            a = jnp.dot(h, w1_ref[:, cols], preferred_element_type=F32)
            b = jnp.dot(h, w3_ref[:, cols], preferred_element_type=F32)
            act_ref[rows, cols] = (_silu(a) * b).astype(BF16)
        y = x + 0.5 * gate * jnp.dot(act_ref[rows, :], w2_ref[...],
                                     preferred_element_type=F32)
        if final:
            y = (_rms(y, NORM_EPS) * gf_ref[...] * (1.0 + modf_ref[0, 1:2, :])
                 + modf_ref[0, 0:1, :])
        o_ref[rows, :] = y


def _ffn(x2d, mod3, g, w1, w3, w2, mod_row, rows_per_batch, mixer_args=None, final_args=None):
    m, d = x2d.shape
    tm = ROW_TILE
    tiles_per_batch = rows_per_batch // tm
    row = lambda i: (i, 0)
    fix = lambda i: (0, 0)
    per_batch = lambda i: (i // tiles_per_batch, 0, 0)
    resident = lambda w: pl.BlockSpec(w.shape, fix, pipeline_mode=pl.Buffered(1))
    in_specs = [pl.BlockSpec((tm, d), row),
                pl.BlockSpec((1, N_MOD, d), per_batch),
                pl.BlockSpec((1, d), fix)]
    args = [x2d, mod3, g.reshape(1, d)]
    if mixer_args is not None:
        a4d, y2d, w_out = mixer_args
        in_specs += [pl.BlockSpec((1, a4d.shape[1], tm, a4d.shape[3]),
                                  lambda i: (i // tiles_per_batch, 0, i % tiles_per_batch, 0)),
                     pl.BlockSpec((tm, y2d.shape[1]), row), resident(w_out)]
        args += [a4d, y2d, w_out]
    in_specs += [resident(w1), resident(w3), resident(w2)]
    args += [w1, w3, w2]
    if final_args is not None:
        modf3, gf = final_args
        in_specs += [pl.BlockSpec((1, 2, d), per_batch), pl.BlockSpec((1, d), fix)]
        args += [modf3, gf.reshape(1, d)]
    return pl.pallas_call(
        functools.partial(_ffn_kernel, mod_row=mod_row, mixer=mixer_args is not None,
                          final=final_args is not None),
        grid=(m // tm,),
        in_specs=in_specs,
        out_specs=pl.BlockSpec((tm, d), row),
        out_shape=jax.ShapeDtypeStruct((m, d), F32),
        scratch_shapes=[pltpu.VMEM((tm, w2.shape[0]), BF16)],
        compiler_params=pltpu.CompilerParams(
            dimension_semantics=("arbitrary",), vmem_limit_bytes=VMEM_LIMIT),
        name="ffn" + ("_mixer" if mixer_args is not None else "")
        + ("_final" if final_args is not None else ""),
    )(*args)


def _inproj_kernel(x_ref, mod_ref, g_ref, w_ref, wvt_ref, cos_ref, sinp_ref, sinm_ref,
                   convw_ref, cng_ref, gmat_ref,
                   q_ref, k_ref, vt_ref, y_ref, halo_ref, *, mod_row, tiles_per_batch):
    shift = mod_ref[0, mod_row:mod_row + 1, :]
    scale = mod_ref[0, mod_row + 1:mod_row + 2, :]
    half = HEAD_DIM // 2
    w = convw_ref[...]

    @pl.when(pl.program_id(0) % tiles_per_batch == 0)
    def _():
        halo_ref[...] = jnp.zeros_like(halo_ref)

    prev = halo_ref[...]
    for r in range(x_ref.shape[0] // ROW_SUBTILE):
        rows = slice(r * ROW_SUBTILE, (r + 1) * ROW_SUBTILE)
        h = (_rms(x_ref[rows, :], NORM_EPS) * g_ref[...] * (1.0 + scale) + shift).astype(BF16)
        zqk = jnp.dot(h, w_ref[:, :2 * ATTN_WIDTH], preferred_element_type=F32)
        zcv = jnp.dot(h, w_ref[:, 3 * ATTN_WIDTH:], preferred_element_type=F32)
        vt = lax.dot_general(wvt_ref[...], h, NT_DIMS, preferred_element_type=F32).astype(BF16)
        for j in range(ATTN_HEADS):
            vt_ref[0, j * VT_ROWS:j * VT_ROWS + LANES, rows] = vt[j * LANES:(j + 1) * LANES, :]
            vt_ref[0, j * VT_ROWS + LANES:(j + 1) * VT_ROWS, rows] = jnp.ones(
                (ONES_ROWS, ROW_SUBTILE), BF16)
        cos = cos_ref[rows, :]
        sinp = sinp_ref[rows, :]
        sinm = sinm_ref[rows, :]
        for j in range(ATTN_WIDTH // LANES):
            for base, out_ref, mult in ((0, q_ref, Q_SCALE), (ATTN_WIDTH, k_ref, 1.0)):
                t = zqk[:, base + j * LANES: base + (j + 1) * LANES]
                rot = (t * cos + pltpu.roll(t, half, axis=1) * sinp
                       + pltpu.roll(t, LANES - half, axis=1) * sinm)
                out_ref[0, j, rows, :] = (rot * mult).astype(BF16)

        gate_b = zcv[:, :CONV_WIDTH]
        cu = zcv[:, CONV_WIDTH:2 * CONV_WIDTH] * zcv[:, 2 * CONV_WIDTH:]
        buf = jnp.concatenate([prev, cu], axis=0)
        cu_m1 = pltpu.roll(buf, 1, axis=0)[SUBLANES:, :]
        cu_m2 = pltpu.roll(buf, 2, axis=0)[SUBLANES:, :]
        prev = cu[ROW_SUBTILE - SUBLANES:, :]
        y = gate_b * (w[0:1, :] * cu_m2 + w[1:2, :] * cu_m1 + w[2:3, :] * cu)
        y2 = y * y
        y2_hi = y2.astype(BF16)
        y2_lo = (y2 - y2_hi.astype(F32)).astype(BF16)
        ms = (jnp.dot(y2_hi, gmat_ref[...], preferred_element_type=F32)
              + jnp.dot(y2_lo, gmat_ref[...], preferred_element_type=F32))
        y_ref[rows, :] = (y * lax.rsqrt(ms + NORM_EPS) * cng_ref[...]).astype(BF16)
    halo_ref[...] = prev


def _inproj(x2d, mod3, g, w_in, w_vt, cos, sinp, sinm, conv_w, conv_norm_g, gmat,
            rows_per_batch):
    m, d = x2d.shape
    tm = ROW_TILE
    tiles_per_batch = rows_per_batch // tm
    row = lambda i: (i, 0)
    fix = lambda i: (0, 0)
    pos = lambda i: (i % tiles_per_batch, 0)
    n_batch = m // rows_per_batch
    head_major = pl.BlockSpec((1, ATTN_HEADS, tm, LANES),
                              lambda i: (i // tiles_per_batch, 0, i % tiles_per_batch, 0))
    out_sd = jax.ShapeDtypeStruct((n_batch, ATTN_HEADS, rows_per_batch, LANES), BF16)
    return pl.pallas_call(
        functools.partial(_inproj_kernel, mod_row=3, tiles_per_batch=tiles_per_batch),
        grid=(m // tm,),
        in_specs=[pl.BlockSpec((tm, d), row),
                  pl.BlockSpec((1, N_MOD, d), lambda i: (i // tiles_per_batch, 0, 0)),
                  pl.BlockSpec((1, d), fix),
                  pl.BlockSpec(w_in.shape, fix, pipeline_mode=pl.Buffered(1)),
                  pl.BlockSpec(w_vt.shape, fix, pipeline_mode=pl.Buffered(1)),
                  pl.BlockSpec((tm, LANES), pos),
                  pl.BlockSpec((tm, LANES), pos),
                  pl.BlockSpec((tm, LANES), pos),
                  pl.BlockSpec((CONV_K, CONV_WIDTH), fix),
                  pl.BlockSpec((1, CONV_WIDTH), fix),
                  pl.BlockSpec((CONV_WIDTH, CONV_WIDTH), fix)],
        out_specs=[head_major, head_major,
                   pl.BlockSpec((1, ATTN_HEADS * VT_ROWS, tm),
                                lambda i: (i // tiles_per_batch, 0, i % tiles_per_batch)),
                   pl.BlockSpec((tm, CONV_WIDTH), row)],
        out_shape=[out_sd, out_sd,
                   jax.ShapeDtypeStruct((n_batch, ATTN_HEADS * VT_ROWS, rows_per_batch), BF16),
                   jax.ShapeDtypeStruct((m, CONV_WIDTH), BF16)],
        scratch_shapes=[pltpu.VMEM((SUBLANES, CONV_WIDTH), F32)],
        compiler_params=pltpu.CompilerParams(
            dimension_semantics=("arbitrary",), vmem_limit_bytes=VMEM_LIMIT),
        name="inproj",
    )(x2d, mod3, g.reshape(1, d), w_in, w_vt, cos, sinp, sinm, conv_w,
      conv_norm_g.reshape(1, CONV_WIDTH), gmat)


def _attn_kernel(q_ref, k_ref, vta_ref, lq1_ref, lk1_ref, lq2_ref, lk2_ref, sg_ref,
                 o_ref, sa_ref, sb_ref, cma_ref, cmb_ref, m_ref, acc_ref, *, lambda_init):
    seq = q_ref.shape[2]
    t = ATTN_TILE
    dv = 2 * HEAD_DIM
    n_tiles = seq // t
    heads = range(HEADS_PER_STEP)
    lam = (jnp.exp(jnp.sum(lq1_ref[...] * lk1_ref[...], axis=-1, keepdims=True))
           - jnp.exp(jnp.sum(lq2_ref[...] * lk2_ref[...], axis=-1, keepdims=True))
           + lambda_init)

    def stacked_queries(hd, i):
        q = q_ref[0, hd, pl.ds(pl.multiple_of(i * t, t), t), :]
        first_map = lax.broadcasted_iota(jnp.int32, (t, LANES), 1) < HEAD_DIM
        zero = jnp.zeros_like(q)
        return jnp.concatenate([jnp.where(first_map, q, zero),
                                jnp.where(first_map, zero, q)], axis=0)

    def scores_into(s_buf, cm_buf, qq, k0):
        for hd in heads:
            s = lax.dot_general(k_ref[0, hd, pl.ds(k0, t), :], qq[hd], NT_DIMS,
                                preferred_element_type=F32)
            s_buf[hd] = s
            cm_buf[hd] = jnp.max(s, axis=0, keepdims=True)

    def absorb(s_buf, cm_buf, k0, mask=False):
        for hd in heads:
            s = s_buf[hd]
            if mask:
                kpos = lax.broadcasted_iota(jnp.int32, (t, 2 * t), 0)
                qpos = lax.broadcasted_iota(jnp.int32, (t, 2 * t), 1)
                qpos = jnp.where(qpos >= t, qpos - t, qpos)
                s = jnp.where(kpos <= qpos, s, -jnp.inf)
                cm = jnp.max(s, axis=0, keepdims=True)
            else:
                cm = cm_buf[hd]
            m = m_ref[hd]
            m_new = jnp.maximum(m, cm)
            alpha = jnp.exp2(m - m_new)
            p = jnp.exp2(s - m_new).astype(BF16)
            m_ref[hd] = m_new
            pv = jnp.dot(vta_ref[0, hd * VT_ROWS:(hd + 1) * VT_ROWS, pl.ds(k0, t)], p,
                         preferred_element_type=F32)
            acc_ref[hd] = alpha * acc_ref[hd] + pv

    def q_tile(i, odd):
        q0 = pl.multiple_of(i * t, t)
        nxt = jnp.minimum(i + 1, n_tiles - 1)
        qq = [stacked_queries(hd, i) for hd in heads]
        qq_next = [stacked_queries(hd, nxt) for hd in heads]
        m_ref[...] = jnp.full(m_ref.shape, -jnp.inf, F32)
        acc_ref[...] = jnp.zeros_like(acc_ref)
        if not odd:
            scores_into(sb_ref, cmb_ref, [jnp.where(i == 0, b, a) for a, b in zip(qq, qq_next)], 0)
        absorb(sa_ref, cma_ref, q0, mask=True)

        def pair(jj, carry):
            k_even = pl.multiple_of(2 * jj * t, t)
            k_odd = pl.multiple_of(k_even + t, t)
            scores_into(sa_ref, cma_ref, qq, k_odd)
            absorb(sb_ref, cmb_ref, k_even)
            if odd:
                scores_into(sb_ref, cmb_ref, qq, pl.multiple_of(k_odd + t, t))
            else:
                done = 2 * jj + 2 >= i
                scores_into(sb_ref, cmb_ref,
                            [jnp.where(done, b, a) for a, b in zip(qq, qq_next)],
                            pl.multiple_of(jnp.where(done, 0, 2 * jj + 2) * t, t))
            absorb(sa_ref, cma_ref, k_odd)
            return carry

        lax.fori_loop(0, i // 2, pair, 0)
        scores_into(sa_ref, cma_ref, qq_next, pl.multiple_of(nxt * t, t))
        if odd:
            absorb(sb_ref, cmb_ref, pl.multiple_of((i - 1) * t, t))
        for hd in heads:
            acc = acc_ref[hd]
            o = acc[:dv, :] / acc[dv:dv + 1, :]
            o = o[:, :t] - lam * o[:, t:]
            o = o * lax.rsqrt(jnp.mean(o * o, axis=0, keepdims=True) + SUBLN_EPS)
            o = o * (sg_ref[...] * (1.0 - lambda_init))
            o_ref[0, hd, pl.ds(q0, t), :] = o.T.astype(BF16)

    def q_tile_pair(a, carry):
        q_tile(2 * a, odd=False)
        q_tile(2 * a + 1, odd=True)
        return carry

    scores_into(sa_ref, cma_ref, [stacked_queries(hd, 0) for hd in heads], 0)
    lax.fori_loop(0, n_tiles // 2, q_tile_pair, 0)


def _attn(q, k, vta, lq1, lk1, lq2, lk2, subln_g, lambda_init):
    bsz, n_heads, seq, _ = q.shape
    t = ATTN_TILE
    nh = HEADS_PER_STEP
    assert (seq // t) % 2 == 0 and n_heads % nh == 0 and vta.shape[1] == n_heads * VT_ROWS
    blk = pl.BlockSpec((1, nh, seq, LANES), lambda b, h: (b, h, 0, 0))
    vec = pl.BlockSpec((1, HEAD_DIM), lambda b, h: (0, 0))
    return pl.pallas_call(
        functools.partial(_attn_kernel, lambda_init=lambda_init),
        grid=(bsz, n_heads // nh),
        in_specs=[blk, blk,
                  pl.BlockSpec((1, nh * VT_ROWS, seq), lambda b, h: (b, h, 0)),
                  vec, vec, vec, vec,
                  pl.BlockSpec((2 * HEAD_DIM, 1), lambda b, h: (0, 0))],
        out_specs=blk,
        out_shape=jax.ShapeDtypeStruct((bsz, n_heads, seq, LANES), BF16),
        scratch_shapes=[pltpu.VMEM((nh, t, 2 * t), F32),
                        pltpu.VMEM((nh, t, 2 * t), F32),
                        pltpu.VMEM((nh, 1, 2 * t), F32),
                        pltpu.VMEM((nh, 1, 2 * t), F32),
                        pltpu.VMEM((nh, 1, 2 * t), F32),
                        pltpu.VMEM((nh, VT_ROWS, 2 * t), F32)],
        compiler_params=pltpu.CompilerParams(
            dimension_semantics=("arbitrary", "arbitrary"),
            vmem_limit_bytes=60 * 1024 * 1024),
        name="attn",
    )(q, k, vta, lq1.reshape(1, -1), lk1.reshape(1, -1), lq2.reshape(1, -1),
      lk2.reshape(1, -1), subln_g.reshape(-1, 1))


def _rope_tables(seq):
    inv = 1.0 / (ROPE_THETA ** (np.arange(0, HEAD_DIM, 2, dtype=np.float64) / HEAD_DIM))
    ang = np.arange(seq, dtype=np.float64)[:, None] * inv[None, :]
    ang = np.concatenate([ang, ang, ang, ang], axis=-1)
    cos, sin = np.cos(ang), np.sin(ang)
    upper = (np.arange(LANES) % HEAD_DIM) >= HEAD_DIM // 2
    as_f32 = lambda a: jnp.asarray(a.astype(np.float32))
    return as_f32(cos), as_f32(np.where(upper, sin, 0.0)), as_f32(np.where(upper, 0.0, -sin))


def kernel(x, c, w_ada, b_ada, w_ada_final, b_ada_final, g_ffn1, g_mix, g_ffn2, g_final,
           ffn1_w1, ffn1_w3, ffn1_w2, ffn2_w1, ffn2_w3, ffn2_w2, w_in,
           lambda_q1, lambda_k1, lambda_q2, lambda_k2, subln_g, conv_w, conv_norm_g, w_out):
    bsz, seq, d = x.shape
    depth = w_ada.shape[0]
    cos, sinp, sinm = _rope_tables(seq)
    gsz = CONV_WIDTH // CONV_GROUPS
    grp = np.arange(CONV_WIDTH) // gsz
    gmat = jnp.asarray(np.where(grp[:, None] == grp[None, :], 1.0 / gsz, 0.0), dtype=BF16)
    modf3 = _ada(c, w_ada_final, b_ada_final, 1024).reshape(bsz, 2, d)
    bf = _to_bf16

    x2d = x.reshape(bsz * seq, d)
    for l in range(depth):
        lambda_init = 0.8 - 0.6 * math.exp(-0.3 * l)
        mod3 = _ada(c, w_ada[l], b_ada[l], 1024).reshape(bsz, N_MOD, d)
        w_in_b = bf(w_in[l])
        w_vt = w_in_b[:, 2 * ATTN_WIDTH:3 * ATTN_WIDTH].T

        x2d = _ffn(x2d, mod3, g_ffn1[l], bf(ffn1_w1[l]), bf(ffn1_w3[l]), bf(ffn1_w2[l]), 0, seq)
        q, k, vt, y = _inproj(x2d, mod3, g_mix[l], w_in_b, w_vt, cos, sinp, sinm,
                              conv_w[l], conv_norm_g[l], gmat, seq)
        a = _attn(q, k, vt, lambda_q1[l], lambda_k1[l], lambda_q2[l], lambda_k2[l],
                  subln_g[l], lambda_init)
        mixer_args = (a, y, bf(w_out[l]))
        final_args = (modf3, g_final) if l == depth - 1 else None
        x2d = _ffn(x2d, mod3, g_ffn2[l], bf(ffn2_w1[l]), bf(ffn2_w3[l]), bf(ffn2_w2[l]), 6, seq,
                   mixer_args, final_args)
    return x2d.reshape(bsz, seq, d)
```

```python
import functools
import math

import jax
import jax.numpy as jnp
import numpy as np
from jax import lax
from jax.experimental import pallas as pl
from jax.experimental.pallas import tpu as pltpu

F32 = jnp.float32
BF16 = jnp.bfloat16

LANES = 128
SUBLANES = 8

D_MODEL = 1024
ATTN_HEADS = 4
HEAD_DIM = 64
ATTN_WIDTH = ATTN_HEADS * 2 * HEAD_DIM
CONV_WIDTH = D_MODEL - ATTN_WIDTH
CONV_GROUPS = 8
CONV_K = 3
ROPE_THETA = 10000.0
NORM_EPS = 1e-6
SUBLN_EPS = 1e-5
N_MOD = 9
IN_COLS = 3 * ATTN_WIDTH + 3 * CONV_WIDTH

FF_CHUNK = 256
ROW_TILE = 1024
ROW_SUBTILE = 512
ATTN_TILE = 512
ONES_ROWS = 16
VT_ROWS = 2 * HEAD_DIM + ONES_ROWS
HEADS_PER_STEP = 4
VMEM_LIMIT = 56 * 1024 * 1024
ATTN_VMEM_LIMIT = 60 * 1024 * 1024
CAST_BLOCKS = 8
ADA_BLOCKS = 4
NT_DIMS = (((1,), (1,)), ((), ()))
Q_SCALE = math.log2(math.e) / math.sqrt(HEAD_DIM)


def _silu(a):
    return a / (1.0 + jnp.exp(-a))


def _rms(x, eps):
    return x * lax.rsqrt(jnp.mean(x * x, axis=-1, keepdims=True) + eps)


def _ada_kernel(c_ref, w_ref, b_ref, o_ref):
    o_ref[...] = jnp.dot(_silu(c_ref[...]), w_ref[...],
                         preferred_element_type=F32) + b_ref[...]


def _ada(c, w, b):
    bsz, d = c.shape
    n = w.shape[1]
    tn = n // ADA_BLOCKS
    return pl.pallas_call(
        _ada_kernel,
        grid=(n // tn,),
        in_specs=[pl.BlockSpec((bsz, d), lambda j: (0, 0)),
                  pl.BlockSpec((d, tn), lambda j: (0, j)),
                  pl.BlockSpec((1, tn), lambda j: (0, j))],
        out_specs=pl.BlockSpec((bsz, tn), lambda j: (0, j)),
        out_shape=jax.ShapeDtypeStruct((bsz, n), F32),
        name="ada",
    )(c, w, b.reshape(1, n))


def _cast_kernel(*refs):
    n = len(refs) // 2
    for w_ref, o_ref in zip(refs[:n], refs[n:]):
        o_ref[...] = w_ref[...].astype(o_ref.dtype)


def _to_bf16(*ws):
    rows, cols = ws[0].shape
    assert all(w.shape == (rows, cols) for w in ws)
    rb = rows // CAST_BLOCKS
    spec = pl.BlockSpec((rb, cols), lambda i: (i, 0))
    return pl.pallas_call(
        _cast_kernel,
        grid=(CAST_BLOCKS,),
        in_specs=[spec] * len(ws),
        out_specs=[spec] * len(ws),
        out_shape=[jax.ShapeDtypeStruct((rows, cols), BF16)] * len(ws),
        name="cast",
    )(*ws)


def _ffn_kernel(*refs, mod_row, mixer, final):
    refs = list(refs)
    x_ref, mod_ref, g_ref = refs[:3]
    del refs[:3]
    if mixer:
        a_ref, y_ref, wo_ref = refs[:3]
        del refs[:3]
    w1_ref, w3_ref, w2_ref = refs[:3]
    del refs[:3]
    if final:
        modf_ref, gf_ref = refs[:2]
        del refs[:2]
    o_ref, act_ref = refs
    shift = mod_ref[0, mod_row:mod_row + 1, :]
    scale = mod_ref[0, mod_row + 1:mod_row + 2, :]
    gate = mod_ref[0, mod_row + 2:mod_row + 3, :]
    n_chunks = w2_ref.shape[0] // FF_CHUNK
    for r in range(x_ref.shape[0] // ROW_SUBTILE):
        rows = slice(r * ROW_SUBTILE, (r + 1) * ROW_SUBTILE)
        x = x_ref[rows, :]
        if mixer:
            attn = jnp.concatenate([a_ref[0, hd, rows, :] for hd in range(a_ref.shape[1])],
                                   axis=-1)
            mix = (jnp.dot(attn, wo_ref[:ATTN_WIDTH, :], preferred_element_type=F32)
                   + jnp.dot(y_ref[rows, :], wo_ref[ATTN_WIDTH:, :], preferred_element_type=F32))
            x = x + mod_ref[0, mod_row - 1:mod_row, :] * mix
        h = (_rms(x, NORM_EPS) * g_ref[...] * (1.0 + scale) + shift).astype(BF16)
        for c in range(n_chunks):
            cols = slice(c * FF_CHUNK, (c + 1) * FF_CHUNK)
            a = jnp.dot(h, w1_ref[:, cols], preferred_element_type=F32)
            b = jnp.dot(h, w3_ref[:, cols], preferred_element_type=F32)
            act_ref[rows, cols] = (_silu(a) * b).astype(BF16)
        y = x + 0.5 * gate * jnp.dot(act_ref[rows, :], w2_ref[...],
                                     preferred_element_type=F32)
        if final:
            y = (_rms(y, NORM_EPS) * gf_ref[...] * (1.0 + modf_ref[0, 1:2, :])
                 + modf_ref[0, 0:1, :])
        o_ref[rows, :] = y


def _ffn(x2d, mod3, g, w1, w3, w2, mod_row, rows_per_batch, mixer_args=None, final_args=None):
    m, d = x2d.shape
    tm = ROW_TILE
    tiles_per_batch = rows_per_batch // tm
    row = lambda i: (i, 0)
    fix = lambda i: (0, 0)
    per_batch = lambda i: (i // tiles_per_batch, 0, 0)
    resident = lambda w: pl.BlockSpec(w.shape, fix, pipeline_mode=pl.Buffered(1))
    in_specs = [pl.BlockSpec((tm, d), row),
                pl.BlockSpec((1, N_MOD, d), per_batch),
                pl.BlockSpec((1, d), fix)]
    args = [x2d, mod3, g.reshape(1, d)]
    if mixer_args is not None:
        a4d, y2d, w_out = mixer_args
        in_specs += [pl.BlockSpec((1, a4d.shape[1], tm, a4d.shape[3]),
                                  lambda i: (i // tiles_per_batch, 0, i % tiles_per_batch, 0)),
                     pl.BlockSpec((tm, y2d.shape[1]), row), resident(w_out)]
        args += [a4d, y2d, w_out]
    in_specs += [resident(w1), resident(w3), resident(w2)]
    args += [w1, w3, w2]
    if final_args is not None:
        modf3, gf = final_args
        in_specs += [pl.BlockSpec((1, 2, d), per_batch), pl.BlockSpec((1, d), fix)]
        args += [modf3, gf.reshape(1, d)]
    return pl.pallas_call(
        functools.partial(_ffn_kernel, mod_row=mod_row, mixer=mixer_args is not None,
                          final=final_args is not None),
        grid=(m // tm,),
        in_specs=in_specs,
        out_specs=pl.BlockSpec((tm, d), row),
        out_shape=jax.ShapeDtypeStruct((m, d), F32),
        scratch_shapes=[pltpu.VMEM((tm, w2.shape[0]), BF16)],
        compiler_params=pltpu.CompilerParams(
            dimension_semantics=("arbitrary",), vmem_limit_bytes=VMEM_LIMIT),
        name="ffn" + ("_mixer" if mixer_args is not None else "")
        + ("_final" if final_args is not None else ""),
    )(*args)


def _inproj_kernel(x_ref, mod_ref, g_ref, w_ref, wvt_ref, cos_ref, sinp_ref, sinm_ref,
                   convw_ref, cng_ref, gmat_ref,
                   q_ref, k_ref, vt_ref, y_ref, halo_ref, *, mod_row, tiles_per_batch):
    shift = mod_ref[0, mod_row:mod_row + 1, :]
    scale = mod_ref[0, mod_row + 1:mod_row + 2, :]
    half = HEAD_DIM // 2
    w = convw_ref[...]

    @pl.when(pl.program_id(0) % tiles_per_batch == 0)
    def _():
        halo_ref[...] = jnp.zeros_like(halo_ref)

    prev = halo_ref[...]
    for r in range(x_ref.shape[0] // ROW_SUBTILE):
        rows = slice(r * ROW_SUBTILE, (r + 1) * ROW_SUBTILE)
        h = (_rms(x_ref[rows, :], NORM_EPS) * g_ref[...] * (1.0 + scale) + shift).astype(BF16)
        zqk = jnp.dot(h, w_ref[:, :2 * ATTN_WIDTH], preferred_element_type=F32)
        zcv = jnp.dot(h, w_ref[:, 3 * ATTN_WIDTH:], preferred_element_type=F32)
        vt = lax.dot_general(wvt_ref[...], h, NT_DIMS, preferred_element_type=F32).astype(BF16)
        for j in range(ATTN_HEADS):
            vt_ref[0, j * VT_ROWS:j * VT_ROWS + LANES, rows] = vt[j * LANES:(j + 1) * LANES, :]
            vt_ref[0, j * VT_ROWS + LANES:(j + 1) * VT_ROWS, rows] = jnp.ones(
                (ONES_ROWS, ROW_SUBTILE), BF16)
        cos = cos_ref[rows, :]
        sinp = sinp_ref[rows, :]
        sinm = sinm_ref[rows, :]
        for j in range(ATTN_WIDTH // LANES):
            for base, out_ref, mult in ((0, q_ref, Q_SCALE), (ATTN_WIDTH, k_ref, 1.0)):
                t = zqk[:, base + j * LANES: base + (j + 1) * LANES]
                rot = (t * cos + pltpu.roll(t, half, axis=1) * sinp
                       + pltpu.roll(t, LANES - half, axis=1) * sinm)
                out_ref[0, j, rows, :] = (rot * mult).astype(BF16)

        gate_b = zcv[:, :CONV_WIDTH]
        cu = zcv[:, CONV_WIDTH:2 * CONV_WIDTH] * zcv[:, 2 * CONV_WIDTH:]
        buf = jnp.concatenate([prev, cu], axis=0)
        cu_m1 = pltpu.roll(buf, 1, axis=0)[SUBLANES:, :]
        cu_m2 = pltpu.roll(buf, 2, axis=0)[SUBLANES:, :]
        prev = cu[ROW_SUBTILE - SUBLANES:, :]
        y = gate_b * (w[0:1, :] * cu_m2 + w[1:2, :] * cu_m1 + w[2:3, :] * cu)
        y2 = y * y
        y2_hi = y2.astype(BF16)
        y2_lo = (y2 - y2_hi.astype(F32)).astype(BF16)
        ms = (jnp.dot(y2_hi, gmat_ref[...], preferred_element_type=F32)
              + jnp.dot(y2_lo, gmat_ref[...], preferred_element_type=F32))
        y_ref[rows, :] = (y * lax.rsqrt(ms + NORM_EPS) * cng_ref[...]).astype(BF16)
    halo_ref[...] = prev


def _inproj(x2d, mod3, g, w_in, w_vt, cos, sinp, sinm, conv_w, conv_norm_g, gmat,
            rows_per_batch):
    m, d = x2d.shape
    tm = ROW_TILE
    tiles_per_batch = rows_per_batch // tm
    row = lambda i: (i, 0)
    fix = lambda i: (0, 0)
    pos = lambda i: (i % tiles_per_batch, 0)
    n_batch = m // rows_per_batch
    head_major = pl.BlockSpec((1, ATTN_HEADS, tm, LANES),
                              lambda i: (i // tiles_per_batch, 0, i % tiles_per_batch, 0))
    out_sd = jax.ShapeDtypeStruct((n_batch, ATTN_HEADS, rows_per_batch, LANES), BF16)
    return pl.pallas_call(
        functools.partial(_inproj_kernel, mod_row=3, tiles_per_batch=tiles_per_batch),
        grid=(m // tm,),
        in_specs=[pl.BlockSpec((tm, d), row),
                  pl.BlockSpec((1, N_MOD, d), lambda i: (i // tiles_per_batch, 0, 0)),
                  pl.BlockSpec((1, d), fix),
                  pl.BlockSpec(w_in.shape, fix, pipeline_mode=pl.Buffered(1)),
                  pl.BlockSpec(w_vt.shape, fix, pipeline_mode=pl.Buffered(1)),
                  pl.BlockSpec((tm, LANES), pos),
                  pl.BlockSpec((tm, LANES), pos),
                  pl.BlockSpec((tm, LANES), pos),
                  pl.BlockSpec((CONV_K, CONV_WIDTH), fix),
                  pl.BlockSpec((1, CONV_WIDTH), fix),
                  pl.BlockSpec((CONV_WIDTH, CONV_WIDTH), fix)],
        out_specs=[head_major, head_major,
                   pl.BlockSpec((1, ATTN_HEADS * VT_ROWS, tm),
                                lambda i: (i // tiles_per_batch, 0, i % tiles_per_batch)),
                   pl.BlockSpec((tm, CONV_WIDTH), row)],
        out_shape=[out_sd, out_sd,
                   jax.ShapeDtypeStruct((n_batch, ATTN_HEADS * VT_ROWS, rows_per_batch), BF16),
                   jax.ShapeDtypeStruct((m, CONV_WIDTH), BF16)],
        scratch_shapes=[pltpu.VMEM((SUBLANES, CONV_WIDTH), F32)],
        compiler_params=pltpu.CompilerParams(
            dimension_semantics=("arbitrary",), vmem_limit_bytes=VMEM_LIMIT),
        name="inproj",
    )(x2d, mod3, g.reshape(1, d), w_in, w_vt, cos, sinp, sinm, conv_w,
      conv_norm_g.reshape(1, CONV_WIDTH), gmat)


def _attn_kernel(q_ref, k_ref, vta_ref, lq1_ref, lk1_ref, lq2_ref, lk2_ref, sg_ref,
                 o_ref, sa_ref, sb_ref, cma_ref, cmb_ref, m_ref, acc_ref, *, lambda_init):
    seq = q_ref.shape[2]
    t = ATTN_TILE
    dv = 2 * HEAD_DIM
    n_tiles = seq // t
    heads = range(HEADS_PER_STEP)
    lam = (jnp.exp(jnp.sum(lq1_ref[...] * lk1_ref[...], axis=-1, keepdims=True))
           - jnp.exp(jnp.sum(lq2_ref[...] * lk2_ref[...], axis=-1, keepdims=True))
           + lambda_init)

    def stacked_queries(hd, i):
        q = q_ref[0, hd, pl.ds(pl.multiple_of(i * t, t), t), :]
        first_map = lax.broadcasted_iota(jnp.int32, (t, LANES), 1) < HEAD_DIM
        zero = jnp.zeros_like(q)
        return jnp.concatenate([jnp.where(first_map, q, zero),
                                jnp.where(first_map, zero, q)], axis=0)

    def scores_into(s_buf, cm_buf, qq, k0):
        for hd in heads:
            s = lax.dot_general(k_ref[0, hd, pl.ds(k0, t), :], qq[hd], NT_DIMS,
                                preferred_element_type=F32)
            s_buf[hd] = s
            cm_buf[hd] = jnp.max(s, axis=0, keepdims=True)

    def absorb(s_buf, cm_buf, k0, mask=False):
        for hd in heads:
            s = s_buf[hd]
            if mask:
                kpos = lax.broadcasted_iota(jnp.int32, (t, 2 * t), 0)
                qpos = lax.broadcasted_iota(jnp.int32, (t, 2 * t), 1)
                qpos = jnp.where(qpos >= t, qpos - t, qpos)
                s = jnp.where(kpos <= qpos, s, -jnp.inf)
                cm = jnp.max(s, axis=0, keepdims=True)
            else:
                cm = cm_buf[hd]
            m = m_ref[hd]
            m_new = jnp.maximum(m, cm)
            alpha = jnp.exp2(m - m_new)
            p = jnp.exp2(s - m_new).astype(BF16)
            m_ref[hd] = m_new
            pv = jnp.dot(vta_ref[0, hd * VT_ROWS:(hd + 1) * VT_ROWS, pl.ds(k0, t)], p,
                         preferred_element_type=F32)
            acc_ref[hd] = alpha * acc_ref[hd] + pv

    def q_tile(i, odd):
        q0 = pl.multiple_of(i * t, t)
        nxt = jnp.minimum(i + 1, n_tiles - 1)
        qq = [stacked_queries(hd, i) for hd in heads]
        qq_next = [stacked_queries(hd, nxt) for hd in heads]
        m_ref[...] = jnp.full(m_ref.shape, -jnp.inf, F32)
        acc_ref[...] = jnp.zeros_like(acc_ref)
        if not odd:
            scores_into(sb_ref, cmb_ref, [jnp.where(i == 0, b, a) for a, b in zip(qq, qq_next)], 0)
        absorb(sa_ref, cma_ref, q0, mask=True)

        def pair(jj, carry):
            k_even = pl.multiple_of(2 * jj * t, t)
            k_odd = pl.multiple_of(k_even + t, t)
            scores_into(sa_ref, cma_ref, qq, k_odd)
            absorb(sb_ref, cmb_ref, k_even)
            if odd:
                scores_into(sb_ref, cmb_ref, qq, pl.multiple_of(k_odd + t, t))
            else:
                done = 2 * jj + 2 >= i
                scores_into(sb_ref, cmb_ref,
                            [jnp.where(done, b, a) for a, b in zip(qq, qq_next)],
                            pl.multiple_of(jnp.where(done, 0, 2 * jj + 2) * t, t))
            absorb(sa_ref, cma_ref, k_odd)
            return carry

        lax.fori_loop(0, i // 2, pair, 0)
        scores_into(sa_ref, cma_ref, qq_next, pl.multiple_of(nxt * t, t))
        if odd:
            absorb(sb_ref, cmb_ref, pl.multiple_of((i - 1) * t, t))
        for hd in heads:
            acc = acc_ref[hd]
            o = acc[:dv, :] / acc[dv:dv + 1, :]
            o = o[:, :t] - lam * o[:, t:]
            o = o * lax.rsqrt(jnp.mean(o * o, axis=0, keepdims=True) + SUBLN_EPS)
            o = o * (sg_ref[...] * (1.0 - lambda_init))
            o_ref[0, hd, pl.ds(q0, t), :] = o.T.astype(BF16)

    def q_tile_pair(a, carry):
        q_tile(2 * a, odd=False)
        q_tile(2 * a + 1, odd=True)
        return carry

    scores_into(sa_ref, cma_ref, [stacked_queries(hd, 0) for hd in heads], 0)
    lax.fori_loop(0, n_tiles // 2, q_tile_pair, 0)


def _attn(q, k, vta, lq1, lk1, lq2, lk2, subln_g, lambda_init):
    bsz, n_heads, seq, _ = q.shape
    t = ATTN_TILE
    nh = HEADS_PER_STEP
    assert (seq // t) % 2 == 0 and n_heads % nh == 0 and vta.shape[1] == n_heads * VT_ROWS
    blk = pl.BlockSpec((1, nh, seq, LANES), lambda b, h: (b, h, 0, 0))
    vec = pl.BlockSpec((1, HEAD_DIM), lambda b, h: (0, 0))
    return pl.pallas_call(
        functools.partial(_attn_kernel, lambda_init=lambda_init),
        grid=(bsz, n_heads // nh),
        in_specs=[blk, blk,
                  pl.BlockSpec((1, nh * VT_ROWS, seq), lambda b, h: (b, h, 0)),
                  vec, vec, vec, vec,
                  pl.BlockSpec((2 * HEAD_DIM, 1), lambda b, h: (0, 0))],
        out_specs=blk,
        out_shape=jax.ShapeDtypeStruct((bsz, n_heads, seq, LANES), BF16),
        scratch_shapes=[pltpu.VMEM((nh, t, 2 * t), F32),
                        pltpu.VMEM((nh, t, 2 * t), F32),
                        pltpu.VMEM((nh, 1, 2 * t), F32),
                        pltpu.VMEM((nh, 1, 2 * t), F32),
                        pltpu.VMEM((nh, 1, 2 * t), F32),
                        pltpu.VMEM((nh, VT_ROWS, 2 * t), F32)],
        compiler_params=pltpu.CompilerParams(
            dimension_semantics=("arbitrary", "arbitrary"),
            vmem_limit_bytes=ATTN_VMEM_LIMIT),
        name="attn",
    )(q, k, vta, lq1.reshape(1, -1), lk1.reshape(1, -1), lq2.reshape(1, -1),
      lk2.reshape(1, -1), subln_g.reshape(-1, 1))


def _rope_tables(seq):
    inv = 1.0 / (ROPE_THETA ** (np.arange(0, HEAD_DIM, 2, dtype=np.float64) / HEAD_DIM))
    ang = np.arange(seq, dtype=np.float64)[:, None] * inv[None, :]
    ang = np.concatenate([ang, ang, ang, ang], axis=-1)
    cos, sin = np.cos(ang), np.sin(ang)
    upper = (np.arange(LANES) % HEAD_DIM) >= HEAD_DIM // 2
    as_f32 = lambda a: jnp.asarray(a.astype(np.float32))
    return as_f32(cos), as_f32(np.where(upper, sin, 0.0)), as_f32(np.where(upper, 0.0, -sin))


def kernel(x, c, w_ada, b_ada, w_ada_final, b_ada_final, g_ffn1, g_mix, g_ffn2, g_final,
           ffn1_w1, ffn1_w3, ffn1_w2, ffn2_w1, ffn2_w3, ffn2_w2, w_in,
           lambda_q1, lambda_k1, lambda_q2, lambda_k2, subln_g, conv_w, conv_norm_g, w_out):
    bsz, seq, d = x.shape
    depth = w_ada.shape[0]
    cos, sinp, sinm = _rope_tables(seq)
    gsz = CONV_WIDTH // CONV_GROUPS
    grp = np.arange(CONV_WIDTH) // gsz
    gmat = jnp.asarray(np.where(grp[:, None] == grp[None, :], 1.0 / gsz, 0.0), dtype=BF16)
    modf3 = _ada(c, w_ada_final, b_ada_final).reshape(bsz, 2, d)

    x2d = x.reshape(bsz * seq, d)
    for l in range(depth):
        lambda_init = 0.8 - 0.6 * math.exp(-0.3 * l)
        mod3 = _ada(c, w_ada[l], b_ada[l]).reshape(bsz, N_MOD, d)
        w1a, w3a, w1b, w3b = _to_bf16(ffn1_w1[l], ffn1_w3[l], ffn2_w1[l], ffn2_w3[l])
        w2a, w2b = _to_bf16(ffn1_w2[l], ffn2_w2[l])
        w_in_b, = _to_bf16(w_in[l])
        w_out_b, = _to_bf16(w_out[l])
        w_vt = w_in_b[:, 2 * ATTN_WIDTH:3 * ATTN_WIDTH].T

        x2d = _ffn(x2d, mod3, g_ffn1[l], w1a, w3a, w2a, 0, seq)
        q, k, vt, y = _inproj(x2d, mod3, g_mix[l], w_in_b, w_vt, cos, sinp, sinm,
                              conv_w[l], conv_norm_g[l], gmat, seq)
        a = _attn(q, k, vt, lambda_q1[l], lambda_k1[l], lambda_q2[l], lambda_k2[l],
                  subln_g[l], lambda_init)
        final_args = (modf3, g_final) if l == depth - 1 else None
        x2d = _ffn(x2d, mod3, g_ffn2[l], w1b, w3b, w2b, 6, seq, (a, y, w_out_b), final_args)
    return x2d.reshape(bsz, seq, d)
```

```python
import functools
import math

import jax
import jax.numpy as jnp
import numpy as np
from jax import lax
from jax.experimental import pallas as pl
from jax.experimental.pallas import tpu as pltpu

F32 = jnp.float32
BF16 = jnp.bfloat16

LANES = 128
SUBLANES = 8

D_MODEL = 1024
ATTN_HEADS = 4
HEAD_DIM = 64
ATTN_WIDTH = ATTN_HEADS * 2 * HEAD_DIM
CONV_WIDTH = D_MODEL - ATTN_WIDTH
CONV_GROUPS = 8
CONV_K = 3
ROPE_THETA = 10000.0
NORM_EPS = 1e-6
SUBLN_EPS = 1e-5
N_MOD = 9
IN_COLS = 3 * ATTN_WIDTH + 3 * CONV_WIDTH

FF_CHUNK = 256
ROW_TILE = 1024
ROW_SUBTILE = 512
ATTN_TILE = 512
ONES_ROWS = 16
VT_ROWS = 2 * HEAD_DIM + ONES_ROWS
HEADS_PER_STEP = 4
VMEM_LIMIT = 56 * 1024 * 1024
ATTN_VMEM_LIMIT = 60 * 1024 * 1024
CAST_BLOCKS = 8
ADA_BLOCKS = 4
NT_DIMS = (((1,), (1,)), ((), ()))
Q_SCALE = math.log2(math.e) / math.sqrt(HEAD_DIM)


def _silu(a):
    return a / (1.0 + jnp.exp(-a))


def _rms(x, eps):
    return x * lax.rsqrt(jnp.mean(x * x, axis=-1, keepdims=True) + eps)


def _ada_kernel(c_ref, w_ref, b_ref, o_ref):
    o_ref[...] = jnp.dot(_silu(c_ref[...]), w_ref[...],
                         preferred_element_type=F32) + b_ref[...]


def _ada(c, w, b):
    bsz, d = c.shape
    n = w.shape[1]
    tn = n // ADA_BLOCKS
    return pl.pallas_call(
        _ada_kernel,
        grid=(n // tn,),
        in_specs=[pl.BlockSpec((bsz, d), lambda j: (0, 0)),
                  pl.BlockSpec((d, tn), lambda j: (0, j)),
                  pl.BlockSpec((1, tn), lambda j: (0, j))],
        out_specs=pl.BlockSpec((bsz, tn), lambda j: (0, j)),
        out_shape=jax.ShapeDtypeStruct((bsz, n), F32),
        name="ada",
    )(c, w, b.reshape(1, n))


def _cast_kernel(*refs):
    n = len(refs) // 2
    for w_ref, o_ref in zip(refs[:n], refs[n:]):
        o_ref[...] = w_ref[...].astype(o_ref.dtype)


def _to_bf16(*ws):
    rows, cols = ws[0].shape
    assert all(w.shape == (rows, cols) for w in ws)
    rb = rows // CAST_BLOCKS
    spec = pl.BlockSpec((rb, cols), lambda i: (i, 0))
    return pl.pallas_call(
        _cast_kernel,
        grid=(CAST_BLOCKS,),
        in_specs=[spec] * len(ws),
        out_specs=[spec] * len(ws),
        out_shape=[jax.ShapeDtypeStruct((rows, cols), BF16)] * len(ws),
        name="cast",
    )(*ws)


def _ffn_kernel(*refs, mod_row, mixer, final):
    refs = list(refs)
    x_ref, mod_ref, g_ref = refs[:3]
    del refs[:3]
    if mixer:
        a_ref, y_ref, wo_ref = refs[:3]
        del refs[:3]
    w1_ref, w3_ref, w2_ref = refs[:3]
    del refs[:3]
    if final:
        modf_ref, gf_ref = refs[:2]
        del refs[:2]
    o_ref, act_ref = refs
    shift = mod_ref[0, mod_row:mod_row + 1, :]
    scale = mod_ref[0, mod_row + 1:mod_row + 2, :]
    gate = mod_ref[0, mod_row + 2:mod_row + 3, :]
    n_chunks = w2_ref.shape[0] // FF_CHUNK
    sub_tiles = [slice(r * ROW_SUBTILE, (r + 1) * ROW_SUBTILE)
                 for r in range(x_ref.shape[0] // ROW_SUBTILE)]
    hs = []
    for rows in sub_tiles:
        x = x_ref[rows, :]
        if mixer:
            attn = jnp.concatenate([a_ref[0, hd, rows, :] for hd in range(a_ref.shape[1])],
                                   axis=-1)
            mix = (jnp.dot(attn, wo_ref[:ATTN_WIDTH, :], preferred_element_type=F32)
                   + jnp.dot(y_ref[rows, :], wo_ref[ATTN_WIDTH:, :], preferred_element_type=F32))
            x = x + mod_ref[0, mod_row - 1:mod_row, :] * mix
            o_ref[rows, :] = x
        hs.append((_rms(x, NORM_EPS) * g_ref[...] * (1.0 + scale) + shift).astype(BF16))
    for rows, h in zip(sub_tiles, hs):
        for c in range(n_chunks):
            cols = slice(c * FF_CHUNK, (c + 1) * FF_CHUNK)
            a = jnp.dot(h, w1_ref[:, cols], preferred_element_type=F32)
            b = jnp.dot(h, w3_ref[:, cols], preferred_element_type=F32)
            act_ref[rows, cols] = (_silu(a) * b).astype(BF16)
        x = o_ref[rows, :] if mixer else x_ref[rows, :]
        y = x + 0.5 * gate * jnp.dot(act_ref[rows, :], w2_ref[...],
                                     preferred_element_type=F32)
        if final:
            y = (_rms(y, NORM_EPS) * gf_ref[...] * (1.0 + modf_ref[0, 1:2, :])
                 + modf_ref[0, 0:1, :])
        o_ref[rows, :] = y


def _ffn(x2d, mod3, g, w1, w3, w2, mod_row, rows_per_batch, mixer_args=None, final_args=None):
    m, d = x2d.shape
    tm = ROW_TILE
    tiles_per_batch = rows_per_batch // tm
    row = lambda i: (i, 0)
    fix = lambda i: (0, 0)
    per_batch = lambda i: (i // tiles_per_batch, 0, 0)
    resident = lambda w: pl.BlockSpec(w.shape, fix, pipeline_mode=pl.Buffered(1))
    in_specs = [pl.BlockSpec((tm, d), row),
                pl.BlockSpec((1, N_MOD, d), per_batch),
                pl.BlockSpec((1, d), fix)]
    args = [x2d, mod3, g.reshape(1, d)]
    if mixer_args is not None:
        a4d, y2d, w_out = mixer_args
        in_specs += [pl.BlockSpec((1, a4d.shape[1], tm, a4d.shape[3]),
                                  lambda i: (i // tiles_per_batch, 0, i % tiles_per_batch, 0)),
                     pl.BlockSpec((tm, y2d.shape[1]), row), resident(w_out)]
        args += [a4d, y2d, w_out]
    in_specs += [resident(w1), resident(w3), resident(w2)]
    args += [w1, w3, w2]
    if final_args is not None:
        modf3, gf = final_args
        in_specs += [pl.BlockSpec((1, 2, d), per_batch), pl.BlockSpec((1, d), fix)]
        args += [modf3, gf.reshape(1, d)]
    return pl.pallas_call(
        functools.partial(_ffn_kernel, mod_row=mod_row, mixer=mixer_args is not None,
                          final=final_args is not None),
        grid=(m // tm,),
        in_specs=in_specs,
        out_specs=pl.BlockSpec((tm, d), row),
        out_shape=jax.ShapeDtypeStruct((m, d), F32),
        scratch_shapes=[pltpu.VMEM((tm, w2.shape[0]), BF16)],
        compiler_params=pltpu.CompilerParams(
            dimension_semantics=("arbitrary",), vmem_limit_bytes=VMEM_LIMIT),
        name="ffn" + ("_mixer" if mixer_args is not None else "")
        + ("_final" if final_args is not None else ""),
    )(*args)


def _inproj_kernel(x_ref, mod_ref, g_ref, w_ref, wvt_ref, cos_ref, sinp_ref, sinm_ref,
                   convw_ref, cng_ref, gmat_ref,
                   q_ref, k_ref, vt_ref, y_ref, halo_ref, *, mod_row, tiles_per_batch):
    shift = mod_ref[0, mod_row:mod_row + 1, :]
    scale = mod_ref[0, mod_row + 1:mod_row + 2, :]
    half = HEAD_DIM // 2
    w = convw_ref[...]

    @pl.when(pl.program_id(0) % tiles_per_batch == 0)
    def _():
        halo_ref[...] = jnp.zeros_like(halo_ref)

    prev = halo_ref[...]
    for r in range(x_ref.shape[0] // ROW_SUBTILE):
        rows = slice(r * ROW_SUBTILE, (r + 1) * ROW_SUBTILE)
        h = (_rms(x_ref[rows, :], NORM_EPS) * g_ref[...] * (1.0 + scale) + shift).astype(BF16)
        zqk = jnp.dot(h, w_ref[:, :2 * ATTN_WIDTH], preferred_element_type=F32)
        zcv = jnp.dot(h, w_ref[:, 3 * ATTN_WIDTH:], preferred_element_type=F32)
        vt = lax.dot_general(wvt_ref[...], h, NT_DIMS, preferred_element_type=F32).astype(BF16)
        for j in range(ATTN_HEADS):
            vt_ref[0, j * VT_ROWS:j * VT_ROWS + LANES, rows] = vt[j * LANES:(j + 1) * LANES, :]
            vt_ref[0, j * VT_ROWS + LANES:(j + 1) * VT_ROWS, rows] = jnp.ones(
                (ONES_ROWS, ROW_SUBTILE), BF16)
        cos = cos_ref[rows, :]
        sinp = sinp_ref[rows, :]
        sinm = sinm_ref[rows, :]
        for j in range(ATTN_WIDTH // LANES):
            for base, out_ref, mult in ((0, q_ref, Q_SCALE), (ATTN_WIDTH, k_ref, 1.0)):
                t = zqk[:, base + j * LANES: base + (j + 1) * LANES]
                rot = (t * cos + pltpu.roll(t, half, axis=1) * sinp
                       + pltpu.roll(t, LANES - half, axis=1) * sinm)
                out_ref[0, j, rows, :] = (rot * mult).astype(BF16)

        gate_b = zcv[:, :CONV_WIDTH]
        cu = zcv[:, CONV_WIDTH:2 * CONV_WIDTH] * zcv[:, 2 * CONV_WIDTH:]
        buf = jnp.concatenate([prev, cu], axis=0)
        cu_m1 = pltpu.roll(buf, 1, axis=0)[SUBLANES:, :]
        cu_m2 = pltpu.roll(buf, 2, axis=0)[SUBLANES:, :]
        prev = cu[ROW_SUBTILE - SUBLANES:, :]
        y = gate_b * (w[0:1, :] * cu_m2 + w[1:2, :] * cu_m1 + w[2:3, :] * cu)
        y2 = y * y
        y2_hi = y2.astype(BF16)
        y2_lo = (y2 - y2_hi.astype(F32)).astype(BF16)
        ms = (jnp.dot(y2_hi, gmat_ref[...], preferred_element_type=F32)
              + jnp.dot(y2_lo, gmat_ref[...], preferred_element_type=F32))
        y_ref[rows, :] = (y * lax.rsqrt(ms + NORM_EPS) * cng_ref[...]).astype(BF16)
    halo_ref[...] = prev


def _inproj(x2d, mod3, g, w_in, w_vt, cos, sinp, sinm, conv_w, conv_norm_g, gmat,
            rows_per_batch):
    m, d = x2d.shape
    tm = ROW_TILE
    tiles_per_batch = rows_per_batch // tm
    row = lambda i: (i, 0)
    fix = lambda i: (0, 0)
    pos = lambda i: (i % tiles_per_batch, 0)
    n_batch = m // rows_per_batch
    head_major = pl.BlockSpec((1, ATTN_HEADS, tm, LANES),
                              lambda i: (i // tiles_per_batch, 0, i % tiles_per_batch, 0))
    out_sd = jax.ShapeDtypeStruct((n_batch, ATTN_HEADS, rows_per_batch, LANES), BF16)
    return pl.pallas_call(
        functools.partial(_inproj_kernel, mod_row=3, tiles_per_batch=tiles_per_batch),
        grid=(m // tm,),
        in_specs=[pl.BlockSpec((tm, d), row),
                  pl.BlockSpec((1, N_MOD, d), lambda i: (i // tiles_per_batch, 0, 0)),
                  pl.BlockSpec((1, d), fix),
                  pl.BlockSpec(w_in.shape, fix, pipeline_mode=pl.Buffered(1)),
                  pl.BlockSpec(w_vt.shape, fix, pipeline_mode=pl.Buffered(1)),
                  pl.BlockSpec((tm, LANES), pos),
                  pl.BlockSpec((tm, LANES), pos),
                  pl.BlockSpec((tm, LANES), pos),
                  pl.BlockSpec((CONV_K, CONV_WIDTH), fix),
                  pl.BlockSpec((1, CONV_WIDTH), fix),
                  pl.BlockSpec((CONV_WIDTH, CONV_WIDTH), fix)],
        out_specs=[head_major, head_major,
                   pl.BlockSpec((1, ATTN_HEADS * VT_ROWS, tm),
                                lambda i: (i // tiles_per_batch, 0, i % tiles_per_batch)),
                   pl.BlockSpec((tm, CONV_WIDTH), row)],
        out_shape=[out_sd, out_sd,
                   jax.ShapeDtypeStruct((n_batch, ATTN_HEADS * VT_ROWS, rows_per_batch), BF16),
                   jax.ShapeDtypeStruct((m, CONV_WIDTH), BF16)],
        scratch_shapes=[pltpu.VMEM((SUBLANES, CONV_WIDTH), F32)],
        compiler_params=pltpu.CompilerParams(
            dimension_semantics=("arbitrary",), vmem_limit_bytes=VMEM_LIMIT),
        name="inproj",
    )(x2d, mod3, g.reshape(1, d), w_in, w_vt, cos, sinp, sinm, conv_w,
      conv_norm_g.reshape(1, CONV_WIDTH), gmat)


def _attn_kernel(q_ref, k_ref, vta_ref, lq1_ref, lk1_ref, lq2_ref, lk2_ref, sg_ref,
                 o_ref, sa_ref, sb_ref, cma_ref, cmb_ref, m_ref, acc_ref, *, lambda_init):
    seq = q_ref.shape[2]
    t = ATTN_TILE
    dv = 2 * HEAD_DIM
    n_tiles = seq // t
    heads = range(HEADS_PER_STEP)
    lam = (jnp.exp(jnp.sum(lq1_ref[...] * lk1_ref[...], axis=-1, keepdims=True))
           - jnp.exp(jnp.sum(lq2_ref[...] * lk2_ref[...], axis=-1, keepdims=True))
           + lambda_init)

    def stacked_queries(hd, i):
        q = q_ref[0, hd, pl.ds(pl.multiple_of(i * t, t), t), :]
        first_map = lax.broadcasted_iota(jnp.int32, (t, LANES), 1) < HEAD_DIM
        zero = jnp.zeros_like(q)
        return jnp.concatenate([jnp.where(first_map, q, zero),
                                jnp.where(first_map, zero, q)], axis=0)

    def scores_into(s_buf, cm_buf, qq, k0):
        for hd in heads:
            s = lax.dot_general(k_ref[0, hd, pl.ds(k0, t), :], qq[hd], NT_DIMS,
                                preferred_element_type=F32)
            s_buf[hd] = s
            cm_buf[hd] = jnp.max(s, axis=0, keepdims=True)

    def absorb(s_buf, cm_buf, k0, mask=False):
        for hd in heads:
            s = s_buf[hd]
            if mask:
                kpos = lax.broadcasted_iota(jnp.int32, (t, 2 * t), 0)
                qpos = lax.broadcasted_iota(jnp.int32, (t, 2 * t), 1)
                qpos = jnp.where(qpos >= t, qpos - t, qpos)
                s = jnp.where(kpos <= qpos, s, -jnp.inf)
                cm = jnp.max(s, axis=0, keepdims=True)
            else:
                cm = cm_buf[hd]
            m = m_ref[hd]
            m_new = jnp.maximum(m, cm)
            alpha = jnp.exp2(m - m_new)
            p = jnp.exp2(s - m_new).astype(BF16)
            m_ref[hd] = m_new
            pv = jnp.dot(vta_ref[0, hd * VT_ROWS:(hd + 1) * VT_ROWS, pl.ds(k0, t)], p,
                         preferred_element_type=F32)
            acc_ref[hd] = alpha * acc_ref[hd] + pv

    def q_tile(i, odd):
        q0 = pl.multiple_of(i * t, t)
        nxt = jnp.minimum(i + 1, n_tiles - 1)
        qq = [stacked_queries(hd, i) for hd in heads]
        qq_next = [stacked_queries(hd, nxt) for hd in heads]
        m_ref[...] = jnp.full(m_ref.shape, -jnp.inf, F32)
        acc_ref[...] = jnp.zeros_like(acc_ref)
        if not odd:
            scores_into(sb_ref, cmb_ref, [jnp.where(i == 0, b, a) for a, b in zip(qq, qq_next)], 0)
        absorb(sa_ref, cma_ref, q0, mask=True)

        def pair(jj, carry):
            k_even = pl.multiple_of(2 * jj * t, t)
            k_odd = pl.multiple_of(k_even + t, t)
            scores_into(sa_ref, cma_ref, qq, k_odd)
            absorb(sb_ref, cmb_ref, k_even)
            if odd:
                scores_into(sb_ref, cmb_ref, qq, pl.multiple_of(k_odd + t, t))
            else:
                done = 2 * jj + 2 >= i
                scores_into(sb_ref, cmb_ref,
                            [jnp.where(done, b, a) for a, b in zip(qq, qq_next)],
                            pl.multiple_of(jnp.where(done, 0, 2 * jj + 2) * t, t))
            absorb(sa_ref, cma_ref, k_odd)
            return carry

        lax.fori_loop(0, i // 2, pair, 0)
        scores_into(sa_ref, cma_ref, qq_next, pl.multiple_of(nxt * t, t))
        if odd:
            absorb(sb_ref, cmb_ref, pl.multiple_of((i - 1) * t, t))
        for hd in heads:
            acc = acc_ref[hd]
            o = acc[:dv, :] / acc[dv:dv + 1, :]
            o = o[:, :t] - lam * o[:, t:]
            o = o * lax.rsqrt(jnp.mean(o * o, axis=0, keepdims=True) + SUBLN_EPS)
            o = o * (sg_ref[...] * (1.0 - lambda_init))
            o_ref[0, hd, pl.ds(q0, t), :] = o.T.astype(BF16)

    def q_tile_pair(a, carry):
        q_tile(2 * a, odd=False)
        q_tile(2 * a + 1, odd=True)
        return carry

    scores_into(sa_ref, cma_ref, [stacked_queries(hd, 0) for hd in heads], 0)
    lax.fori_loop(0, n_tiles // 2, q_tile_pair, 0)


def _attn(q, k, vta, lq1, lk1, lq2, lk2, subln_g, lambda_init):
    bsz, n_heads, seq, _ = q.shape
    t = ATTN_TILE
    nh = HEADS_PER_STEP
    assert (seq // t) % 2 == 0 and n_heads % nh == 0 and vta.shape[1] == n_heads * VT_ROWS
    blk = pl.BlockSpec((1, nh, seq, LANES), lambda b, h: (b, h, 0, 0))
    vec = pl.BlockSpec((1, HEAD_DIM), lambda b, h: (0, 0))
    return pl.pallas_call(
        functools.partial(_attn_kernel, lambda_init=lambda_init),
        grid=(bsz, n_heads // nh),
        in_specs=[blk, blk,
                  pl.BlockSpec((1, nh * VT_ROWS, seq), lambda b, h: (b, h, 0)),
                  vec, vec, vec, vec,
                  pl.BlockSpec((2 * HEAD_DIM, 1), lambda b, h: (0, 0))],
        out_specs=blk,
        out_shape=jax.ShapeDtypeStruct((bsz, n_heads, seq, LANES), BF16),
        scratch_shapes=[pltpu.VMEM((nh, t, 2 * t), F32),
                        pltpu.VMEM((nh, t, 2 * t), F32),
                        pltpu.VMEM((nh, 1, 2 * t), F32),
                        pltpu.VMEM((nh, 1, 2 * t), F32),
                        pltpu.VMEM((nh, 1, 2 * t), F32),
                        pltpu.VMEM((nh, VT_ROWS, 2 * t), F32)],
        compiler_params=pltpu.CompilerParams(
            dimension_semantics=("arbitrary", "arbitrary"),
            vmem_limit_bytes=ATTN_VMEM_LIMIT),
        name="attn",
    )(q, k, vta, lq1.reshape(1, -1), lk1.reshape(1, -1), lq2.reshape(1, -1),
      lk2.reshape(1, -1), subln_g.reshape(-1, 1))


def _rope_tables(seq):
    inv = 1.0 / (ROPE_THETA ** (np.arange(0, HEAD_DIM, 2, dtype=np.float64) / HEAD_DIM))
    ang = np.arange(seq, dtype=np.float64)[:, None] * inv[None, :]
    ang = np.concatenate([ang, ang, ang, ang], axis=-1)
    cos, sin = np.cos(ang), np.sin(ang)
    upper = (np.arange(LANES) % HEAD_DIM) >= HEAD_DIM // 2
    as_f32 = lambda a: jnp.asarray(a.astype(np.float32))
    return as_f32(cos), as_f32(np.where(upper, sin, 0.0)), as_f32(np.where(upper, 0.0, -sin))


def kernel(x, c, w_ada, b_ada, w_ada_final, b_ada_final, g_ffn1, g_mix, g_ffn2, g_final,
           ffn1_w1, ffn1_w3, ffn1_w2, ffn2_w1, ffn2_w3, ffn2_w2, w_in,
           lambda_q1, lambda_k1, lambda_q2, lambda_k2, subln_g, conv_w, conv_norm_g, w_out):
    bsz, seq, d = x.shape
    depth = w_ada.shape[0]
    cos, sinp, sinm = _rope_tables(seq)
    gsz = CONV_WIDTH // CONV_GROUPS
    grp = np.arange(CONV_WIDTH) // gsz
    gmat = jnp.asarray(np.where(grp[:, None] == grp[None, :], 1.0 / gsz, 0.0), dtype=BF16)
    modf3 = _ada(c, w_ada_final, b_ada_final).reshape(bsz, 2, d)

    x2d = x.reshape(bsz * seq, d)
    for l in range(depth):
        lambda_init = 0.8 - 0.6 * math.exp(-0.3 * l)
        mod3 = _ada(c, w_ada[l], b_ada[l]).reshape(bsz, N_MOD, d)
        w1a, w3a, w1b, w3b = _to_bf16(ffn1_w1[l], ffn1_w3[l], ffn2_w1[l], ffn2_w3[l])
        w2a, w2b = _to_bf16(ffn1_w2[l], ffn2_w2[l])
        w_in_b, = _to_bf16(w_in[l])
        w_out_b, = _to_bf16(w_out[l])
        w_vt = w_in_b[:, 2 * ATTN_WIDTH:3 * ATTN_WIDTH].T

        x2d = _ffn(x2d, mod3, g_ffn1[l], w1a, w3a, w2a, 0, seq)
        q, k, vt, y = _inproj(x2d, mod3, g_mix[l], w_in_b, w_vt, cos, sinp, sinm,
                              conv_w[l], conv_norm_g[l], gmat, seq)
        a = _attn(q, k, vt, lambda_q1[l], lambda_k1[l], lambda_q2[l], lambda_k2[l],
                  subln_g[l], lambda_init)
        final_args = (modf3, g_final) if l == depth - 1 else None
        x2d = _ffn(x2d, mod3, g_ffn2[l], w1b, w3b, w2b, 6, seq, (a, y, w_out_b), final_args)
    return x2d.reshape(bsz, seq, d)
```

```python
import functools
import math

import jax
import jax.numpy as jnp
import numpy as np
from jax import lax
from jax.experimental import pallas as pl
from jax.experimental.pallas import tpu as pltpu

F32 = jnp.float32
BF16 = jnp.bfloat16

LANES = 128
SUBLANES = 8

D_MODEL = 1024
ATTN_HEADS = 4
HEAD_DIM = 64
ATTN_WIDTH = ATTN_HEADS * 2 * HEAD_DIM
CONV_WIDTH = D_MODEL - ATTN_WIDTH
CONV_GROUPS = 8
CONV_K = 3
ROPE_THETA = 10000.0
NORM_EPS = 1e-6
SUBLN_EPS = 1e-5
N_MOD = 9
IN_COLS = 3 * ATTN_WIDTH + 3 * CONV_WIDTH

FF_CHUNK = 256
ROW_TILE = 1024
ROW_SUBTILE = 512
ATTN_TILE = 512
ONES_ROWS = 16
VT_ROWS = 2 * HEAD_DIM + ONES_ROWS
HEADS_PER_STEP = 4
VMEM_LIMIT = 56 * 1024 * 1024
ATTN_VMEM_LIMIT = 60 * 1024 * 1024
CAST_BLOCKS = 4
ADA_COLS = 1024
NT_DIMS = (((1,), (1,)), ((), ()))
Q_SCALE = math.log2(math.e) / math.sqrt(HEAD_DIM)


def _silu(a):
    return a / (1.0 + jnp.exp(-a))


def _rms(x, eps):
    return x * lax.rsqrt(jnp.mean(x * x, axis=-1, keepdims=True) + eps)


def _ada_kernel(c_ref, w_ref, b_ref, o_ref):
    o_ref[...] = jnp.dot(_silu(c_ref[...]), w_ref[...],
                         preferred_element_type=F32) + b_ref[...]


def _ada(c, w, b):
    bsz, d = c.shape
    n = w.shape[1]
    tn = ADA_COLS
    return pl.pallas_call(
        _ada_kernel,
        grid=(n // tn,),
        in_specs=[pl.BlockSpec((bsz, d), lambda j: (0, 0)),
                  pl.BlockSpec((d, tn), lambda j: (0, j)),
                  pl.BlockSpec((1, tn), lambda j: (0, j))],
        out_specs=pl.BlockSpec((bsz, tn), lambda j: (0, j)),
        out_shape=jax.ShapeDtypeStruct((bsz, n), F32),
        name="ada",
    )(c, w, b.reshape(1, n))


def _cast_kernel(*refs):
    n = len(refs) // 2
    for w_ref, o_ref in zip(refs[:n], refs[n:]):
        o_ref[...] = w_ref[...].astype(o_ref.dtype)


def _to_bf16(*ws):
    rows, cols = ws[0].shape
    assert all(w.shape == (rows, cols) for w in ws)
    rb = rows // CAST_BLOCKS
    spec = pl.BlockSpec((rb, cols), lambda i: (i, 0))
    return pl.pallas_call(
        _cast_kernel,
        grid=(CAST_BLOCKS,),
        in_specs=[spec] * len(ws),
        out_specs=[spec] * len(ws),
        out_shape=[jax.ShapeDtypeStruct((rows, cols), BF16)] * len(ws),
        name="cast",
    )(*ws)


def _ffn_kernel(*refs, mod_row, mixer, final):
    refs = list(refs)
    x_ref, mod_ref, g_ref = refs[:3]
    del refs[:3]
    if mixer:
        a_ref, y_ref, wo_ref = refs[:3]
        del refs[:3]
    w1_ref, w3_ref, w2_ref = refs[:3]
    del refs[:3]
    if final:
        modf_ref, gf_ref = refs[:2]
        del refs[:2]
    o_ref, act_ref = refs
    shift = mod_ref[0, mod_row:mod_row + 1, :]
    scale = mod_ref[0, mod_row + 1:mod_row + 2, :]
    gate = mod_ref[0, mod_row + 2:mod_row + 3, :]
    n_chunks = w2_ref.shape[0] // FF_CHUNK
    sub_tiles = [slice(r * ROW_SUBTILE, (r + 1) * ROW_SUBTILE)
                 for r in range(x_ref.shape[0] // ROW_SUBTILE)]
    hs = []
    for rows in sub_tiles:
        x = x_ref[rows, :]
        if mixer:
            attn = jnp.concatenate([a_ref[0, hd, rows, :] for hd in range(a_ref.shape[1])],
                                   axis=-1)
            mix = (jnp.dot(attn, wo_ref[:ATTN_WIDTH, :], preferred_element_type=F32)
                   + jnp.dot(y_ref[rows, :], wo_ref[ATTN_WIDTH:, :], preferred_element_type=F32))
            x = x + mod_ref[0, mod_row - 1:mod_row, :] * mix
            o_ref[rows, :] = x
        hs.append((_rms(x, NORM_EPS) * g_ref[...] * (1.0 + scale) + shift).astype(BF16))
    for rows, h in zip(sub_tiles, hs):
        for c in range(n_chunks):
            cols = slice(c * FF_CHUNK, (c + 1) * FF_CHUNK)
            a = jnp.dot(h, w1_ref[:, cols], preferred_element_type=F32)
            b = jnp.dot(h, w3_ref[:, cols], preferred_element_type=F32)
            act_ref[rows, cols] = (_silu(a) * b).astype(BF16)
        x = o_ref[rows, :] if mixer else x_ref[rows, :]
        y = x + 0.5 * gate * jnp.dot(act_ref[rows, :], w2_ref[...],
                                     preferred_element_type=F32)
        if final:
            y = (_rms(y, NORM_EPS) * gf_ref[...] * (1.0 + modf_ref[0, 1:2, :])
                 + modf_ref[0, 0:1, :])
        o_ref[rows, :] = y


def _ffn(x2d, mod3, g, w1, w3, w2, mod_row, rows_per_batch, mixer_args=None, final_args=None):
    m, d = x2d.shape
    tm = ROW_TILE
    tiles_per_batch = rows_per_batch // tm
    row = lambda i: (i, 0)
    fix = lambda i: (0, 0)
    per_batch = lambda i: (i // tiles_per_batch, 0, 0)
    resident = lambda w: pl.BlockSpec(w.shape, fix, pipeline_mode=pl.Buffered(1))
    in_specs = [pl.BlockSpec((tm, d), row),
                pl.BlockSpec((1, N_MOD, d), per_batch),
                pl.BlockSpec((1, d), fix)]
    args = [x2d, mod3, g.reshape(1, d)]
    if mixer_args is not None:
        a4d, y2d, w_out = mixer_args
        in_specs += [pl.BlockSpec((1, a4d.shape[1], tm, a4d.shape[3]),
                                  lambda i: (i // tiles_per_batch, 0, i % tiles_per_batch, 0)),
                     pl.BlockSpec((tm, y2d.shape[1]), row), resident(w_out)]
        args += [a4d, y2d, w_out]
    in_specs += [resident(w1), resident(w3), resident(w2)]
    args += [w1, w3, w2]
    if final_args is not None:
        modf3, gf = final_args
        in_specs += [pl.BlockSpec((1, 2, d), per_batch), pl.BlockSpec((1, d), fix)]
        args += [modf3, gf.reshape(1, d)]
    return pl.pallas_call(
        functools.partial(_ffn_kernel, mod_row=mod_row, mixer=mixer_args is not None,
                          final=final_args is not None),
        grid=(m // tm,),
        in_specs=in_specs,
        out_specs=pl.BlockSpec((tm, d), row),
        out_shape=jax.ShapeDtypeStruct((m, d), F32),
        scratch_shapes=[pltpu.VMEM((tm, w2.shape[0]), BF16)],
        compiler_params=pltpu.CompilerParams(
            dimension_semantics=("arbitrary",), vmem_limit_bytes=VMEM_LIMIT),
        name="ffn" + ("_mixer" if mixer_args is not None else "")
        + ("_final" if final_args is not None else ""),
    )(*args)


def _inproj_kernel(x_ref, mod_ref, g_ref, w_ref, wvt_ref, cos_ref, sinp_ref, sinm_ref,
                   convw_ref, cng_ref, gmat_ref,
                   q_ref, k_ref, vt_ref, y_ref, halo_ref, *, mod_row, tiles_per_batch):
    shift = mod_ref[0, mod_row:mod_row + 1, :]
    scale = mod_ref[0, mod_row + 1:mod_row + 2, :]
    half = HEAD_DIM // 2
    w = convw_ref[...]

    @pl.when(pl.program_id(0) % tiles_per_batch == 0)
    def _():
        halo_ref[...] = jnp.zeros_like(halo_ref)

    prev = halo_ref[...]
    for r in range(x_ref.shape[0] // ROW_SUBTILE):
        rows = slice(r * ROW_SUBTILE, (r + 1) * ROW_SUBTILE)
        h = (_rms(x_ref[rows, :], NORM_EPS) * g_ref[...] * (1.0 + scale) + shift).astype(BF16)
        zqk = jnp.dot(h, w_ref[:, :2 * ATTN_WIDTH], preferred_element_type=F32)
        zcv = jnp.dot(h, w_ref[:, 3 * ATTN_WIDTH:], preferred_element_type=F32)
        vt = lax.dot_general(wvt_ref[...], h, NT_DIMS, preferred_element_type=F32).astype(BF16)
        for j in range(ATTN_HEADS):
            vt_ref[0, j * VT_ROWS:j * VT_ROWS + LANES, rows] = vt[j * LANES:(j + 1) * LANES, :]
            vt_ref[0, j * VT_ROWS + LANES:(j + 1) * VT_ROWS, rows] = jnp.ones(
                (ONES_ROWS, ROW_SUBTILE), BF16)
        cos = cos_ref[rows, :]
        sinp = sinp_ref[rows, :]
        sinm = sinm_ref[rows, :]
        for j in range(ATTN_WIDTH // LANES):
            for base, out_ref, mult in ((0, q_ref, Q_SCALE), (ATTN_WIDTH, k_ref, 1.0)):
                t = zqk[:, base + j * LANES: base + (j + 1) * LANES]
                rot = (t * cos + pltpu.roll(t, half, axis=1) * sinp
                       + pltpu.roll(t, LANES - half, axis=1) * sinm)
                out_ref[0, j, rows, :] = (rot * mult).astype(BF16)

        gate_b = zcv[:, :CONV_WIDTH]
        cu = zcv[:, CONV_WIDTH:2 * CONV_WIDTH] * zcv[:, 2 * CONV_WIDTH:]
        buf = jnp.concatenate([prev, cu], axis=0)
        cu_m1 = pltpu.roll(buf, 1, axis=0)[SUBLANES:, :]
        cu_m2 = pltpu.roll(buf, 2, axis=0)[SUBLANES:, :]
        prev = cu[ROW_SUBTILE - SUBLANES:, :]
        y = gate_b * (w[0:1, :] * cu_m2 + w[1:2, :] * cu_m1 + w[2:3, :] * cu)
        y2 = y * y
        y2_hi = y2.astype(BF16)
        y2_lo = (y2 - y2_hi.astype(F32)).astype(BF16)
        ms = (jnp.dot(y2_hi, gmat_ref[...], preferred_element_type=F32)
              + jnp.dot(y2_lo, gmat_ref[...], preferred_element_type=F32))
        y_ref[rows, :] = (y * lax.rsqrt(ms + NORM_EPS) * cng_ref[...]).astype(BF16)
    halo_ref[...] = prev


def _inproj(x2d, mod3, g, w_in, w_vt, cos, sinp, sinm, conv_w, conv_norm_g, gmat,
            rows_per_batch):
    m, d = x2d.shape
    tm = ROW_TILE
    tiles_per_batch = rows_per_batch // tm
    row = lambda i: (i, 0)
    fix = lambda i: (0, 0)
    pos = lambda i: (i % tiles_per_batch, 0)
    n_batch = m // rows_per_batch
    head_major = pl.BlockSpec((1, ATTN_HEADS, tm, LANES),
                              lambda i: (i // tiles_per_batch, 0, i % tiles_per_batch, 0))
    out_sd = jax.ShapeDtypeStruct((n_batch, ATTN_HEADS, rows_per_batch, LANES), BF16)
    return pl.pallas_call(
        functools.partial(_inproj_kernel, mod_row=3, tiles_per_batch=tiles_per_batch),
        grid=(m // tm,),
        in_specs=[pl.BlockSpec((tm, d), row),
                  pl.BlockSpec((1, N_MOD, d), lambda i: (i // tiles_per_batch, 0, 0)),
                  pl.BlockSpec((1, d), fix),
                  pl.BlockSpec(w_in.shape, fix, pipeline_mode=pl.Buffered(1)),
                  pl.BlockSpec(w_vt.shape, fix, pipeline_mode=pl.Buffered(1)),
                  pl.BlockSpec((tm, LANES), pos),
                  pl.BlockSpec((tm, LANES), pos),
                  pl.BlockSpec((tm, LANES), pos),
                  pl.BlockSpec((CONV_K, CONV_WIDTH), fix),
                  pl.BlockSpec((1, CONV_WIDTH), fix),
                  pl.BlockSpec((CONV_WIDTH, CONV_WIDTH), fix)],
        out_specs=[head_major, head_major,
                   pl.BlockSpec((1, ATTN_HEADS * VT_ROWS, tm),
                                lambda i: (i // tiles_per_batch, 0, i % tiles_per_batch)),
                   pl.BlockSpec((tm, CONV_WIDTH), row)],
        out_shape=[out_sd, out_sd,
                   jax.ShapeDtypeStruct((n_batch, ATTN_HEADS * VT_ROWS, rows_per_batch), BF16),
                   jax.ShapeDtypeStruct((m, CONV_WIDTH), BF16)],
        scratch_shapes=[pltpu.VMEM((SUBLANES, CONV_WIDTH), F32)],
        compiler_params=pltpu.CompilerParams(
            dimension_semantics=("arbitrary",), vmem_limit_bytes=VMEM_LIMIT),
        name="inproj",
    )(x2d, mod3, g.reshape(1, d), w_in, w_vt, cos, sinp, sinm, conv_w,
      conv_norm_g.reshape(1, CONV_WIDTH), gmat)


def _attn_kernel(q_ref, k_ref, vta_ref, lq1_ref, lk1_ref, lq2_ref, lk2_ref, sg_ref,
                 o_ref, sa_ref, sb_ref, cma_ref, cmb_ref, m_ref, acc_ref, *, lambda_init):
    seq = q_ref.shape[2]
    t = ATTN_TILE
    dv = 2 * HEAD_DIM
    n_tiles = seq // t
    heads = range(HEADS_PER_STEP)
    lam = (jnp.exp(jnp.sum(lq1_ref[...] * lk1_ref[...], axis=-1, keepdims=True))
           - jnp.exp(jnp.sum(lq2_ref[...] * lk2_ref[...], axis=-1, keepdims=True))
           + lambda_init)

    def stacked_queries(hd, i):
        q = q_ref[0, hd, pl.ds(pl.multiple_of(i * t, t), t), :]
        first_map = lax.broadcasted_iota(jnp.int32, (t, LANES), 1) < HEAD_DIM
        zero = jnp.zeros_like(q)
        return jnp.concatenate([jnp.where(first_map, q, zero),
                                jnp.where(first_map, zero, q)], axis=0)

    def scores_into(s_buf, cm_buf, qq, k0):
        for hd in heads:
            s = lax.dot_general(k_ref[0, hd, pl.ds(k0, t), :], qq[hd], NT_DIMS,
                                preferred_element_type=F32)
            s_buf[hd] = s
            cm_buf[hd] = jnp.max(s, axis=0, keepdims=True)

    def absorb(s_buf, cm_buf, k0, mask=False):
        for hd in heads:
            s = s_buf[hd]
            if mask:
                kpos = lax.broadcasted_iota(jnp.int32, (t, 2 * t), 0)
                qpos = lax.broadcasted_iota(jnp.int32, (t, 2 * t), 1)
                qpos = jnp.where(qpos >= t, qpos - t, qpos)
                s = jnp.where(kpos <= qpos, s, -jnp.inf)
                cm = jnp.max(s, axis=0, keepdims=True)
            else:
                cm = cm_buf[hd]
            m = m_ref[hd]
            m_new = jnp.maximum(m, cm)
            alpha = jnp.exp2(m - m_new)
            p = jnp.exp2(s - m_new).astype(BF16)
            m_ref[hd] = m_new
            pv = jnp.dot(vta_ref[0, hd * VT_ROWS:(hd + 1) * VT_ROWS, pl.ds(k0, t)], p,
                         preferred_element_type=F32)
            acc_ref[hd] = alpha * acc_ref[hd] + pv

    def q_tile(i, odd):
        q0 = pl.multiple_of(i * t, t)
        nxt = jnp.minimum(i + 1, n_tiles - 1)
        qq = [stacked_queries(hd, i) for hd in heads]
        qq_next = [stacked_queries(hd, nxt) for hd in heads]
        m_ref[...] = jnp.full(m_ref.shape, -jnp.inf, F32)
        acc_ref[...] = jnp.zeros_like(acc_ref)
        if not odd:
            scores_into(sb_ref, cmb_ref, [jnp.where(i == 0, b, a) for a, b in zip(qq, qq_next)], 0)
        absorb(sa_ref, cma_ref, q0, mask=True)

        def pair(jj, carry):
            k_even = pl.multiple_of(2 * jj * t, t)
            k_odd = pl.multiple_of(k_even + t, t)
            scores_into(sa_ref, cma_ref, qq, k_odd)
            absorb(sb_ref, cmb_ref, k_even)
            if odd:
                scores_into(sb_ref, cmb_ref, qq, pl.multiple_of(k_odd + t, t))
            else:
                done = 2 * jj + 2 >= i
                scores_into(sb_ref, cmb_ref,
                            [jnp.where(done, b, a) for a, b in zip(qq, qq_next)],
                            pl.multiple_of(jnp.where(done, 0, 2 * jj + 2) * t, t))
            absorb(sa_ref, cma_ref, k_odd)
            return carry

        lax.fori_loop(0, i // 2, pair, 0)
        scores_into(sa_ref, cma_ref, qq_next, pl.multiple_of(nxt * t, t))
        if odd:
            absorb(sb_ref, cmb_ref, pl.multiple_of((i - 1) * t, t))
        for hd in heads:
            acc = acc_ref[hd]
            o = acc[:dv, :] / acc[dv:dv + 1, :]
            o = o[:, :t] - lam * o[:, t:]
            o = o * lax.rsqrt(jnp.mean(o * o, axis=0, keepdims=True) + SUBLN_EPS)
            o = o * (sg_ref[...] * (1.0 - lambda_init))
            o_ref[0, hd, pl.ds(q0, t), :] = o.T.astype(BF16)

    def q_tile_pair(a, carry):
        q_tile(2 * a, odd=False)
        q_tile(2 * a + 1, odd=True)
        return carry

    scores_into(sa_ref, cma_ref, [stacked_queries(hd, 0) for hd in heads], 0)
    lax.fori_loop(0, n_tiles // 2, q_tile_pair, 0)


def _attn(q, k, vta, lq1, lk1, lq2, lk2, subln_g, lambda_init):
    bsz, n_heads, seq, _ = q.shape
    t = ATTN_TILE
    nh = HEADS_PER_STEP
    assert (seq // t) % 2 == 0 and n_heads % nh == 0 and vta.shape[1] == n_heads * VT_ROWS
    blk = pl.BlockSpec((1, nh, seq, LANES), lambda b, h: (b, h, 0, 0))
    vec = pl.BlockSpec((1, HEAD_DIM), lambda b, h: (0, 0))
    return pl.pallas_call(
        functools.partial(_attn_kernel, lambda_init=lambda_init),
        grid=(bsz, n_heads // nh),
        in_specs=[blk, blk,
                  pl.BlockSpec((1, nh * VT_ROWS, seq), lambda b, h: (b, h, 0)),
                  vec, vec, vec, vec,
                  pl.BlockSpec((2 * HEAD_DIM, 1), lambda b, h: (0, 0))],
        out_specs=blk,
        out_shape=jax.ShapeDtypeStruct((bsz, n_heads, seq, LANES), BF16),
        scratch_shapes=[pltpu.VMEM((nh, t, 2 * t), F32),
                        pltpu.VMEM((nh, t, 2 * t), F32),
                        pltpu.VMEM((nh, 1, 2 * t), F32),
                        pltpu.VMEM((nh, 1, 2 * t), F32),
                        pltpu.VMEM((nh, 1, 2 * t), F32),
                        pltpu.VMEM((nh, VT_ROWS, 2 * t), F32)],
        compiler_params=pltpu.CompilerParams(
            dimension_semantics=("arbitrary", "arbitrary"),
            vmem_limit_bytes=ATTN_VMEM_LIMIT),
        name="attn",
    )(q, k, vta, lq1.reshape(1, -1), lk1.reshape(1, -1), lq2.reshape(1, -1),
      lk2.reshape(1, -1), subln_g.reshape(-1, 1))


def _rope_tables(seq):
    inv = 1.0 / (ROPE_THETA ** (np.arange(0, HEAD_DIM, 2, dtype=np.float64) / HEAD_DIM))
    ang = np.arange(seq, dtype=np.float64)[:, None] * inv[None, :]
    ang = np.concatenate([ang, ang, ang, ang], axis=-1)
    cos, sin = np.cos(ang), np.sin(ang)
    upper = (np.arange(LANES) % HEAD_DIM) >= HEAD_DIM // 2
    as_f32 = lambda a: jnp.asarray(a.astype(np.float32))
    return as_f32(cos), as_f32(np.where(upper, sin, 0.0)), as_f32(np.where(upper, 0.0, -sin))


def kernel(x, c, w_ada, b_ada, w_ada_final, b_ada_final, g_ffn1, g_mix, g_ffn2, g_final,
           ffn1_w1, ffn1_w3, ffn1_w2, ffn2_w1, ffn2_w3, ffn2_w2, w_in,
           lambda_q1, lambda_k1, lambda_q2, lambda_k2, subln_g, conv_w, conv_norm_g, w_out):
    bsz, seq, d = x.shape
    depth = w_ada.shape[0]
    cos, sinp, sinm = _rope_tables(seq)
    gsz = CONV_WIDTH // CONV_GROUPS
    grp = np.arange(CONV_WIDTH) // gsz
    gmat = jnp.asarray(np.where(grp[:, None] == grp[None, :], 1.0 / gsz, 0.0), dtype=BF16)
    modf3 = _ada(c, w_ada_final, b_ada_final).reshape(bsz, 2, d)

    x2d = x.reshape(bsz * seq, d)
    for l in range(depth):
        lambda_init = 0.8 - 0.6 * math.exp(-0.3 * l)
        mod3 = _ada(c, w_ada[l], b_ada[l]).reshape(bsz, N_MOD, d)
        bf = lambda w: _to_bf16(w)[0]
        w_in_b = bf(w_in[l])
        w_vt = w_in_b[:, 2 * ATTN_WIDTH:3 * ATTN_WIDTH].T

        x2d = _ffn(x2d, mod3, g_ffn1[l], bf(ffn1_w1[l]), bf(ffn1_w3[l]), bf(ffn1_w2[l]), 0, seq)
        q, k, vt, y = _inproj(x2d, mod3, g_mix[l], w_in_b, w_vt, cos, sinp, sinm,
                              conv_w[l], conv_norm_g[l], gmat, seq)
        a = _attn(q, k, vt, lambda_q1[l], lambda_k1[l], lambda_q2[l], lambda_k2[l],
                  subln_g[l], lambda_init)
        final_args = (modf3, g_final) if l == depth - 1 else None
        mixer_args = (a, y, bf(w_out[l]))
        x2d = _ffn(x2d, mod3, g_ffn2[l], bf(ffn2_w1[l]), bf(ffn2_w3[l]), bf(ffn2_w2[l]), 6, seq,
                   mixer_args, final_args)
    return x2d.reshape(bsz, seq, d)
```

```python
import functools
import math

import jax
import jax.numpy as jnp
import numpy as np
from jax import lax
from jax.experimental import pallas as pl
from jax.experimental.pallas import tpu as pltpu

F32 = jnp.float32
BF16 = jnp.bfloat16

LANES = 128
SUBLANES = 8

D_MODEL = 1024
ATTN_HEADS = 4
HEAD_DIM = 64
ATTN_WIDTH = ATTN_HEADS * 2 * HEAD_DIM
CONV_WIDTH = D_MODEL - ATTN_WIDTH
CONV_GROUPS = 8
CONV_K = 3
ROPE_THETA = 10000.0
NORM_EPS = 1e-6
SUBLN_EPS = 1e-5
N_MOD = 9
IN_COLS = 3 * ATTN_WIDTH + 3 * CONV_WIDTH

FF_CHUNK = 256
ROW_TILE = 1024
ROW_SUBTILE = 512
ATTN_TILE = 512
ONES_ROWS = 16
VT_ROWS = 2 * HEAD_DIM + ONES_ROWS
HEADS_PER_STEP = 4
VMEM_LIMIT = 56 * 1024 * 1024
ATTN_VMEM_LIMIT = 60 * 1024 * 1024
CAST_BLOCKS = 4
ADA_COLS = 1024
NT_DIMS = (((1,), (1,)), ((), ()))
Q_SCALE = math.log2(math.e) / math.sqrt(HEAD_DIM)


def _silu(a):
    return a / (1.0 + jnp.exp(-a))


def _rms(x, eps):
    return x * lax.rsqrt(jnp.mean(x * x, axis=-1, keepdims=True) + eps)


def _ada_kernel(c_ref, w_ref, b_ref, o_ref):
    o_ref[...] = jnp.dot(_silu(c_ref[...]), w_ref[...],
                         preferred_element_type=F32) + b_ref[...]


def _ada(c, w, b):
    bsz, d = c.shape
    n = w.shape[1]
    tn = ADA_COLS
    return pl.pallas_call(
        _ada_kernel,
        grid=(n // tn,),
        in_specs=[pl.BlockSpec((bsz, d), lambda j: (0, 0)),
                  pl.BlockSpec((d, tn), lambda j: (0, j)),
                  pl.BlockSpec((1, tn), lambda j: (0, j))],
        out_specs=pl.BlockSpec((bsz, tn), lambda j: (0, j)),
        out_shape=jax.ShapeDtypeStruct((bsz, n), F32),
        name="ada",
    )(c, w, b.reshape(1, n))


def _cast_kernel(*refs):
    n = len(refs) // 2
    for w_ref, o_ref in zip(refs[:n], refs[n:]):
        o_ref[...] = w_ref[...].astype(o_ref.dtype)


def _to_bf16(*ws):
    rows, cols = ws[0].shape
    assert all(w.shape == (rows, cols) for w in ws)
    rb = rows // CAST_BLOCKS
    spec = pl.BlockSpec((rb, cols), lambda i: (i, 0))
    return pl.pallas_call(
        _cast_kernel,
        grid=(CAST_BLOCKS,),
        in_specs=[spec] * len(ws),
        out_specs=[spec] * len(ws),
        out_shape=[jax.ShapeDtypeStruct((rows, cols), BF16)] * len(ws),
        name="cast",
    )(*ws)


def _ffn_kernel(*refs, mod_row, mixer, final, n_side):
    refs = list(refs)
    x_ref, mod_ref, g_ref = refs[:3]
    del refs[:3]
    if mixer:
        a_ref, y_ref, wo_ref = refs[:3]
        del refs[:3]
    w1_ref, w3_ref, w2_ref = refs[:3]
    del refs[:3]
    if final:
        modf_ref, gf_ref = refs[:2]
        del refs[:2]
    side_in = refs[:n_side]
    del refs[:n_side]
    o_ref = refs[0]
    side_out = refs[1:1 + n_side]
    act_ref = refs[1 + n_side]
    for w_ref, wb_ref in zip(side_in, side_out):
        wb_ref[...] = w_ref[...].astype(wb_ref.dtype)
    shift = mod_ref[0, mod_row:mod_row + 1, :]
    scale = mod_ref[0, mod_row + 1:mod_row + 2, :]
    gate = mod_ref[0, mod_row + 2:mod_row + 3, :]
    n_chunks = w2_ref.shape[0] // FF_CHUNK
    sub_tiles = [slice(r * ROW_SUBTILE, (r + 1) * ROW_SUBTILE)
                 for r in range(x_ref.shape[0] // ROW_SUBTILE)]
    hs = []
    for rows in sub_tiles:
        x = x_ref[rows, :]
        if mixer:
            attn = jnp.concatenate([a_ref[0, hd, rows, :] for hd in range(a_ref.shape[1])],
                                   axis=-1)
            mix = (jnp.dot(attn, wo_ref[:ATTN_WIDTH, :], preferred_element_type=F32)
                   + jnp.dot(y_ref[rows, :], wo_ref[ATTN_WIDTH:, :], preferred_element_type=F32))
            x = x + mod_ref[0, mod_row - 1:mod_row, :] * mix
            o_ref[rows, :] = x
        hs.append((_rms(x, NORM_EPS) * g_ref[...] * (1.0 + scale) + shift).astype(BF16))
    for rows, h in zip(sub_tiles, hs):
        for c in range(n_chunks):
            cols = slice(c * FF_CHUNK, (c + 1) * FF_CHUNK)
            a = jnp.dot(h, w1_ref[:, cols], preferred_element_type=F32)
            b = jnp.dot(h, w3_ref[:, cols], preferred_element_type=F32)
            act_ref[rows, cols] = (_silu(a) * b).astype(BF16)
        x = o_ref[rows, :] if mixer else x_ref[rows, :]
        y = x + 0.5 * gate * jnp.dot(act_ref[rows, :], w2_ref[...],
                                     preferred_element_type=F32)
        if final:
            y = (_rms(y, NORM_EPS) * gf_ref[...] * (1.0 + modf_ref[0, 1:2, :])
                 + modf_ref[0, 0:1, :])
        o_ref[rows, :] = y


def _ffn(x2d, mod3, g, w1, w3, w2, mod_row, rows_per_batch, mixer_args=None, final_args=None,
         side_casts=()):
    m, d = x2d.shape
    tm = ROW_TILE
    tiles_per_batch = rows_per_batch // tm
    n_steps = m // tm
    row = lambda i: (i, 0)
    fix = lambda i: (0, 0)
    per_batch = lambda i: (i // tiles_per_batch, 0, 0)
    resident = lambda w: pl.BlockSpec(w.shape, fix, pipeline_mode=pl.Buffered(1))
    in_specs = [pl.BlockSpec((tm, d), row),
                pl.BlockSpec((1, N_MOD, d), per_batch),
                pl.BlockSpec((1, d), fix)]
    args = [x2d, mod3, g.reshape(1, d)]
    if mixer_args is not None:
        a4d, y2d, w_out = mixer_args
        in_specs += [pl.BlockSpec((1, a4d.shape[1], tm, a4d.shape[3]),
                                  lambda i: (i // tiles_per_batch, 0, i % tiles_per_batch, 0)),
                     pl.BlockSpec((tm, y2d.shape[1]), row), resident(w_out)]
        args += [a4d, y2d, w_out]
    in_specs += [resident(w1), resident(w3), resident(w2)]
    args += [w1, w3, w2]
    if final_args is not None:
        modf3, gf = final_args
        in_specs += [pl.BlockSpec((1, 2, d), per_batch), pl.BlockSpec((1, d), fix)]
        args += [modf3, gf.reshape(1, d)]
    side_specs = [pl.BlockSpec((w.shape[0] // n_steps, w.shape[1]), row) for w in side_casts]
    in_specs += side_specs
    args += list(side_casts)
    return pl.pallas_call(
        functools.partial(_ffn_kernel, mod_row=mod_row, mixer=mixer_args is not None,
                          final=final_args is not None, n_side=len(side_casts)),
        grid=(n_steps,),
        in_specs=in_specs,
        out_specs=[pl.BlockSpec((tm, d), row)] + side_specs,
        out_shape=[jax.ShapeDtypeStruct((m, d), F32)]
        + [jax.ShapeDtypeStruct(w.shape, BF16) for w in side_casts],
        scratch_shapes=[pltpu.VMEM((tm, w2.shape[0]), BF16)],
        compiler_params=pltpu.CompilerParams(
            dimension_semantics=("arbitrary",), vmem_limit_bytes=VMEM_LIMIT),
        name="ffn" + ("_mixer" if mixer_args is not None else "")
        + ("_final" if final_args is not None else ""),
    )(*args)


def _inproj_kernel(x_ref, mod_ref, g_ref, w_ref, wvt_ref, cos_ref, sinp_ref, sinm_ref,
                   convw_ref, cng_ref, gmat_ref,
                   q_ref, k_ref, vt_ref, y_ref, halo_ref, *, mod_row, tiles_per_batch):
    shift = mod_ref[0, mod_row:mod_row + 1, :]
    scale = mod_ref[0, mod_row + 1:mod_row + 2, :]
    half = HEAD_DIM // 2
    w = convw_ref[...]

    @pl.when(pl.program_id(0) % tiles_per_batch == 0)
    def _():
        halo_ref[...] = jnp.zeros_like(halo_ref)

    prev = halo_ref[...]
    for r in range(x_ref.shape[0] // ROW_SUBTILE):
        rows = slice(r * ROW_SUBTILE, (r + 1) * ROW_SUBTILE)
        h = (_rms(x_ref[rows, :], NORM_EPS) * g_ref[...] * (1.0 + scale) + shift).astype(BF16)
        zqk = jnp.dot(h, w_ref[:, :2 * ATTN_WIDTH], preferred_element_type=F32)
        zcv = jnp.dot(h, w_ref[:, 3 * ATTN_WIDTH:], preferred_element_type=F32)
        vt = lax.dot_general(wvt_ref[...], h, NT_DIMS, preferred_element_type=F32).astype(BF16)
        for j in range(ATTN_HEADS):
            vt_ref[0, j * VT_ROWS:j * VT_ROWS + LANES, rows] = vt[j * LANES:(j + 1) * LANES, :]
            vt_ref[0, j * VT_ROWS + LANES:(j + 1) * VT_ROWS, rows] = jnp.ones(
                (ONES_ROWS, ROW_SUBTILE), BF16)
        cos = cos_ref[rows, :]
        sinp = sinp_ref[rows, :]
        sinm = sinm_ref[rows, :]
        for j in range(ATTN_WIDTH // LANES):
            for base, out_ref, mult in ((0, q_ref, Q_SCALE), (ATTN_WIDTH, k_ref, 1.0)):
                t = zqk[:, base + j * LANES: base + (j + 1) * LANES]
                rot = (t * cos + pltpu.roll(t, half, axis=1) * sinp
                       + pltpu.roll(t, LANES - half, axis=1) * sinm)
                out_ref[0, j, rows, :] = (rot * mult).astype(BF16)

        gate_b = zcv[:, :CONV_WIDTH]
        cu = zcv[:, CONV_WIDTH:2 * CONV_WIDTH] * zcv[:, 2 * CONV_WIDTH:]
        buf = jnp.concatenate([prev, cu], axis=0)
        cu_m1 = pltpu.roll(buf, 1, axis=0)[SUBLANES:, :]
        cu_m2 = pltpu.roll(buf, 2, axis=0)[SUBLANES:, :]
        prev = cu[ROW_SUBTILE - SUBLANES:, :]
        y = gate_b * (w[0:1, :] * cu_m2 + w[1:2, :] * cu_m1 + w[2:3, :] * cu)
        y2 = y * y
        y2_hi = y2.astype(BF16)
        y2_lo = (y2 - y2_hi.astype(F32)).astype(BF16)
        ms = (jnp.dot(y2_hi, gmat_ref[...], preferred_element_type=F32)
              + jnp.dot(y2_lo, gmat_ref[...], preferred_element_type=F32))
        y_ref[rows, :] = (y * lax.rsqrt(ms + NORM_EPS) * cng_ref[...]).astype(BF16)
    halo_ref[...] = prev


def _inproj(x2d, mod3, g, w_in, w_vt, cos, sinp, sinm, conv_w, conv_norm_g, gmat,
            rows_per_batch):
    m, d = x2d.shape
    tm = ROW_TILE
    tiles_per_batch = rows_per_batch // tm
    row = lambda i: (i, 0)
    fix = lambda i: (0, 0)
    pos = lambda i: (i % tiles_per_batch, 0)
    n_batch = m // rows_per_batch
    head_major = pl.BlockSpec((1, ATTN_HEADS, tm, LANES),
                              lambda i: (i // tiles_per_batch, 0, i % tiles_per_batch, 0))
    out_sd = jax.ShapeDtypeStruct((n_batch, ATTN_HEADS, rows_per_batch, LANES), BF16)
    return pl.pallas_call(
        functools.partial(_inproj_kernel, mod_row=3, tiles_per_batch=tiles_per_batch),
        grid=(m // tm,),
        in_specs=[pl.BlockSpec((tm, d), row),
                  pl.BlockSpec((1, N_MOD, d), lambda i: (i // tiles_per_batch, 0, 0)),
                  pl.BlockSpec((1, d), fix),
                  pl.BlockSpec(w_in.shape, fix, pipeline_mode=pl.Buffered(1)),
                  pl.BlockSpec(w_vt.shape, fix, pipeline_mode=pl.Buffered(1)),
                  pl.BlockSpec((tm, LANES), pos),
                  pl.BlockSpec((tm, LANES), pos),
                  pl.BlockSpec((tm, LANES), pos),
                  pl.BlockSpec((CONV_K, CONV_WIDTH), fix),
                  pl.BlockSpec((1, CONV_WIDTH), fix),
                  pl.BlockSpec((CONV_WIDTH, CONV_WIDTH), fix)],
        out_specs=[head_major, head_major,
                   pl.BlockSpec((1, ATTN_HEADS * VT_ROWS, tm),
                                lambda i: (i // tiles_per_batch, 0, i % tiles_per_batch)),
                   pl.BlockSpec((tm, CONV_WIDTH), row)],
        out_shape=[out_sd, out_sd,
                   jax.ShapeDtypeStruct((n_batch, ATTN_HEADS * VT_ROWS, rows_per_batch), BF16),
                   jax.ShapeDtypeStruct((m, CONV_WIDTH), BF16)],
        scratch_shapes=[pltpu.VMEM((SUBLANES, CONV_WIDTH), F32)],
        compiler_params=pltpu.CompilerParams(
            dimension_semantics=("arbitrary",), vmem_limit_bytes=VMEM_LIMIT),
        name="inproj",
    )(x2d, mod3, g.reshape(1, d), w_in, w_vt, cos, sinp, sinm, conv_w,
      conv_norm_g.reshape(1, CONV_WIDTH), gmat)


def _attn_kernel(q_ref, k_ref, vta_ref, lq1_ref, lk1_ref, lq2_ref, lk2_ref, sg_ref,
                 o_ref, sa_ref, sb_ref, cma_ref, cmb_ref, m_ref, acc_ref, *, lambda_init):
    seq = q_ref.shape[2]
    t = ATTN_TILE
    dv = 2 * HEAD_DIM
    n_tiles = seq // t
    heads = range(HEADS_PER_STEP)
    lam = (jnp.exp(jnp.sum(lq1_ref[...] * lk1_ref[...], axis=-1, keepdims=True))
           - jnp.exp(jnp.sum(lq2_ref[...] * lk2_ref[...], axis=-1, keepdims=True))
           + lambda_init)

    def stacked_queries(hd, i):
        q = q_ref[0, hd, pl.ds(pl.multiple_of(i * t, t), t), :]
        first_map = lax.broadcasted_iota(jnp.int32, (t, LANES), 1) < HEAD_DIM
        zero = jnp.zeros_like(q)
        return jnp.concatenate([jnp.where(first_map, q, zero),
                                jnp.where(first_map, zero, q)], axis=0)

    def scores_into(s_buf, cm_buf, qq, k0):
        for hd in heads:
            s = lax.dot_general(k_ref[0, hd, pl.ds(k0, t), :], qq[hd], NT_DIMS,
                                preferred_element_type=F32)
            s_buf[hd] = s
            cm_buf[hd] = jnp.max(s, axis=0, keepdims=True)

    def absorb(s_buf, cm_buf, k0, mask=False):
        for hd in heads:
            s = s_buf[hd]
            if mask:
                kpos = lax.broadcasted_iota(jnp.int32, (t, 2 * t), 0)
                qpos = lax.broadcasted_iota(jnp.int32, (t, 2 * t), 1)
                qpos = jnp.where(qpos >= t, qpos - t, qpos)
                s = jnp.where(kpos <= qpos, s, -jnp.inf)
                cm = jnp.max(s, axis=0, keepdims=True)
            else:
                cm = cm_buf[hd]
            m = m_ref[hd]
            m_new = jnp.maximum(m, cm)
            alpha = jnp.exp2(m - m_new)
            p = jnp.exp2(s - m_new).astype(BF16)
            m_ref[hd] = m_new
            pv = jnp.dot(vta_ref[0, hd * VT_ROWS:(hd + 1) * VT_ROWS, pl.ds(k0, t)], p,
                         preferred_element_type=F32)
            acc_ref[hd] = alpha * acc_ref[hd] + pv

    def q_tile(i, odd):
        q0 = pl.multiple_of(i * t, t)
        nxt = jnp.minimum(i + 1, n_tiles - 1)
        qq = [stacked_queries(hd, i) for hd in heads]
        qq_next = [stacked_queries(hd, nxt) for hd in heads]
        m_ref[...] = jnp.full(m_ref.shape, -jnp.inf, F32)
        acc_ref[...] = jnp.zeros_like(acc_ref)
        if not odd:
            scores_into(sb_ref, cmb_ref, [jnp.where(i == 0, b, a) for a, b in zip(qq, qq_next)], 0)
        absorb(sa_ref, cma_ref, q0, mask=True)

        def pair(jj, carry):
            k_even = pl.multiple_of(2 * jj * t, t)
            k_odd = pl.multiple_of(k_even + t, t)
            scores_into(sa_ref, cma_ref, qq, k_odd)
            absorb(sb_ref, cmb_ref, k_even)
            if odd:
                scores_into(sb_ref, cmb_ref, qq, pl.multiple_of(k_odd + t, t))
            else:
                done = 2 * jj + 2 >= i
                scores_into(sb_ref, cmb_ref,
                            [jnp.where(done, b, a) for a, b in zip(qq, qq_next)],
                            pl.multiple_of(jnp.where(done, 0, 2 * jj + 2) * t, t))
            absorb(sa_ref, cma_ref, k_odd)
            return carry

        lax.fori_loop(0, i // 2, pair, 0)
        scores_into(sa_ref, cma_ref, qq_next, pl.multiple_of(nxt * t, t))
        if odd:
            absorb(sb_ref, cmb_ref, pl.multiple_of((i - 1) * t, t))
        for hd in heads:
            acc = acc_ref[hd]
            o = acc[:dv, :] / acc[dv:dv + 1, :]
            o = o[:, :t] - lam * o[:, t:]
            o = o * lax.rsqrt(jnp.mean(o * o, axis=0, keepdims=True) + SUBLN_EPS)
            o = o * (sg_ref[...] * (1.0 - lambda_init))
            o_ref[0, hd, pl.ds(q0, t), :] = o.T.astype(BF16)

    def q_tile_pair(a, carry):
        q_tile(2 * a, odd=False)
        q_tile(2 * a + 1, odd=True)
        return carry

    scores_into(sa_ref, cma_ref, [stacked_queries(hd, 0) for hd in heads], 0)
    lax.fori_loop(0, n_tiles // 2, q_tile_pair, 0)


def _attn(q, k, vta, lq1, lk1, lq2, lk2, subln_g, lambda_init):
    bsz, n_heads, seq, _ = q.shape
    t = ATTN_TILE
    nh = HEADS_PER_STEP
    assert (seq // t) % 2 == 0 and n_heads % nh == 0 and vta.shape[1] == n_heads * VT_ROWS
    blk = pl.BlockSpec((1, nh, seq, LANES), lambda b, h: (b, h, 0, 0))
    vec = pl.BlockSpec((1, HEAD_DIM), lambda b, h: (0, 0))
    return pl.pallas_call(
        functools.partial(_attn_kernel, lambda_init=lambda_init),
        grid=(bsz, n_heads // nh),
        in_specs=[blk, blk,
                  pl.BlockSpec((1, nh * VT_ROWS, seq), lambda b, h: (b, h, 0)),
                  vec, vec, vec, vec,
                  pl.BlockSpec((2 * HEAD_DIM, 1), lambda b, h: (0, 0))],
        out_specs=blk,
        out_shape=jax.ShapeDtypeStruct((bsz, n_heads, seq, LANES), BF16),
        scratch_shapes=[pltpu.VMEM((nh, t, 2 * t), F32),
                        pltpu.VMEM((nh, t, 2 * t), F32),
                        pltpu.VMEM((nh, 1, 2 * t), F32),
                        pltpu.VMEM((nh, 1, 2 * t), F32),
                        pltpu.VMEM((nh, 1, 2 * t), F32),
                        pltpu.VMEM((nh, VT_ROWS, 2 * t), F32)],
        compiler_params=pltpu.CompilerParams(
            dimension_semantics=("arbitrary", "arbitrary"),
            vmem_limit_bytes=ATTN_VMEM_LIMIT),
        name="attn",
    )(q, k, vta, lq1.reshape(1, -1), lk1.reshape(1, -1), lq2.reshape(1, -1),
      lk2.reshape(1, -1), subln_g.reshape(-1, 1))


def _rope_tables(seq):
    inv = 1.0 / (ROPE_THETA ** (np.arange(0, HEAD_DIM, 2, dtype=np.float64) / HEAD_DIM))
    ang = np.arange(seq, dtype=np.float64)[:, None] * inv[None, :]
    ang = np.concatenate([ang, ang, ang, ang], axis=-1)
    cos, sin = np.cos(ang), np.sin(ang)
    upper = (np.arange(LANES) % HEAD_DIM) >= HEAD_DIM // 2
    as_f32 = lambda a: jnp.asarray(a.astype(np.float32))
    return as_f32(cos), as_f32(np.where(upper, sin, 0.0)), as_f32(np.where(upper, 0.0, -sin))


def kernel(x, c, w_ada, b_ada, w_ada_final, b_ada_final, g_ffn1, g_mix, g_ffn2, g_final,
           ffn1_w1, ffn1_w3, ffn1_w2, ffn2_w1, ffn2_w3, ffn2_w2, w_in,
           lambda_q1, lambda_k1, lambda_q2, lambda_k2, subln_g, conv_w, conv_norm_g, w_out):
    bsz, seq, d = x.shape
    depth = w_ada.shape[0]
    cos, sinp, sinm = _rope_tables(seq)
    gsz = CONV_WIDTH // CONV_GROUPS
    grp = np.arange(CONV_WIDTH) // gsz
    gmat = jnp.asarray(np.where(grp[:, None] == grp[None, :], 1.0 / gsz, 0.0), dtype=BF16)
    modf3 = _ada(c, w_ada_final, b_ada_final).reshape(bsz, 2, d)

    x2d = x.reshape(bsz * seq, d)
    for l in range(depth):
        lambda_init = 0.8 - 0.6 * math.exp(-0.3 * l)
        mod3 = _ada(c, w_ada[l], b_ada[l]).reshape(bsz, N_MOD, d)
        bf = lambda w: _to_bf16(w)[0]
        x2d, w_in_b, w_out_b, w1b, w3b = _ffn(
            x2d, mod3, g_ffn1[l], bf(ffn1_w1[l]), bf(ffn1_w3[l]), bf(ffn1_w2[l]), 0, seq,
            side_casts=(w_in[l], w_out[l], ffn2_w1[l], ffn2_w3[l]))
        w_vt = w_in_b[:, 2 * ATTN_WIDTH:3 * ATTN_WIDTH].T
        q, k, vt, y = _inproj(x2d, mod3, g_mix[l], w_in_b, w_vt, cos, sinp, sinm,
                              conv_w[l], conv_norm_g[l], gmat, seq)
        a = _attn(q, k, vt, lambda_q1[l], lambda_k1[l], lambda_q2[l], lambda_k2[l],
                  subln_g[l], lambda_init)
        final_args = (modf3, g_final) if l == depth - 1 else None
        x2d, = _ffn(x2d, mod3, g_ffn2[l], w1b, w3b, bf(ffn2_w2[l]), 6, seq,
                    (a, y, w_out_b), final_args)
    return x2d.reshape(bsz, seq, d)
```

```python
import functools
import math

import jax
import jax.numpy as jnp
import numpy as np
from jax import lax
from jax.experimental import pallas as pl
from jax.experimental.pallas import tpu as pltpu

F32 = jnp.float32
BF16 = jnp.bfloat16

LANES = 128
SUBLANES = 8
BF16_ROWS = 16

D_MODEL = 1024
ATTN_HEADS = 4
HEAD_DIM = 64
ATTN_WIDTH = ATTN_HEADS * 2 * HEAD_DIM
CONV_WIDTH = D_MODEL - ATTN_WIDTH
CONV_GROUPS = 8
CONV_K = 3
ROPE_THETA = 10000.0
NORM_EPS = 1e-6
SUBLN_EPS = 1e-5
N_MOD = 9
IN_COLS = 3 * ATTN_WIDTH + 3 * CONV_WIDTH

FF_CHUNK = 256
ROW_TILE = 1024
ROW_SUBTILE = 512
ATTN_TILE = 512
ONES_ROWS = 16
VT_ROWS = 2 * HEAD_DIM + ONES_ROWS
HEADS_PER_STEP = 4
VMEM_LIMIT = 56 * 1024 * 1024
ATTN_VMEM_LIMIT = 60 * 1024 * 1024
CAST_BLOCKS = 4
ADA_COLS = 1024
NT_DIMS = (((1,), (1,)), ((), ()))
Q_SCALE = math.log2(math.e) / math.sqrt(HEAD_DIM)


def _silu(a):
    return a / (1.0 + jnp.exp(-a))


def _rms(x, eps):
    return x * lax.rsqrt(jnp.mean(x * x, axis=-1, keepdims=True) + eps)


def _ada_kernel(c_ref, w_ref, b_ref, o_ref):
    o_ref[...] = jnp.dot(_silu(c_ref[...]), w_ref[...],
                         preferred_element_type=F32) + b_ref[...]


def _ada(c, w, b):
    bsz, d = c.shape
    n = w.shape[1]
    tn = ADA_COLS
    return pl.pallas_call(
        _ada_kernel,
        grid=(n // tn,),
        in_specs=[pl.BlockSpec((bsz, d), lambda j: (0, 0)),
                  pl.BlockSpec((d, tn), lambda j: (0, j)),
                  pl.BlockSpec((1, tn), lambda j: (0, j))],
        out_specs=pl.BlockSpec((bsz, tn), lambda j: (0, j)),
        out_shape=jax.ShapeDtypeStruct((bsz, n), F32),
        name="ada",
    )(c, w, b.reshape(1, n))


def _cast_kernel(*refs):
    n = len(refs) // 2
    for w_ref, o_ref in zip(refs[:n], refs[n:]):
        o_ref[...] = w_ref[...].astype(o_ref.dtype)


def _to_bf16(*ws):
    rows, cols = ws[0].shape
    assert all(w.shape == (rows, cols) for w in ws)
    rb = rows // CAST_BLOCKS
    spec = pl.BlockSpec((rb, cols), lambda i: (i, 0))
    return pl.pallas_call(
        _cast_kernel,
        grid=(CAST_BLOCKS,),
        in_specs=[spec] * len(ws),
        out_specs=[spec] * len(ws),
        out_shape=[jax.ShapeDtypeStruct((rows, cols), BF16)] * len(ws),
        name="cast",
    )(*ws)


def _ffn_kernel(*refs, mod_row, mixer, final, n_side):
    refs = list(refs)
    x_ref, mod_ref, g_ref = refs[:3]
    del refs[:3]
    if mixer:
        a_ref, y_ref, wo_ref = refs[:3]
        del refs[:3]
    w1_ref, w3_ref, w2_ref = refs[:3]
    del refs[:3]
    if final:
        modf_ref, gf_ref = refs[:2]
        del refs[:2]
    side_in = refs[:n_side]
    del refs[:n_side]
    o_ref = refs[0]
    side_out = refs[1:1 + n_side]
    act_ref = refs[1 + n_side]
    for w_ref, wb_ref in zip(side_in, side_out):
        wb_ref[...] = w_ref[...].astype(wb_ref.dtype)
    shift = mod_ref[0, mod_row:mod_row + 1, :]
    scale = mod_ref[0, mod_row + 1:mod_row + 2, :]
    gate = mod_ref[0, mod_row + 2:mod_row + 3, :]
    n_chunks = w2_ref.shape[0] // FF_CHUNK
    sub_tiles = [slice(r * ROW_SUBTILE, (r + 1) * ROW_SUBTILE)
                 for r in range(x_ref.shape[0] // ROW_SUBTILE)]
    hs = []
    for rows in sub_tiles:
        x = x_ref[rows, :]
        if mixer:
            attn = jnp.concatenate([a_ref[0, hd, rows, :] for hd in range(a_ref.shape[1])],
                                   axis=-1)
            mix = (jnp.dot(attn, wo_ref[:ATTN_WIDTH, :], preferred_element_type=F32)
                   + jnp.dot(y_ref[rows, :], wo_ref[ATTN_WIDTH:, :], preferred_element_type=F32))
            x = x + mod_ref[0, mod_row - 1:mod_row, :] * mix
            o_ref[rows, :] = x
        hs.append((_rms(x, NORM_EPS) * g_ref[...] * (1.0 + scale) + shift).astype(BF16))
    for rows, h in zip(sub_tiles, hs):
        for c in range(n_chunks):
            cols = slice(c * FF_CHUNK, (c + 1) * FF_CHUNK)
            a = jnp.dot(h, w1_ref[:, cols], preferred_element_type=F32)
            b = jnp.dot(h, w3_ref[:, cols], preferred_element_type=F32)
            act_ref[rows, cols] = (_silu(a) * b).astype(BF16)
        x = o_ref[rows, :] if mixer else x_ref[rows, :]
        y = x + 0.5 * gate * jnp.dot(act_ref[rows, :], w2_ref[...],
                                     preferred_element_type=F32)
        if final:
            y = (_rms(y, NORM_EPS) * gf_ref[...] * (1.0 + modf_ref[0, 1:2, :])
                 + modf_ref[0, 0:1, :])
        o_ref[rows, :] = y


def _ffn(x2d, mod3, g, w1, w3, w2, mod_row, rows_per_batch, mixer_args=None, final_args=None,
         side_casts=()):
    m, d = x2d.shape
    tm = ROW_TILE
    tiles_per_batch = rows_per_batch // tm
    n_steps = m // tm
    row = lambda i: (i, 0)
    fix = lambda i: (0, 0)
    per_batch = lambda i: (i // tiles_per_batch, 0, 0)
    resident = lambda w: pl.BlockSpec(w.shape, fix, pipeline_mode=pl.Buffered(1))
    in_specs = [pl.BlockSpec((tm, d), row),
                pl.BlockSpec((1, N_MOD, d), per_batch),
                pl.BlockSpec((1, d), fix)]
    args = [x2d, mod3, g.reshape(1, d)]
    if mixer_args is not None:
        a4d, y2d, w_out = mixer_args
        in_specs += [pl.BlockSpec((1, a4d.shape[1], tm, a4d.shape[3]),
                                  lambda i: (i // tiles_per_batch, 0, i % tiles_per_batch, 0)),
                     pl.BlockSpec((tm, y2d.shape[1]), row), resident(w_out)]
        args += [a4d, y2d, w_out]
    in_specs += [resident(w1), resident(w3), resident(w2)]
    args += [w1, w3, w2]
    if final_args is not None:
        modf3, gf = final_args
        in_specs += [pl.BlockSpec((1, 2, d), per_batch), pl.BlockSpec((1, d), fix)]
        args += [modf3, gf.reshape(1, d)]
    side_specs = []
    for w in side_casts:
        reps = 1
        while (w.shape[0] * reps // n_steps) % BF16_ROWS:
            reps *= 2
        side_specs.append(pl.BlockSpec((w.shape[0] * reps // n_steps, w.shape[1]),
                                       functools.partial(lambda i, r: (i // r, 0), r=reps)))
    in_specs += side_specs
    args += list(side_casts)
    return pl.pallas_call(
        functools.partial(_ffn_kernel, mod_row=mod_row, mixer=mixer_args is not None,
                          final=final_args is not None, n_side=len(side_casts)),
        grid=(n_steps,),
        in_specs=in_specs,
        out_specs=[pl.BlockSpec((tm, d), row)] + side_specs,
        out_shape=[jax.ShapeDtypeStruct((m, d), F32)]
        + [jax.ShapeDtypeStruct(w.shape, BF16) for w in side_casts],
        scratch_shapes=[pltpu.VMEM((tm, w2.shape[0]), BF16)],
        compiler_params=pltpu.CompilerParams(
            dimension_semantics=("arbitrary",), vmem_limit_bytes=VMEM_LIMIT),
        name="ffn" + ("_mixer" if mixer_args is not None else "")
        + ("_final" if final_args is not None else ""),
    )(*args)


def _inproj_kernel(x_ref, mod_ref, g_ref, w_ref, wvt_ref, cos_ref, sinp_ref, sinm_ref,
                   convw_ref, cng_ref, gmat_ref,
                   q_ref, k_ref, vt_ref, y_ref, halo_ref, *, mod_row, tiles_per_batch):
    shift = mod_ref[0, mod_row:mod_row + 1, :]
    scale = mod_ref[0, mod_row + 1:mod_row + 2, :]
    half = HEAD_DIM // 2
    w = convw_ref[...]

    @pl.when(pl.program_id(0) % tiles_per_batch == 0)
    def _():
        halo_ref[...] = jnp.zeros_like(halo_ref)

    prev = halo_ref[...]
    for r in range(x_ref.shape[0] // ROW_SUBTILE):
        rows = slice(r * ROW_SUBTILE, (r + 1) * ROW_SUBTILE)
        h = (_rms(x_ref[rows, :], NORM_EPS) * g_ref[...] * (1.0 + scale) + shift).astype(BF16)
        zqk = jnp.dot(h, w_ref[:, :2 * ATTN_WIDTH], preferred_element_type=F32)
        zcv = jnp.dot(h, w_ref[:, 3 * ATTN_WIDTH:], preferred_element_type=F32)
        vt = lax.dot_general(wvt_ref[...], h, NT_DIMS, preferred_element_type=F32).astype(BF16)
        for j in range(ATTN_HEADS):
            vt_ref[0, j * VT_ROWS:j * VT_ROWS + LANES, rows] = vt[j * LANES:(j + 1) * LANES, :]
            vt_ref[0, j * VT_ROWS + LANES:(j + 1) * VT_ROWS, rows] = jnp.ones(
                (ONES_ROWS, ROW_SUBTILE), BF16)
        cos = cos_ref[rows, :]
        sinp = sinp_ref[rows, :]
        sinm = sinm_ref[rows, :]
        for j in range(ATTN_WIDTH // LANES):
            for base, out_ref, mult in ((0, q_ref, Q_SCALE), (ATTN_WIDTH, k_ref, 1.0)):
                t = zqk[:, base + j * LANES: base + (j + 1) * LANES]
                rot = (t * cos + pltpu.roll(t, half, axis=1) * sinp
                       + pltpu.roll(t, LANES - half, axis=1) * sinm)
                out_ref[0, j, rows, :] = (rot * mult).astype(BF16)

        gate_b = zcv[:, :CONV_WIDTH]
        cu = zcv[:, CONV_WIDTH:2 * CONV_WIDTH] * zcv[:, 2 * CONV_WIDTH:]
        buf = jnp.concatenate([prev, cu], axis=0)
        cu_m1 = pltpu.roll(buf, 1, axis=0)[SUBLANES:, :]
        cu_m2 = pltpu.roll(buf, 2, axis=0)[SUBLANES:, :]
        prev = cu[ROW_SUBTILE - SUBLANES:, :]
        y = gate_b * (w[0:1, :] * cu_m2 + w[1:2, :] * cu_m1 + w[2:3, :] * cu)
        y2 = y * y
        y2_hi = y2.astype(BF16)
        y2_lo = (y2 - y2_hi.astype(F32)).astype(BF16)
        ms = (jnp.dot(y2_hi, gmat_ref[...], preferred_element_type=F32)
              + jnp.dot(y2_lo, gmat_ref[...], preferred_element_type=F32))
        y_ref[rows, :] = (y * lax.rsqrt(ms + NORM_EPS) * cng_ref[...]).astype(BF16)
    halo_ref[...] = prev


def _inproj(x2d, mod3, g, w_in, w_vt, cos, sinp, sinm, conv_w, conv_norm_g, gmat,
            rows_per_batch):
    m, d = x2d.shape
    tm = ROW_TILE
    tiles_per_batch = rows_per_batch // tm
    row = lambda i: (i, 0)
    fix = lambda i: (0, 0)
    pos = lambda i: (i % tiles_per_batch, 0)
    n_batch = m // rows_per_batch
    head_major = pl.BlockSpec((1, ATTN_HEADS, tm, LANES),
                              lambda i: (i // tiles_per_batch, 0, i % tiles_per_batch, 0))
    out_sd = jax.ShapeDtypeStruct((n_batch, ATTN_HEADS, rows_per_batch, LANES), BF16)
    return pl.pallas_call(
        functools.partial(_inproj_kernel, mod_row=3, tiles_per_batch=tiles_per_batch),
        grid=(m // tm,),
        in_specs=[pl.BlockSpec((tm, d), row),
                  pl.BlockSpec((1, N_MOD, d), lambda i: (i // tiles_per_batch, 0, 0)),
                  pl.BlockSpec((1, d), fix),
                  pl.BlockSpec(w_in.shape, fix, pipeline_mode=pl.Buffered(1)),
                  pl.BlockSpec(w_vt.shape, fix, pipeline_mode=pl.Buffered(1)),
                  pl.BlockSpec((tm, LANES), pos),
                  pl.BlockSpec((tm, LANES), pos),
                  pl.BlockSpec((tm, LANES), pos),
                  pl.BlockSpec((CONV_K, CONV_WIDTH), fix),
                  pl.BlockSpec((1, CONV_WIDTH), fix),
                  pl.BlockSpec((CONV_WIDTH, CONV_WIDTH), fix)],
        out_specs=[head_major, head_major,
                   pl.BlockSpec((1, ATTN_HEADS * VT_ROWS, tm),
                                lambda i: (i // tiles_per_batch, 0, i % tiles_per_batch)),
                   pl.BlockSpec((tm, CONV_WIDTH), row)],
        out_shape=[out_sd, out_sd,
                   jax.ShapeDtypeStruct((n_batch, ATTN_HEADS * VT_ROWS, rows_per_batch), BF16),
                   jax.ShapeDtypeStruct((m, CONV_WIDTH), BF16)],
        scratch_shapes=[pltpu.VMEM((SUBLANES, CONV_WIDTH), F32)],
        compiler_params=pltpu.CompilerParams(
            dimension_semantics=("arbitrary",), vmem_limit_bytes=VMEM_LIMIT),
        name="inproj",
    )(x2d, mod3, g.reshape(1, d), w_in, w_vt, cos, sinp, sinm, conv_w,
      conv_norm_g.reshape(1, CONV_WIDTH), gmat)


def _attn_kernel(q_ref, k_ref, vta_ref, lq1_ref, lk1_ref, lq2_ref, lk2_ref, sg_ref,
                 o_ref, sa_ref, sb_ref, cma_ref, cmb_ref, m_ref, acc_ref, *, lambda_init):
    seq = q_ref.shape[2]
    t = ATTN_TILE
    dv = 2 * HEAD_DIM
    n_tiles = seq // t
    heads = range(HEADS_PER_STEP)
    lam = (jnp.exp(jnp.sum(lq1_ref[...] * lk1_ref[...], axis=-1, keepdims=True))
           - jnp.exp(jnp.sum(lq2_ref[...] * lk2_ref[...], axis=-1, keepdims=True))
           + lambda_init)

    def stacked_queries(hd, i):
        q = q_ref[0, hd, pl.ds(pl.multiple_of(i * t, t), t), :]
        first_map = lax.broadcasted_iota(jnp.int32, (t, LANES), 1) < HEAD_DIM
        zero = jnp.zeros_like(q)
        return jnp.concatenate([jnp.where(first_map, q, zero),
                                jnp.where(first_map, zero, q)], axis=0)

    def scores_into(s_buf, cm_buf, qq, k0):
        for hd in heads:
            s = lax.dot_general(k_ref[0, hd, pl.ds(k0, t), :], qq[hd], NT_DIMS,
                                preferred_element_type=F32)
            s_buf[hd] = s
            cm_buf[hd] = jnp.max(s, axis=0, keepdims=True)

    def absorb(s_buf, cm_buf, k0, mask=False):
        for hd in heads:
            s = s_buf[hd]
            if mask:
                kpos = lax.broadcasted_iota(jnp.int32, (t, 2 * t), 0)
                qpos = lax.broadcasted_iota(jnp.int32, (t, 2 * t), 1)
                qpos = jnp.where(qpos >= t, qpos - t, qpos)
                s = jnp.where(kpos <= qpos, s, -jnp.inf)
                cm = jnp.max(s, axis=0, keepdims=True)
            else:
                cm = cm_buf[hd]
            m = m_ref[hd]
            m_new = jnp.maximum(m, cm)
            alpha = jnp.exp2(m - m_new)
            p = jnp.exp2(s - m_new).astype(BF16)
            m_ref[hd] = m_new
            pv = jnp.dot(vta_ref[0, hd * VT_ROWS:(hd + 1) * VT_ROWS, pl.ds(k0, t)], p,
                         preferred_element_type=F32)
            acc_ref[hd] = alpha * acc_ref[hd] + pv

    def q_tile(i, odd):
        q0 = pl.multiple_of(i * t, t)
        nxt = jnp.minimum(i + 1, n_tiles - 1)
        qq = [stacked_queries(hd, i) for hd in heads]
        qq_next = [stacked_queries(hd, nxt) for hd in heads]
        m_ref[...] = jnp.full(m_ref.shape, -jnp.inf, F32)
        acc_ref[...] = jnp.zeros_like(acc_ref)
        if not odd:
            scores_into(sb_ref, cmb_ref, [jnp.where(i == 0, b, a) for a, b in zip(qq, qq_next)], 0)
        absorb(sa_ref, cma_ref, q0, mask=True)

        def pair(jj, carry):
            k_even = pl.multiple_of(2 * jj * t, t)
            k_odd = pl.multiple_of(k_even + t, t)
            scores_into(sa_ref, cma_ref, qq, k_odd)
            absorb(sb_ref, cmb_ref, k_even)
            if odd:
                scores_into(sb_ref, cmb_ref, qq, pl.multiple_of(k_odd + t, t))
            else:
                done = 2 * jj + 2 >= i
                scores_into(sb_ref, cmb_ref,
                            [jnp.where(done, b, a) for a, b in zip(qq, qq_next)],
                            pl.multiple_of(jnp.where(done, 0, 2 * jj + 2) * t, t))
            absorb(sa_ref, cma_ref, k_odd)
            return carry

        lax.fori_loop(0, i // 2, pair, 0)
        scores_into(sa_ref, cma_ref, qq_next, pl.multiple_of(nxt * t, t))
        if odd:
            absorb(sb_ref, cmb_ref, pl.multiple_of((i - 1) * t, t))
        for hd in heads:
            acc = acc_ref[hd]
            o = acc[:dv, :] / acc[dv:dv + 1, :]
            o = o[:, :t] - lam * o[:, t:]
            o = o * lax.rsqrt(jnp.mean(o * o, axis=0, keepdims=True) + SUBLN_EPS)
            o = o * (sg_ref[...] * (1.0 - lambda_init))
            o_ref[0, hd, pl.ds(q0, t), :] = o.T.astype(BF16)

    def q_tile_pair(a, carry):
        q_tile(2 * a, odd=False)
        q_tile(2 * a + 1, odd=True)
        return carry

    scores_into(sa_ref, cma_ref, [stacked_queries(hd, 0) for hd in heads], 0)
    lax.fori_loop(0, n_tiles // 2, q_tile_pair, 0)


def _attn(q, k, vta, lq1, lk1, lq2, lk2, subln_g, lambda_init):
    bsz, n_heads, seq, _ = q.shape
    t = ATTN_TILE
    nh = HEADS_PER_STEP
    assert (seq // t) % 2 == 0 and n_heads % nh == 0 and vta.shape[1] == n_heads * VT_ROWS
    blk = pl.BlockSpec((1, nh, seq, LANES), lambda b, h: (b, h, 0, 0))
    vec = pl.BlockSpec((1, HEAD_DIM), lambda b, h: (0, 0))
    return pl.pallas_call(
        functools.partial(_attn_kernel, lambda_init=lambda_init),
        grid=(bsz, n_heads // nh),
        in_specs=[blk, blk,
                  pl.BlockSpec((1, nh * VT_ROWS, seq), lambda b, h: (b, h, 0)),
                  vec, vec, vec, vec,
                  pl.BlockSpec((2 * HEAD_DIM, 1), lambda b, h: (0, 0))],
        out_specs=blk,
        out_shape=jax.ShapeDtypeStruct((bsz, n_heads, seq, LANES), BF16),
        scratch_shapes=[pltpu.VMEM((nh, t, 2 * t), F32),
                        pltpu.VMEM((nh, t, 2 * t), F32),
                        pltpu.VMEM((nh, 1, 2 * t), F32),
                        pltpu.VMEM((nh, 1, 2 * t), F32),
                        pltpu.VMEM((nh, 1, 2 * t), F32),
                        pltpu.VMEM((nh, VT_ROWS, 2 * t), F32)],
        compiler_params=pltpu.CompilerParams(
            dimension_semantics=("arbitrary", "arbitrary"),
            vmem_limit_bytes=ATTN_VMEM_LIMIT),
        name="attn",
    )(q, k, vta, lq1.reshape(1, -1), lk1.reshape(1, -1), lq2.reshape(1, -1),
      lk2.reshape(1, -1), subln_g.reshape(-1, 1))


def _rope_tables(seq):
    inv = 1.0 / (ROPE_THETA ** (np.arange(0, HEAD_DIM, 2, dtype=np.float64) / HEAD_DIM))
    ang = np.arange(seq, dtype=np.float64)[:, None] * inv[None, :]
    ang = np.concatenate([ang, ang, ang, ang], axis=-1)
    cos, sin = np.cos(ang), np.sin(ang)
    upper = (np.arange(LANES) % HEAD_DIM) >= HEAD_DIM // 2
    as_f32 = lambda a: jnp.asarray(a.astype(np.float32))
    return as_f32(cos), as_f32(np.where(upper, sin, 0.0)), as_f32(np.where(upper, 0.0, -sin))


def kernel(x, c, w_ada, b_ada, w_ada_final, b_ada_final, g_ffn1, g_mix, g_ffn2, g_final,
           ffn1_w1, ffn1_w3, ffn1_w2, ffn2_w1, ffn2_w3, ffn2_w2, w_in,
           lambda_q1, lambda_k1, lambda_q2, lambda_k2, subln_g, conv_w, conv_norm_g, w_out):
    bsz, seq, d = x.shape
    depth = w_ada.shape[0]
    cos, sinp, sinm = _rope_tables(seq)
    gsz = CONV_WIDTH // CONV_GROUPS
    grp = np.arange(CONV_WIDTH) // gsz
    gmat = jnp.asarray(np.where(grp[:, None] == grp[None, :], 1.0 / gsz, 0.0), dtype=BF16)
    modf3 = _ada(c, w_ada_final, b_ada_final).reshape(bsz, 2, d)

    x2d = x.reshape(bsz * seq, d)
    for l in range(depth):
        lambda_init = 0.8 - 0.6 * math.exp(-0.3 * l)
        mod3 = _ada(c, w_ada[l], b_ada[l]).reshape(bsz, N_MOD, d)
        bf = lambda w: _to_bf16(w)[0]
        x2d, w_in_b, w_out_b, w1b, w3b, w2b = _ffn(
            x2d, mod3, g_ffn1[l], bf(ffn1_w1[l]), bf(ffn1_w3[l]), bf(ffn1_w2[l]), 0, seq,
            side_casts=(w_in[l], w_out[l], ffn2_w1[l], ffn2_w3[l], ffn2_w2[l]))
        w_vt = w_in_b[:, 2 * ATTN_WIDTH:3 * ATTN_WIDTH].T
        q, k, vt, y = _inproj(x2d, mod3, g_mix[l], w_in_b, w_vt, cos, sinp, sinm,
                              conv_w[l], conv_norm_g[l], gmat, seq)
        a = _attn(q, k, vt, lambda_q1[l], lambda_k1[l], lambda_q2[l], lambda_k2[l],
                  subln_g[l], lambda_init)
        final_args = (modf3, g_final) if l == depth - 1 else None
        x2d, = _ffn(x2d, mod3, g_ffn2[l], w1b, w3b, w2b, 6, seq, (a, y, w_out_b), final_args)
    return x2d.reshape(bsz, seq, d)
```

```python
import functools
import math

import jax
import jax.numpy as jnp
import numpy as np
from jax import lax
from jax.experimental import pallas as pl
from jax.experimental.pallas import tpu as pltpu

F32 = jnp.float32
BF16 = jnp.bfloat16

LANES = 128
SUBLANES = 8
BF16_ROWS = 16

D_MODEL = 1024
ATTN_HEADS = 4
HEAD_DIM = 64
ATTN_WIDTH = ATTN_HEADS * 2 * HEAD_DIM
CONV_WIDTH = D_MODEL - ATTN_WIDTH
CONV_GROUPS = 8
CONV_K = 3
ROPE_THETA = 10000.0
NORM_EPS = 1e-6
SUBLN_EPS = 1e-5
N_MOD = 9

FF_CHUNK = 256
ROW_TILE = 1024
ROW_SUBTILE = 512
ATTN_TILE = 512
ONES_ROWS = BF16_ROWS
VT_ROWS = 2 * HEAD_DIM + ONES_ROWS
HEADS_PER_STEP = 4
VMEM_LIMIT = 56 * 1024 * 1024
ATTN_VMEM_LIMIT = 60 * 1024 * 1024
CAST_BLOCKS = 4
ADA_COLS = 1024
NT_DIMS = (((1,), (1,)), ((), ()))
Q_SCALE = math.log2(math.e) / math.sqrt(HEAD_DIM)


def _silu(a):
    return a / (1.0 + jnp.exp(-a))


def _rms(x, eps):
    return x * lax.rsqrt(jnp.mean(x * x, axis=-1, keepdims=True) + eps)


def _ada_kernel(c_ref, w_ref, b_ref, o_ref):
    o_ref[...] = jnp.dot(_silu(c_ref[...]), w_ref[...],
                         preferred_element_type=F32) + b_ref[...]


def _ada(c, w, b):
    bsz, d = c.shape
    n = w.shape[1]
    tn = ADA_COLS
    return pl.pallas_call(
        _ada_kernel,
        grid=(n // tn,),
        in_specs=[pl.BlockSpec((bsz, d), lambda j: (0, 0)),
                  pl.BlockSpec((d, tn), lambda j: (0, j)),
                  pl.BlockSpec((1, tn), lambda j: (0, j))],
        out_specs=pl.BlockSpec((bsz, tn), lambda j: (0, j)),
        out_shape=jax.ShapeDtypeStruct((bsz, n), F32),
        name="ada",
    )(c, w, b.reshape(1, n))


def _cast_kernel(*refs):
    n = len(refs) // 2
    for w_ref, o_ref in zip(refs[:n], refs[n:]):
        o_ref[...] = w_ref[...].astype(o_ref.dtype)


def _to_bf16(*ws):
    rows, cols = ws[0].shape
    assert all(w.shape == (rows, cols) for w in ws)
    rb = rows // CAST_BLOCKS
    spec = pl.BlockSpec((rb, cols), lambda i: (i, 0))
    return pl.pallas_call(
        _cast_kernel,
        grid=(CAST_BLOCKS,),
        in_specs=[spec] * len(ws),
        out_specs=[spec] * len(ws),
        out_shape=[jax.ShapeDtypeStruct((rows, cols), BF16)] * len(ws),
        name="cast",
    )(*ws)


def _ffn_kernel(*refs, mod_row, mixer, final, n_side):
    refs = list(refs)
    x_ref, mod_ref, g_ref = refs[:3]
    del refs[:3]
    if mixer:
        a_ref, y_ref, wo_ref = refs[:3]
        del refs[:3]
    w1_ref, w3_ref, w2_ref = refs[:3]
    del refs[:3]
    if final:
        modf_ref, gf_ref = refs[:2]
        del refs[:2]
    side_in = refs[:n_side]
    del refs[:n_side]
    o_ref = refs[0]
    side_out = refs[1:1 + n_side]
    act_ref = refs[1 + n_side]
    for w_ref, wb_ref in zip(side_in, side_out):
        wb_ref[...] = w_ref[...].astype(wb_ref.dtype)
    shift = mod_ref[0, mod_row:mod_row + 1, :]
    scale = mod_ref[0, mod_row + 1:mod_row + 2, :]
    gate = mod_ref[0, mod_row + 2:mod_row + 3, :]
    n_chunks = w2_ref.shape[0] // FF_CHUNK
    sub_tiles = [slice(r * ROW_SUBTILE, (r + 1) * ROW_SUBTILE)
                 for r in range(x_ref.shape[0] // ROW_SUBTILE)]
    hs = []
    for rows in sub_tiles:
        x = x_ref[rows, :]
        if mixer:
            attn = jnp.concatenate([a_ref[0, hd, rows, :] for hd in range(a_ref.shape[1])],
                                   axis=-1)
            mix = (jnp.dot(attn, wo_ref[:ATTN_WIDTH, :], preferred_element_type=F32)
                   + jnp.dot(y_ref[rows, :], wo_ref[ATTN_WIDTH:, :], preferred_element_type=F32))
            x = x + mod_ref[0, mod_row - 1:mod_row, :] * mix
            o_ref[rows, :] = x
        hs.append((_rms(x, NORM_EPS) * g_ref[...] * (1.0 + scale) + shift).astype(BF16))
    for rows, h in zip(sub_tiles, hs):
        for c in range(n_chunks):
            cols = slice(c * FF_CHUNK, (c + 1) * FF_CHUNK)
            a = jnp.dot(h, w1_ref[:, cols], preferred_element_type=F32)
            b = jnp.dot(h, w3_ref[:, cols], preferred_element_type=F32)
            act_ref[rows, cols] = (_silu(a) * b).astype(BF16)
        x = o_ref[rows, :] if mixer else x_ref[rows, :]
        y = x + 0.5 * gate * jnp.dot(act_ref[rows, :], w2_ref[...],
                                     preferred_element_type=F32)
        if final:
            y = (_rms(y, NORM_EPS) * gf_ref[...] * (1.0 + modf_ref[0, 1:2, :])
                 + modf_ref[0, 0:1, :])
        o_ref[rows, :] = y


def _ffn(x2d, mod3, g, w1, w3, w2, mod_row, rows_per_batch, mixer_args=None, final_args=None,
         side_casts=()):
    m, d = x2d.shape
    tm = ROW_TILE
    tiles_per_batch = rows_per_batch // tm
    n_steps = m // tm
    row = lambda i: (i, 0)
    fix = lambda i: (0, 0)
    per_batch = lambda i: (i // tiles_per_batch, 0, 0)
    resident = lambda w: pl.BlockSpec(w.shape, fix, pipeline_mode=pl.Buffered(1))
    in_specs = [pl.BlockSpec((tm, d), row),
                pl.BlockSpec((1, N_MOD, d), per_batch),
                pl.BlockSpec((1, d), fix)]
    args = [x2d, mod3, g.reshape(1, d)]
    if mixer_args is not None:
        a4d, y2d, w_out = mixer_args
        in_specs += [pl.BlockSpec((1, a4d.shape[1], tm, a4d.shape[3]),
                                  lambda i: (i // tiles_per_batch, 0, i % tiles_per_batch, 0)),
                     pl.BlockSpec((tm, y2d.shape[1]), row), resident(w_out)]
        args += [a4d, y2d, w_out]
    in_specs += [resident(w1), resident(w3), resident(w2)]
    args += [w1, w3, w2]
    if final_args is not None:
        modf3, gf = final_args
        in_specs += [pl.BlockSpec((1, 2, d), per_batch), pl.BlockSpec((1, d), fix)]
        args += [modf3, gf.reshape(1, d)]
    side_specs = []
    for w in side_casts:
        reps = 1
        while (w.shape[0] * reps // n_steps) % BF16_ROWS:
            reps *= 2
        side_specs.append(pl.BlockSpec((w.shape[0] * reps // n_steps, w.shape[1]),
                                       functools.partial(lambda i, r: (i // r, 0), r=reps)))
    in_specs += side_specs
    args += list(side_casts)
    return pl.pallas_call(
        functools.partial(_ffn_kernel, mod_row=mod_row, mixer=mixer_args is not None,
                          final=final_args is not None, n_side=len(side_casts)),
        grid=(n_steps,),
        in_specs=in_specs,
        out_specs=[pl.BlockSpec((tm, d), row)] + side_specs,
        out_shape=[jax.ShapeDtypeStruct((m, d), F32)]
        + [jax.ShapeDtypeStruct(w.shape, BF16) for w in side_casts],
        scratch_shapes=[pltpu.VMEM((tm, w2.shape[0]), BF16)],
        compiler_params=pltpu.CompilerParams(
            dimension_semantics=("arbitrary",), vmem_limit_bytes=VMEM_LIMIT),
        name="ffn" + ("_mixer" if mixer_args is not None else "")
        + ("_final" if final_args is not None else ""),
    )(*args)


def _inproj_kernel(x_ref, mod_ref, g_ref, w_ref, wvt_ref, cos_ref, sinp_ref, sinm_ref,
                   convw_ref, cng_ref, gmat_ref,
                   q_ref, k_ref, vt_ref, y_ref, halo_ref, *, mod_row, tiles_per_batch):
    shift = mod_ref[0, mod_row:mod_row + 1, :]
    scale = mod_ref[0, mod_row + 1:mod_row + 2, :]
    half = HEAD_DIM // 2
    w = convw_ref[...]

    @pl.when(pl.program_id(0) % tiles_per_batch == 0)
    def _():
        halo_ref[...] = jnp.zeros_like(halo_ref)

    prev = halo_ref[...]
    for r in range(x_ref.shape[0] // ROW_SUBTILE):
        rows = slice(r * ROW_SUBTILE, (r + 1) * ROW_SUBTILE)
        h = (_rms(x_ref[rows, :], NORM_EPS) * g_ref[...] * (1.0 + scale) + shift).astype(BF16)
        zqk = jnp.dot(h, w_ref[:, :2 * ATTN_WIDTH], preferred_element_type=F32)
        zcv = jnp.dot(h, w_ref[:, 3 * ATTN_WIDTH:], preferred_element_type=F32)
        vt = lax.dot_general(wvt_ref[...], h, NT_DIMS, preferred_element_type=F32).astype(BF16)
        for j in range(ATTN_HEADS):
            vt_ref[0, j * VT_ROWS:j * VT_ROWS + LANES, rows] = vt[j * LANES:(j + 1) * LANES, :]
            vt_ref[0, j * VT_ROWS + LANES:(j + 1) * VT_ROWS, rows] = jnp.ones(
                (ONES_ROWS, ROW_SUBTILE), BF16)
        cos = cos_ref[rows, :]
        sinp = sinp_ref[rows, :]
        sinm = sinm_ref[rows, :]
        for j in range(ATTN_WIDTH // LANES):
            for base, out_ref, mult in ((0, q_ref, Q_SCALE), (ATTN_WIDTH, k_ref, 1.0)):
                t = zqk[:, base + j * LANES: base + (j + 1) * LANES]
                rot = (t * cos + pltpu.roll(t, half, axis=1) * sinp
                       + pltpu.roll(t, LANES - half, axis=1) * sinm)
                out_ref[0, j, rows, :] = (rot * mult).astype(BF16)

        gate_b = zcv[:, :CONV_WIDTH]
        cu = zcv[:, CONV_WIDTH:2 * CONV_WIDTH] * zcv[:, 2 * CONV_WIDTH:]
        buf = jnp.concatenate([prev, cu], axis=0)
        cu_m1 = pltpu.roll(buf, 1, axis=0)[SUBLANES:, :]
        cu_m2 = pltpu.roll(buf, 2, axis=0)[SUBLANES:, :]
        prev = cu[ROW_SUBTILE - SUBLANES:, :]
        y = gate_b * (w[0:1, :] * cu_m2 + w[1:2, :] * cu_m1 + w[2:3, :] * cu)
        y2 = y * y
        y2_hi = y2.astype(BF16)
        y2_lo = (y2 - y2_hi.astype(F32)).astype(BF16)
        ms = (jnp.dot(y2_hi, gmat_ref[...], preferred_element_type=F32)
              + jnp.dot(y2_lo, gmat_ref[...], preferred_element_type=F32))
        y_ref[rows, :] = (y * lax.rsqrt(ms + NORM_EPS) * cng_ref[...]).astype(BF16)
    halo_ref[...] = prev


def _inproj(x2d, mod3, g, w_in, w_vt, cos, sinp, sinm, conv_w, conv_norm_g, gmat,
            rows_per_batch):
    m, d = x2d.shape
    tm = ROW_TILE
    tiles_per_batch = rows_per_batch // tm
    row = lambda i: (i, 0)
    fix = lambda i: (0, 0)
    pos = lambda i: (i % tiles_per_batch, 0)
    n_batch = m // rows_per_batch
    head_major = pl.BlockSpec((1, ATTN_HEADS, tm, LANES),
                              lambda i: (i // tiles_per_batch, 0, i % tiles_per_batch, 0))
    out_sd = jax.ShapeDtypeStruct((n_batch, ATTN_HEADS, rows_per_batch, LANES), BF16)
    return pl.pallas_call(
        functools.partial(_inproj_kernel, mod_row=3, tiles_per_batch=tiles_per_batch),
        grid=(m // tm,),
        in_specs=[pl.BlockSpec((tm, d), row),
                  pl.BlockSpec((1, N_MOD, d), lambda i: (i // tiles_per_batch, 0, 0)),
                  pl.BlockSpec((1, d), fix),
                  pl.BlockSpec(w_in.shape, fix, pipeline_mode=pl.Buffered(1)),
                  pl.BlockSpec(w_vt.shape, fix, pipeline_mode=pl.Buffered(1)),
                  pl.BlockSpec((tm, LANES), pos),
                  pl.BlockSpec((tm, LANES), pos),
                  pl.BlockSpec((tm, LANES), pos),
                  pl.BlockSpec((CONV_K, CONV_WIDTH), fix),
                  pl.BlockSpec((1, CONV_WIDTH), fix),
                  pl.BlockSpec((CONV_WIDTH, CONV_WIDTH), fix)],
        out_specs=[head_major, head_major,
                   pl.BlockSpec((1, ATTN_HEADS * VT_ROWS, tm),
                                lambda i: (i // tiles_per_batch, 0, i % tiles_per_batch)),
                   pl.BlockSpec((tm, CONV_WIDTH), row)],
        out_shape=[out_sd, out_sd,
                   jax.ShapeDtypeStruct((n_batch, ATTN_HEADS * VT_ROWS, rows_per_batch), BF16),
                   jax.ShapeDtypeStruct((m, CONV_WIDTH), BF16)],
        scratch_shapes=[pltpu.VMEM((SUBLANES, CONV_WIDTH), F32)],
        compiler_params=pltpu.CompilerParams(
            dimension_semantics=("arbitrary",), vmem_limit_bytes=VMEM_LIMIT),
        name="inproj",
    )(x2d, mod3, g.reshape(1, d), w_in, w_vt, cos, sinp, sinm, conv_w,
      conv_norm_g.reshape(1, CONV_WIDTH), gmat)


def _attn_kernel(q_ref, k_ref, vta_ref, lq1_ref, lk1_ref, lq2_ref, lk2_ref, sg_ref,
                 o_ref, sa_ref, sb_ref, cma_ref, cmb_ref, m_ref, acc_ref, *, lambda_init):
    seq = q_ref.shape[2]
    t = ATTN_TILE
    dv = 2 * HEAD_DIM
    n_tiles = seq // t
    heads = range(HEADS_PER_STEP)
    lam = (jnp.exp(jnp.sum(lq1_ref[...] * lk1_ref[...], axis=-1, keepdims=True))
           - jnp.exp(jnp.sum(lq2_ref[...] * lk2_ref[...], axis=-1, keepdims=True))
           + lambda_init)

    def stacked_queries(hd, i):
        q = q_ref[0, hd, pl.ds(pl.multiple_of(i * t, t), t), :]
        first_map = lax.broadcasted_iota(jnp.int32, (t, LANES), 1) < HEAD_DIM
        zero = jnp.zeros_like(q)
        return jnp.concatenate([jnp.where(first_map, q, zero),
                                jnp.where(first_map, zero, q)], axis=0)

    def scores_into(s_buf, cm_buf, qq, k0):
        for hd in heads:
            s = lax.dot_general(k_ref[0, hd, pl.ds(k0, t), :], qq[hd], NT_DIMS,
                                preferred_element_type=F32)
            s_buf[hd] = s
            cm_buf[hd] = jnp.max(s, axis=0, keepdims=True)

    def absorb(s_buf, cm_buf, k0, mask=False):
        for hd in heads:
            s = s_buf[hd]
            if mask:
                kpos = lax.broadcasted_iota(jnp.int32, (t, 2 * t), 0)
                qpos = lax.broadcasted_iota(jnp.int32, (t, 2 * t), 1)
                qpos = jnp.where(qpos >= t, qpos - t, qpos)
                s = jnp.where(kpos <= qpos, s, -jnp.inf)
                cm = jnp.max(s, axis=0, keepdims=True)
            else:
                cm = cm_buf[hd]
            m = m_ref[hd]
            m_new = jnp.maximum(m, cm)
            alpha = jnp.exp2(m - m_new)
            p = jnp.exp2(s - m_new).astype(BF16)
            m_ref[hd] = m_new
            pv = jnp.dot(vta_ref[0, hd * VT_ROWS:(hd + 1) * VT_ROWS, pl.ds(k0, t)], p,
                         preferred_element_type=F32)
            acc_ref[hd] = alpha * acc_ref[hd] + pv

    def q_tile(i, odd):
        q0 = pl.multiple_of(i * t, t)
        nxt = jnp.minimum(i + 1, n_tiles - 1)
        qq = [stacked_queries(hd, i) for hd in heads]
        qq_next = [stacked_queries(hd, nxt) for hd in heads]
        m_ref[...] = jnp.full(m_ref.shape, -jnp.inf, F32)
        acc_ref[...] = jnp.zeros_like(acc_ref)
        if not odd:
            scores_into(sb_ref, cmb_ref, [jnp.where(i == 0, b, a) for a, b in zip(qq, qq_next)], 0)
        absorb(sa_ref, cma_ref, q0, mask=True)

        def pair(jj, carry):
            k_even = pl.multiple_of(2 * jj * t, t)
            k_odd = pl.multiple_of(k_even + t, t)
            scores_into(sa_ref, cma_ref, qq, k_odd)
            absorb(sb_ref, cmb_ref, k_even)
            if odd:
                scores_into(sb_ref, cmb_ref, qq, pl.multiple_of(k_odd + t, t))
            else:
                done = 2 * jj + 2 >= i
                scores_into(sb_ref, cmb_ref,
                            [jnp.where(done, b, a) for a, b in zip(qq, qq_next)],
                            pl.multiple_of(jnp.where(done, 0, 2 * jj + 2) * t, t))
            absorb(sa_ref, cma_ref, k_odd)
            return carry

        lax.fori_loop(0, i // 2, pair, 0)
        scores_into(sa_ref, cma_ref, qq_next, pl.multiple_of(nxt * t, t))
        if odd:
            absorb(sb_ref, cmb_ref, pl.multiple_of((i - 1) * t, t))
        for hd in heads:
            acc = acc_ref[hd]
            o = acc[:dv, :] / acc[dv:dv + 1, :]
            o = o[:, :t] - lam * o[:, t:]
            o = o * lax.rsqrt(jnp.mean(o * o, axis=0, keepdims=True) + SUBLN_EPS)
            o = o * (sg_ref[...] * (1.0 - lambda_init))
            o_ref[0, hd, pl.ds(q0, t), :] = o.T.astype(BF16)

    def q_tile_pair(a, carry):
        q_tile(2 * a, odd=False)
        q_tile(2 * a + 1, odd=True)
        return carry

    scores_into(sa_ref, cma_ref, [stacked_queries(hd, 0) for hd in heads], 0)
    lax.fori_loop(0, n_tiles // 2, q_tile_pair, 0)


def _attn(q, k, vta, lq1, lk1, lq2, lk2, subln_g, lambda_init):
    bsz, n_heads, seq, _ = q.shape
    t = ATTN_TILE
    nh = HEADS_PER_STEP
    assert (seq // t) % 2 == 0 and n_heads % nh == 0 and vta.shape[1] == n_heads * VT_ROWS
    blk = pl.BlockSpec((1, nh, seq, LANES), lambda b, h: (b, h, 0, 0))
    vec = pl.BlockSpec((1, HEAD_DIM), lambda b, h: (0, 0))
    return pl.pallas_call(
        functools.partial(_attn_kernel, lambda_init=lambda_init),
        grid=(bsz, n_heads // nh),
        in_specs=[blk, blk,
                  pl.BlockSpec((1, nh * VT_ROWS, seq), lambda b, h: (b, h, 0)),
                  vec, vec, vec, vec,
                  pl.BlockSpec((2 * HEAD_DIM, 1), lambda b, h: (0, 0))],
        out_specs=blk,
        out_shape=jax.ShapeDtypeStruct((bsz, n_heads, seq, LANES), BF16),
        scratch_shapes=[pltpu.VMEM((nh, t, 2 * t), F32),
                        pltpu.VMEM((nh, t, 2 * t), F32),
                        pltpu.VMEM((nh, 1, 2 * t), F32),
                        pltpu.VMEM((nh, 1, 2 * t), F32),
                        pltpu.VMEM((nh, 1, 2 * t), F32),
                        pltpu.VMEM((nh, VT_ROWS, 2 * t), F32)],
        compiler_params=pltpu.CompilerParams(
            dimension_semantics=("arbitrary", "arbitrary"),
            vmem_limit_bytes=ATTN_VMEM_LIMIT),
        name="attn",
    )(q, k, vta, lq1.reshape(1, -1), lk1.reshape(1, -1), lq2.reshape(1, -1),
      lk2.reshape(1, -1), subln_g.reshape(-1, 1))


def _rope_tables(seq):
    inv = 1.0 / (ROPE_THETA ** (np.arange(0, HEAD_DIM, 2, dtype=np.float64) / HEAD_DIM))
    ang = np.arange(seq, dtype=np.float64)[:, None] * inv[None, :]
    ang = np.concatenate([ang, ang, ang, ang], axis=-1)
    cos, sin = np.cos(ang), np.sin(ang)
    upper = (np.arange(LANES) % HEAD_DIM) >= HEAD_DIM // 2
    as_f32 = lambda a: jnp.asarray(a.astype(np.float32))
    return as_f32(cos), as_f32(np.where(upper, sin, 0.0)), as_f32(np.where(upper, 0.0, -sin))


def kernel(x, c, w_ada, b_ada, w_ada_final, b_ada_final, g_ffn1, g_mix, g_ffn2, g_final,
           ffn1_w1, ffn1_w3, ffn1_w2, ffn2_w1, ffn2_w3, ffn2_w2, w_in,
           lambda_q1, lambda_k1, lambda_q2, lambda_k2, subln_g, conv_w, conv_norm_g, w_out):
    bsz, seq, d = x.shape
    depth = w_ada.shape[0]
    cos, sinp, sinm = _rope_tables(seq)
    gsz = CONV_WIDTH // CONV_GROUPS
    grp = np.arange(CONV_WIDTH) // gsz
    gmat = jnp.asarray(np.where(grp[:, None] == grp[None, :], 1.0 / gsz, 0.0), dtype=BF16)
    modf3 = _ada(c, w_ada_final, b_ada_final).reshape(bsz, 2, d)

    x2d = x.reshape(bsz * seq, d)
    for l in range(depth):
        lambda_init = 0.8 - 0.6 * math.exp(-0.3 * l)
        mod3 = _ada(c, w_ada[l], b_ada[l]).reshape(bsz, N_MOD, d)
        w1a, w3a = _to_bf16(ffn1_w1[l], ffn1_w3[l])
        w2a, = _to_bf16(ffn1_w2[l])
        x2d, w_in_b, w_out_b, w1b, w3b, w2b = _ffn(
            x2d, mod3, g_ffn1[l], w1a, w3a, w2a, 0, seq,
            side_casts=(w_in[l], w_out[l], ffn2_w1[l], ffn2_w3[l], ffn2_w2[l]))
        w_vt = w_in_b[:, 2 * ATTN_WIDTH:3 * ATTN_WIDTH].T
        q, k, vt, y = _inproj(x2d, mod3, g_mix[l], w_in_b, w_vt, cos, sinp, sinm,
                              conv_w[l], conv_norm_g[l], gmat, seq)
        a = _attn(q, k, vt, lambda_q1[l], lambda_k1[l], lambda_q2[l], lambda_k2[l],
                  subln_g[l], lambda_init)
        final_args = (modf3, g_final) if l == depth - 1 else None
        x2d, = _ffn(x2d, mod3, g_ffn2[l], w1b, w3b, w2b, 6, seq, (a, y, w_out_b), final_args)
    return x2d.reshape(bsz, seq, d)
```

```python
import functools
import math

import jax
import jax.numpy as jnp
import numpy as np
from jax import lax
from jax.experimental import pallas as pl
from jax.experimental.pallas import tpu as pltpu

F32 = jnp.float32
BF16 = jnp.bfloat16

LANES = 128
SUBLANES = 8
BF16_ROWS = 16

D_MODEL = 1024
ATTN_HEADS = 4
HEAD_DIM = 64
ATTN_WIDTH = ATTN_HEADS * 2 * HEAD_DIM
CONV_WIDTH = D_MODEL - ATTN_WIDTH
CONV_GROUPS = 8
CONV_K = 3
ROPE_THETA = 10000.0
NORM_EPS = 1e-6
SUBLN_EPS = 1e-5
N_MOD = 9

FF_CHUNK = 256
ROW_TILE = 1024
ROW_SUBTILE = 512
ATTN_TILE = 512
ONES_ROWS = BF16_ROWS
VT_ROWS = 2 * HEAD_DIM + ONES_ROWS
HEADS_PER_STEP = 4
VMEM_LIMIT = 56 * 1024 * 1024
ATTN_VMEM_LIMIT = 60 * 1024 * 1024
ADA_BLOCKS = 8
NT_DIMS = (((1,), (1,)), ((), ()))
Q_SCALE = math.log2(math.e) / math.sqrt(HEAD_DIM)


def _silu(a):
    return a / (1.0 + jnp.exp(-a))


def _rms(x, eps):
    return x * lax.rsqrt(jnp.mean(x * x, axis=-1, keepdims=True) + eps)


def _ada_kernel(*refs):
    n_side = (len(refs) - 4) // 2
    c_ref, w_ref, b_ref = refs[:3]
    side_in = refs[3:3 + n_side]
    o_ref = refs[3 + n_side]
    side_out = refs[4 + n_side:]
    o_ref[...] = jnp.dot(_silu(c_ref[...]), w_ref[...],
                         preferred_element_type=F32) + b_ref[...]
    for f_ref, bf_ref in zip(side_in, side_out):
        bf_ref[...] = f_ref[...].astype(bf_ref.dtype)


def _ada(c, w, b, side_casts=()):
    bsz, d = c.shape
    n = w.shape[1]
    tn = n // ADA_BLOCKS
    side_specs = [pl.BlockSpec((s.shape[0] // ADA_BLOCKS, s.shape[1]), lambda j: (j, 0))
                  for s in side_casts]
    return pl.pallas_call(
        _ada_kernel,
        grid=(ADA_BLOCKS,),
        in_specs=[pl.BlockSpec((bsz, d), lambda j: (0, 0)),
                  pl.BlockSpec((d, tn), lambda j: (0, j)),
                  pl.BlockSpec((1, tn), lambda j: (0, j))] + side_specs,
        out_specs=[pl.BlockSpec((bsz, tn), lambda j: (0, j))] + side_specs,
        out_shape=[jax.ShapeDtypeStruct((bsz, n), F32)]
        + [jax.ShapeDtypeStruct(s.shape, BF16) for s in side_casts],
        compiler_params=pltpu.CompilerParams(vmem_limit_bytes=VMEM_LIMIT),
        name="ada",
    )(c, w, b.reshape(1, n), *side_casts)


def _ffn_kernel(*refs, mod_row, mixer, final, n_side):
    refs = list(refs)
    x_ref, mod_ref, g_ref = refs[:3]
    del refs[:3]
    if mixer:
        a_ref, y_ref, wo_ref = refs[:3]
        del refs[:3]
    w1_ref, w3_ref, w2_ref = refs[:3]
    del refs[:3]
    if final:
        modf_ref, gf_ref = refs[:2]
        del refs[:2]
    side_in = refs[:n_side]
    del refs[:n_side]
    o_ref = refs[0]
    side_out = refs[1:1 + n_side]
    act_ref = refs[1 + n_side]
    for w_ref, wb_ref in zip(side_in, side_out):
        wb_ref[...] = w_ref[...].astype(wb_ref.dtype)
    shift = mod_ref[0, mod_row:mod_row + 1, :]
    scale = mod_ref[0, mod_row + 1:mod_row + 2, :]
    gate = mod_ref[0, mod_row + 2:mod_row + 3, :]
    n_chunks = w2_ref.shape[0] // FF_CHUNK
    sub_tiles = [slice(r * ROW_SUBTILE, (r + 1) * ROW_SUBTILE)
                 for r in range(x_ref.shape[0] // ROW_SUBTILE)]
    hs = []
    for rows in sub_tiles:
        x = x_ref[rows, :]
        if mixer:
            attn = jnp.concatenate([a_ref[0, hd, rows, :] for hd in range(a_ref.shape[1])],
                                   axis=-1)
            mix = (jnp.dot(attn, wo_ref[:ATTN_WIDTH, :], preferred_element_type=F32)
                   + jnp.dot(y_ref[rows, :], wo_ref[ATTN_WIDTH:, :], preferred_element_type=F32))
            x = x + mod_ref[0, mod_row - 1:mod_row, :] * mix
            o_ref[rows, :] = x
        hs.append((_rms(x, NORM_EPS) * g_ref[...] * (1.0 + scale) + shift).astype(BF16))
    for rows, h in zip(sub_tiles, hs):
        for c in range(n_chunks):
            cols = slice(c * FF_CHUNK, (c + 1) * FF_CHUNK)
            a = jnp.dot(h, w1_ref[:, cols], preferred_element_type=F32)
            b = jnp.dot(h, w3_ref[:, cols], preferred_element_type=F32)
            act_ref[rows, cols] = (_silu(a) * b).astype(BF16)
        x = o_ref[rows, :] if mixer else x_ref[rows, :]
        y = x + 0.5 * gate * jnp.dot(act_ref[rows, :], w2_ref[...],
                                     preferred_element_type=F32)
        if final:
            y = (_rms(y, NORM_EPS) * gf_ref[...] * (1.0 + modf_ref[0, 1:2, :])
                 + modf_ref[0, 0:1, :])
        o_ref[rows, :] = y


def _ffn(x2d, mod3, g, w1, w3, w2, mod_row, rows_per_batch, mixer_args=None, final_args=None,
         side_casts=()):
    m, d = x2d.shape
    tm = ROW_TILE
    tiles_per_batch = rows_per_batch // tm
    n_steps = m // tm
    row = lambda i: (i, 0)
    fix = lambda i: (0, 0)
    per_batch = lambda i: (i // tiles_per_batch, 0, 0)
    resident = lambda w: pl.BlockSpec(w.shape, fix, pipeline_mode=pl.Buffered(1))
    in_specs = [pl.BlockSpec((tm, d), row),
                pl.BlockSpec((1, N_MOD, d), per_batch),
                pl.BlockSpec((1, d), fix)]
    args = [x2d, mod3, g.reshape(1, d)]
    if mixer_args is not None:
        a4d, y2d, w_out = mixer_args
        in_specs += [pl.BlockSpec((1, a4d.shape[1], tm, a4d.shape[3]),
                                  lambda i: (i // tiles_per_batch, 0, i % tiles_per_batch, 0)),
                     pl.BlockSpec((tm, y2d.shape[1]), row), resident(w_out)]
        args += [a4d, y2d, w_out]
    in_specs += [resident(w1), resident(w3), resident(w2)]
    args += [w1, w3, w2]
    if final_args is not None:
        modf3, gf = final_args
        in_specs += [pl.BlockSpec((1, 2, d), per_batch), pl.BlockSpec((1, d), fix)]
        args += [modf3, gf.reshape(1, d)]
    side_specs = []
    for w in side_casts:
        reps = 1
        while (w.shape[0] * reps // n_steps) % BF16_ROWS:
            reps *= 2
        side_specs.append(pl.BlockSpec((w.shape[0] * reps // n_steps, w.shape[1]),
                                       functools.partial(lambda i, r: (i // r, 0), r=reps)))
    in_specs += side_specs
    args += list(side_casts)
    return pl.pallas_call(
        functools.partial(_ffn_kernel, mod_row=mod_row, mixer=mixer_args is not None,
                          final=final_args is not None, n_side=len(side_casts)),
        grid=(n_steps,),
        in_specs=in_specs,
        out_specs=[pl.BlockSpec((tm, d), row)] + side_specs,
        out_shape=[jax.ShapeDtypeStruct((m, d), F32)]
        + [jax.ShapeDtypeStruct(w.shape, BF16) for w in side_casts],
        scratch_shapes=[pltpu.VMEM((tm, w2.shape[0]), BF16)],
        compiler_params=pltpu.CompilerParams(
            dimension_semantics=("arbitrary",), vmem_limit_bytes=VMEM_LIMIT),
        name="ffn" + ("_mixer" if mixer_args is not None else "")
        + ("_final" if final_args is not None else ""),
    )(*args)


def _inproj_kernel(x_ref, mod_ref, g_ref, w_ref, wvt_ref, cos_ref, sinp_ref, sinm_ref,
                   convw_ref, cng_ref, gmat_ref,
                   q_ref, k_ref, vt_ref, y_ref, halo_ref, *, mod_row, tiles_per_batch):
    shift = mod_ref[0, mod_row:mod_row + 1, :]
    scale = mod_ref[0, mod_row + 1:mod_row + 2, :]
    half = HEAD_DIM // 2
    w = convw_ref[...]

    @pl.when(pl.program_id(0) % tiles_per_batch == 0)
    def _():
        halo_ref[...] = jnp.zeros_like(halo_ref)

    prev = halo_ref[...]
    for r in range(x_ref.shape[0] // ROW_SUBTILE):
        rows = slice(r * ROW_SUBTILE, (r + 1) * ROW_SUBTILE)
        h = (_rms(x_ref[rows, :], NORM_EPS) * g_ref[...] * (1.0 + scale) + shift).astype(BF16)
        zqk = jnp.dot(h, w_ref[:, :2 * ATTN_WIDTH], preferred_element_type=F32)
        zcv = jnp.dot(h, w_ref[:, 3 * ATTN_WIDTH:], preferred_element_type=F32)
        vt = lax.dot_general(wvt_ref[...], h, NT_DIMS, preferred_element_type=F32).astype(BF16)
        for j in range(ATTN_HEADS):
            vt_ref[0, j * VT_ROWS:j * VT_ROWS + LANES, rows] = vt[j * LANES:(j + 1) * LANES, :]
            vt_ref[0, j * VT_ROWS + LANES:(j + 1) * VT_ROWS, rows] = jnp.ones(
                (ONES_ROWS, ROW_SUBTILE), BF16)
        cos = cos_ref[rows, :]
        sinp = sinp_ref[rows, :]
        sinm = sinm_ref[rows, :]
        for j in range(ATTN_WIDTH // LANES):
            for base, out_ref, mult in ((0, q_ref, Q_SCALE), (ATTN_WIDTH, k_ref, 1.0)):
                t = zqk[:, base + j * LANES: base + (j + 1) * LANES]
                rot = (t * cos + pltpu.roll(t, half, axis=1) * sinp
                       + pltpu.roll(t, LANES - half, axis=1) * sinm)
                out_ref[0, j, rows, :] = (rot * mult).astype(BF16)

        gate_b = zcv[:, :CONV_WIDTH]
        cu = zcv[:, CONV_WIDTH:2 * CONV_WIDTH] * zcv[:, 2 * CONV_WIDTH:]
        buf = jnp.concatenate([prev, cu], axis=0)
        cu_m1 = pltpu.roll(buf, 1, axis=0)[SUBLANES:, :]
        cu_m2 = pltpu.roll(buf, 2, axis=0)[SUBLANES:, :]
        prev = cu[ROW_SUBTILE - SUBLANES:, :]
        y = gate_b * (w[0:1, :] * cu_m2 + w[1:2, :] * cu_m1 + w[2:3, :] * cu)
        y2 = y * y
        y2_hi = y2.astype(BF16)
        y2_lo = (y2 - y2_hi.astype(F32)).astype(BF16)
        ms = (jnp.dot(y2_hi, gmat_ref[...], preferred_element_type=F32)
              + jnp.dot(y2_lo, gmat_ref[...], preferred_element_type=F32))
        y_ref[rows, :] = (y * lax.rsqrt(ms + NORM_EPS) * cng_ref[...]).astype(BF16)
    halo_ref[...] = prev


def _inproj(x2d, mod3, g, w_in, w_vt, cos, sinp, sinm, conv_w, conv_norm_g, gmat,
            rows_per_batch):
    m, d = x2d.shape
    tm = ROW_TILE
    tiles_per_batch = rows_per_batch // tm
    row = lambda i: (i, 0)
    fix = lambda i: (0, 0)
    pos = lambda i: (i % tiles_per_batch, 0)
    n_batch = m // rows_per_batch
    head_major = pl.BlockSpec((1, ATTN_HEADS, tm, LANES),
                              lambda i: (i // tiles_per_batch, 0, i % tiles_per_batch, 0))
    out_sd = jax.ShapeDtypeStruct((n_batch, ATTN_HEADS, rows_per_batch, LANES), BF16)
    return pl.pallas_call(
        functools.partial(_inproj_kernel, mod_row=3, tiles_per_batch=tiles_per_batch),
        grid=(m // tm,),
        in_specs=[pl.BlockSpec((tm, d), row),
                  pl.BlockSpec((1, N_MOD, d), lambda i: (i // tiles_per_batch, 0, 0)),
                  pl.BlockSpec((1, d), fix),
                  pl.BlockSpec(w_in.shape, fix, pipeline_mode=pl.Buffered(1)),
                  pl.BlockSpec(w_vt.shape, fix, pipeline_mode=pl.Buffered(1)),
                  pl.BlockSpec((tm, LANES), pos),
                  pl.BlockSpec((tm, LANES), pos),
                  pl.BlockSpec((tm, LANES), pos),
                  pl.BlockSpec((CONV_K, CONV_WIDTH), fix),
                  pl.BlockSpec((1, CONV_WIDTH), fix),
                  pl.BlockSpec((CONV_WIDTH, CONV_WIDTH), fix)],
        out_specs=[head_major, head_major,
                   pl.BlockSpec((1, ATTN_HEADS * VT_ROWS, tm),
                                lambda i: (i // tiles_per_batch, 0, i % tiles_per_batch)),
                   pl.BlockSpec((tm, CONV_WIDTH), row)],
        out_shape=[out_sd, out_sd,
                   jax.ShapeDtypeStruct((n_batch, ATTN_HEADS * VT_ROWS, rows_per_batch), BF16),
                   jax.ShapeDtypeStruct((m, CONV_WIDTH), BF16)],
        scratch_shapes=[pltpu.VMEM((SUBLANES, CONV_WIDTH), F32)],
        compiler_params=pltpu.CompilerParams(
            dimension_semantics=("arbitrary",), vmem_limit_bytes=VMEM_LIMIT),
        name="inproj",
    )(x2d, mod3, g.reshape(1, d), w_in, w_vt, cos, sinp, sinm, conv_w,
      conv_norm_g.reshape(1, CONV_WIDTH), gmat)


def _attn_kernel(q_ref, k_ref, vta_ref, lq1_ref, lk1_ref, lq2_ref, lk2_ref, sg_ref,
                 o_ref, sa_ref, sb_ref, cma_ref, cmb_ref, m_ref, acc_ref, *, lambda_init):
    seq = q_ref.shape[2]
    t = ATTN_TILE
    dv = 2 * HEAD_DIM
    n_tiles = seq // t
    heads = range(HEADS_PER_STEP)
    lam = (jnp.exp(jnp.sum(lq1_ref[...] * lk1_ref[...], axis=-1, keepdims=True))
           - jnp.exp(jnp.sum(lq2_ref[...] * lk2_ref[...], axis=-1, keepdims=True))
           + lambda_init)

    def stacked_queries(hd, i):
        q = q_ref[0, hd, pl.ds(pl.multiple_of(i * t, t), t), :]
        first_map = lax.broadcasted_iota(jnp.int32, (t, LANES), 1) < HEAD_DIM
        zero = jnp.zeros_like(q)
        return jnp.concatenate([jnp.where(first_map, q, zero),
                                jnp.where(first_map, zero, q)], axis=0)

    def scores_into(s_buf, cm_buf, qq, k0):
        for hd in heads:
            s = lax.dot_general(k_ref[0, hd, pl.ds(k0, t), :], qq[hd], NT_DIMS,
                                preferred_element_type=F32)
            s_buf[hd] = s
            cm_buf[hd] = jnp.max(s, axis=0, keepdims=True)

    def absorb(s_buf, cm_buf, k0, mask=False):
        for hd in heads:
            s = s_buf[hd]
            if mask:
                kpos = lax.broadcasted_iota(jnp.int32, (t, 2 * t), 0)
                qpos = lax.broadcasted_iota(jnp.int32, (t, 2 * t), 1)
                qpos = jnp.where(qpos >= t, qpos - t, qpos)
                s = jnp.where(kpos <= qpos, s, -jnp.inf)
                cm = jnp.max(s, axis=0, keepdims=True)
            else:
                cm = cm_buf[hd]
            m = m_ref[hd]
            m_new = jnp.maximum(m, cm)
            alpha = jnp.exp2(m - m_new)
            p = jnp.exp2(s - m_new).astype(BF16)
            m_ref[hd] = m_new
            pv = jnp.dot(vta_ref[0, hd * VT_ROWS:(hd + 1) * VT_ROWS, pl.ds(k0, t)], p,
                         preferred_element_type=F32)
            acc_ref[hd] = alpha * acc_ref[hd] + pv

    def q_tile(i, odd):
        q0 = pl.multiple_of(i * t, t)
        nxt = jnp.minimum(i + 1, n_tiles - 1)
        qq = [stacked_queries(hd, i) for hd in heads]
        qq_next = [stacked_queries(hd, nxt) for hd in heads]
        m_ref[...] = jnp.full(m_ref.shape, -jnp.inf, F32)
        acc_ref[...] = jnp.zeros_like(acc_ref)
        if not odd:
            scores_into(sb_ref, cmb_ref, [jnp.where(i == 0, b, a) for a, b in zip(qq, qq_next)], 0)
        absorb(sa_ref, cma_ref, q0, mask=True)

        def pair(jj, carry):
            k_even = pl.multiple_of(2 * jj * t, t)
            k_odd = pl.multiple_of(k_even + t, t)
            scores_into(sa_ref, cma_ref, qq, k_odd)
            absorb(sb_ref, cmb_ref, k_even)
            if odd:
                scores_into(sb_ref, cmb_ref, qq, pl.multiple_of(k_odd + t, t))
            else:
                done = 2 * jj + 2 >= i
                scores_into(sb_ref, cmb_ref,
                            [jnp.where(done, b, a) for a, b in zip(qq, qq_next)],
                            pl.multiple_of(jnp.where(done, 0, 2 * jj + 2) * t, t))
            absorb(sa_ref, cma_ref, k_odd)
            return carry

        lax.fori_loop(0, i // 2, pair, 0)
        scores_into(sa_ref, cma_ref, qq_next, pl.multiple_of(nxt * t, t))
        if odd:
            absorb(sb_ref, cmb_ref, pl.multiple_of((i - 1) * t, t))
        for hd in heads:
            acc = acc_ref[hd]
            o = acc[:dv, :] / acc[dv:dv + 1, :]
            o = o[:, :t] - lam * o[:, t:]
            o = o * lax.rsqrt(jnp.mean(o * o, axis=0, keepdims=True) + SUBLN_EPS)
            o = o * (sg_ref[...] * (1.0 - lambda_init))
            o_ref[0, hd, pl.ds(q0, t), :] = o.T.astype(BF16)

    def q_tile_pair(a, carry):
        q_tile(2 * a, odd=False)
        q_tile(2 * a + 1, odd=True)
        return carry

    scores_into(sa_ref, cma_ref, [stacked_queries(hd, 0) for hd in heads], 0)
    lax.fori_loop(0, n_tiles // 2, q_tile_pair, 0)


def _attn(q, k, vta, lq1, lk1, lq2, lk2, subln_g, lambda_init):
    bsz, n_heads, seq, _ = q.shape
    t = ATTN_TILE
    nh = HEADS_PER_STEP
    assert (seq // t) % 2 == 0 and n_heads % nh == 0 and vta.shape[1] == n_heads * VT_ROWS
    blk = pl.BlockSpec((1, nh, seq, LANES), lambda b, h: (b, h, 0, 0))
    vec = pl.BlockSpec((1, HEAD_DIM), lambda b, h: (0, 0))
    return pl.pallas_call(
        functools.partial(_attn_kernel, lambda_init=lambda_init),
        grid=(bsz, n_heads // nh),
        in_specs=[blk, blk,
                  pl.BlockSpec((1, nh * VT_ROWS, seq), lambda b, h: (b, h, 0)),
                  vec, vec, vec, vec,
                  pl.BlockSpec((2 * HEAD_DIM, 1), lambda b, h: (0, 0))],
        out_specs=blk,
        out_shape=jax.ShapeDtypeStruct((bsz, n_heads, seq, LANES), BF16),
        scratch_shapes=[pltpu.VMEM((nh, t, 2 * t), F32),
                        pltpu.VMEM((nh, t, 2 * t), F32),
                        pltpu.VMEM((nh, 1, 2 * t), F32),
                        pltpu.VMEM((nh, 1, 2 * t), F32),
                        pltpu.VMEM((nh, 1, 2 * t), F32),
                        pltpu.VMEM((nh, VT_ROWS, 2 * t), F32)],
        compiler_params=pltpu.CompilerParams(
            dimension_semantics=("arbitrary", "arbitrary"),
            vmem_limit_bytes=ATTN_VMEM_LIMIT),
        name="attn",
    )(q, k, vta, lq1.reshape(1, -1), lk1.reshape(1, -1), lq2.reshape(1, -1),
      lk2.reshape(1, -1), subln_g.reshape(-1, 1))


def _rope_tables(seq):
    inv = 1.0 / (ROPE_THETA ** (np.arange(0, HEAD_DIM, 2, dtype=np.float64) / HEAD_DIM))
    ang = np.arange(seq, dtype=np.float64)[:, None] * inv[None, :]
    ang = np.concatenate([ang, ang, ang, ang], axis=-1)
    cos, sin = np.cos(ang), np.sin(ang)
    upper = (np.arange(LANES) % HEAD_DIM) >= HEAD_DIM // 2
    as_f32 = lambda a: jnp.asarray(a.astype(np.float32))
    return as_f32(cos), as_f32(np.where(upper, sin, 0.0)), as_f32(np.where(upper, 0.0, -sin))


def kernel(x, c, w_ada, b_ada, w_ada_final, b_ada_final, g_ffn1, g_mix, g_ffn2, g_final,
           ffn1_w1, ffn1_w3, ffn1_w2, ffn2_w1, ffn2_w3, ffn2_w2, w_in,
           lambda_q1, lambda_k1, lambda_q2, lambda_k2, subln_g, conv_w, conv_norm_g, w_out):
    bsz, seq, d = x.shape
    depth = w_ada.shape[0]
    cos, sinp, sinm = _rope_tables(seq)
    gsz = CONV_WIDTH // CONV_GROUPS
    grp = np.arange(CONV_WIDTH) // gsz
    gmat = jnp.asarray(np.where(grp[:, None] == grp[None, :], 1.0 / gsz, 0.0), dtype=BF16)
    modf3 = _ada(c, w_ada_final, b_ada_final)[0].reshape(bsz, 2, d)

    x2d = x.reshape(bsz * seq, d)
    for l in range(depth):
        lambda_init = 0.8 - 0.6 * math.exp(-0.3 * l)
        mod, w1a, w3a, w2a = _ada(c, w_ada[l], b_ada[l],
                                  side_casts=(ffn1_w1[l], ffn1_w3[l], ffn1_w2[l]))
        mod3 = mod.reshape(bsz, N_MOD, d)
        x2d, w_in_b, w_out_b, w1b, w3b, w2b = _ffn(
            x2d, mod3, g_ffn1[l], w1a, w3a, w2a, 0, seq,
            side_casts=(w_in[l], w_out[l], ffn2_w1[l], ffn2_w3[l], ffn2_w2[l]))
        w_vt = w_in_b[:, 2 * ATTN_WIDTH:3 * ATTN_WIDTH].T
        q, k, vt, y = _inproj(x2d, mod3, g_mix[l], w_in_b, w_vt, cos, sinp, sinm,
                              conv_w[l], conv_norm_g[l], gmat, seq)
        a = _attn(q, k, vt, lambda_q1[l], lambda_k1[l], lambda_q2[l], lambda_k2[l],
                  subln_g[l], lambda_init)
        final_args = (modf3, g_final) if l == depth - 1 else None
        x2d, = _ffn(x2d, mod3, g_ffn2[l], w1b, w3b, w2b, 6, seq, (a, y, w_out_b), final_args)
    return x2d.reshape(bsz, seq, d)
```

```python
import functools
import math

import jax
import jax.numpy as jnp
import numpy as np
from jax import lax
from jax.experimental import pallas as pl
from jax.experimental.pallas import tpu as pltpu

F32 = jnp.float32
BF16 = jnp.bfloat16

LANES = 128
SUBLANES = 8
BF16_ROWS = 16

D_MODEL = 1024
ATTN_HEADS = 4
HEAD_DIM = 64
ATTN_WIDTH = ATTN_HEADS * 2 * HEAD_DIM
CONV_WIDTH = D_MODEL - ATTN_WIDTH
CONV_GROUPS = 8
CONV_K = 3
ROPE_THETA = 10000.0
NORM_EPS = 1e-6
SUBLN_EPS = 1e-5
N_MOD = 9

FF_CHUNK = 256
ROW_TILE = 1024
ROW_SUBTILE = 512
ATTN_TILE = 512
ONES_ROWS = BF16_ROWS
VT_ROWS = 2 * HEAD_DIM + ONES_ROWS
HEADS_PER_STEP = 4
VMEM_LIMIT = 56 * 1024 * 1024
ATTN_VMEM_LIMIT = 60 * 1024 * 1024
ADA_BLOCKS = 8
NT_DIMS = (((1,), (1,)), ((), ()))
Q_SCALE = math.log2(math.e) / math.sqrt(HEAD_DIM)


def _silu(a):
    return a / (1.0 + jnp.exp(-a))


def _rms(x, eps):
    return x * lax.rsqrt(jnp.mean(x * x, axis=-1, keepdims=True) + eps)


def _ada_kernel(*refs, n_proj):
    n_side = (len(refs) - 1 - 3 * n_proj) // 2
    c_act = _silu(refs[0][...])
    proj_in = refs[1:1 + 2 * n_proj]
    side_in = refs[1 + 2 * n_proj:1 + 2 * n_proj + n_side]
    outs = refs[1 + 2 * n_proj + n_side:]
    for p in range(n_proj):
        w_ref, b_ref = proj_in[2 * p:2 * p + 2]
        outs[p][...] = jnp.dot(c_act, w_ref[...], preferred_element_type=F32) + b_ref[...]
    for f_ref, bf_ref in zip(side_in, outs[n_proj:]):
        bf_ref[...] = f_ref[...].astype(bf_ref.dtype)


def _ada(c, projections, side_casts=()):
    bsz, d = c.shape
    col = lambda j: (0, j)
    in_specs = [pl.BlockSpec((bsz, d), lambda j: (0, 0))]
    args = [c]
    out_specs, out_shape = [], []
    for w, b in projections:
        tn = w.shape[1] // ADA_BLOCKS
        in_specs += [pl.BlockSpec((d, tn), col), pl.BlockSpec((1, tn), col)]
        args += [w, b.reshape(1, -1)]
        out_specs.append(pl.BlockSpec((bsz, tn), col))
        out_shape.append(jax.ShapeDtypeStruct((bsz, w.shape[1]), F32))
    side_specs = [pl.BlockSpec((s.shape[0] // ADA_BLOCKS, s.shape[1]), lambda j: (j, 0))
                  for s in side_casts]
    return pl.pallas_call(
        functools.partial(_ada_kernel, n_proj=len(projections)),
        grid=(ADA_BLOCKS,),
        in_specs=in_specs + side_specs,
        out_specs=out_specs + side_specs,
        out_shape=out_shape + [jax.ShapeDtypeStruct(s.shape, BF16) for s in side_casts],
        compiler_params=pltpu.CompilerParams(vmem_limit_bytes=VMEM_LIMIT),
        name="ada",
    )(*args, *side_casts)


def _ffn_kernel(*refs, mod_row, mixer, final, n_side):
    refs = list(refs)
    x_ref, mod_ref, g_ref = refs[:3]
    del refs[:3]
    if mixer:
        a_ref, y_ref, wo_ref = refs[:3]
        del refs[:3]
    w1_ref, w3_ref, w2_ref = refs[:3]
    del refs[:3]
    if final:
        modf_ref, gf_ref = refs[:2]
        del refs[:2]
    side_in = refs[:n_side]
    del refs[:n_side]
    o_ref = refs[0]
    side_out = refs[1:1 + n_side]
    act_ref = refs[1 + n_side]
    for w_ref, wb_ref in zip(side_in, side_out):
        wb_ref[...] = w_ref[...].astype(wb_ref.dtype)
    shift = mod_ref[0, mod_row:mod_row + 1, :]
    scale = mod_ref[0, mod_row + 1:mod_row + 2, :]
    gate = mod_ref[0, mod_row + 2:mod_row + 3, :]
    n_chunks = w2_ref.shape[0] // FF_CHUNK
    sub_tiles = [slice(r * ROW_SUBTILE, (r + 1) * ROW_SUBTILE)
                 for r in range(x_ref.shape[0] // ROW_SUBTILE)]
    hs = []
    for rows in sub_tiles:
        x = x_ref[rows, :]
        if mixer:
            attn = jnp.concatenate([a_ref[0, hd, rows, :] for hd in range(a_ref.shape[1])],
                                   axis=-1)
            mix = (jnp.dot(attn, wo_ref[:ATTN_WIDTH, :], preferred_element_type=F32)
                   + jnp.dot(y_ref[rows, :], wo_ref[ATTN_WIDTH:, :], preferred_element_type=F32))
            x = x + mod_ref[0, mod_row - 1:mod_row, :] * mix
            o_ref[rows, :] = x
        hs.append((_rms(x, NORM_EPS) * g_ref[...] * (1.0 + scale) + shift).astype(BF16))
    for rows, h in zip(sub_tiles, hs):
        for c in range(n_chunks):
            cols = slice(c * FF_CHUNK, (c + 1) * FF_CHUNK)
            a = jnp.dot(h, w1_ref[:, cols], preferred_element_type=F32)
            b = jnp.dot(h, w3_ref[:, cols], preferred_element_type=F32)
            act_ref[rows, cols] = (_silu(a) * b).astype(BF16)
        x = o_ref[rows, :] if mixer else x_ref[rows, :]
        y = x + 0.5 * gate * jnp.dot(act_ref[rows, :], w2_ref[...],
                                     preferred_element_type=F32)
        if final:
            y = (_rms(y, NORM_EPS) * gf_ref[...] * (1.0 + modf_ref[0, 1:2, :])
                 + modf_ref[0, 0:1, :])
        o_ref[rows, :] = y


def _ffn(x2d, mod3, g, w1, w3, w2, mod_row, rows_per_batch, mixer_args=None, final_args=None,
         side_casts=()):
    m, d = x2d.shape
    tm = ROW_TILE
    tiles_per_batch = rows_per_batch // tm
    n_steps = m // tm
    row = lambda i: (i, 0)
    fix = lambda i: (0, 0)
    per_batch = lambda i: (i // tiles_per_batch, 0, 0)
    resident = lambda w: pl.BlockSpec(w.shape, fix, pipeline_mode=pl.Buffered(1))
    in_specs = [pl.BlockSpec((tm, d), row),
                pl.BlockSpec((1, N_MOD, d), per_batch),
                pl.BlockSpec((1, d), fix)]
    args = [x2d, mod3, g.reshape(1, d)]
    if mixer_args is not None:
        a4d, y2d, w_out = mixer_args
        in_specs += [pl.BlockSpec((1, a4d.shape[1], tm, a4d.shape[3]),
                                  lambda i: (i // tiles_per_batch, 0, i % tiles_per_batch, 0)),
                     pl.BlockSpec((tm, y2d.shape[1]), row), resident(w_out)]
        args += [a4d, y2d, w_out]
    in_specs += [resident(w1), resident(w3), resident(w2)]
    args += [w1, w3, w2]
    if final_args is not None:
        modf3, gf = final_args
        in_specs += [pl.BlockSpec((1, 2, d), per_batch), pl.BlockSpec((1, d), fix)]
        args += [modf3, gf.reshape(1, d)]
    side_specs = []
    for w in side_casts:
        reps = 1
        while (w.shape[0] * reps // n_steps) % BF16_ROWS:
            reps *= 2
        side_specs.append(pl.BlockSpec((w.shape[0] * reps // n_steps, w.shape[1]),
                                       functools.partial(lambda i, r: (i // r, 0), r=reps)))
    in_specs += side_specs
    args += list(side_casts)
    return pl.pallas_call(
        functools.partial(_ffn_kernel, mod_row=mod_row, mixer=mixer_args is not None,
                          final=final_args is not None, n_side=len(side_casts)),
        grid=(n_steps,),
        in_specs=in_specs,
        out_specs=[pl.BlockSpec((tm, d), row)] + side_specs,
        out_shape=[jax.ShapeDtypeStruct((m, d), F32)]
        + [jax.ShapeDtypeStruct(w.shape, BF16) for w in side_casts],
        scratch_shapes=[pltpu.VMEM((tm, w2.shape[0]), BF16)],
        compiler_params=pltpu.CompilerParams(
            dimension_semantics=("arbitrary",), vmem_limit_bytes=VMEM_LIMIT),
        name="ffn" + ("_mixer" if mixer_args is not None else "")
        + ("_final" if final_args is not None else ""),
    )(*args)


def _inproj_kernel(x_ref, mod_ref, g_ref, w_ref, wvt_ref, cos_ref, sinp_ref, sinm_ref,
                   convw_ref, cng_ref, gmat_ref,
                   q_ref, k_ref, vt_ref, y_ref, halo_ref, *, mod_row, tiles_per_batch):
    shift = mod_ref[0, mod_row:mod_row + 1, :]
    scale = mod_ref[0, mod_row + 1:mod_row + 2, :]
    half = HEAD_DIM // 2
    w = convw_ref[...]

    @pl.when(pl.program_id(0) % tiles_per_batch == 0)
    def _():
        halo_ref[...] = jnp.zeros_like(halo_ref)

    prev = halo_ref[...]
    for r in range(x_ref.shape[0] // ROW_SUBTILE):
        rows = slice(r * ROW_SUBTILE, (r + 1) * ROW_SUBTILE)
        h = (_rms(x_ref[rows, :], NORM_EPS) * g_ref[...] * (1.0 + scale) + shift).astype(BF16)
        zqk = jnp.dot(h, w_ref[:, :2 * ATTN_WIDTH], preferred_element_type=F32)
        zcv = jnp.dot(h, w_ref[:, 3 * ATTN_WIDTH:], preferred_element_type=F32)
        vt = lax.dot_general(wvt_ref[...], h, NT_DIMS, preferred_element_type=F32).astype(BF16)
        for j in range(ATTN_HEADS):
            vt_ref[0, j * VT_ROWS:j * VT_ROWS + LANES, rows] = vt[j * LANES:(j + 1) * LANES, :]
            vt_ref[0, j * VT_ROWS + LANES:(j + 1) * VT_ROWS, rows] = jnp.ones(
                (ONES_ROWS, ROW_SUBTILE), BF16)
        cos = cos_ref[rows, :]
        sinp = sinp_ref[rows, :]
        sinm = sinm_ref[rows, :]
        for j in range(ATTN_WIDTH // LANES):
            for base, out_ref, mult in ((0, q_ref, Q_SCALE), (ATTN_WIDTH, k_ref, 1.0)):
                t = zqk[:, base + j * LANES: base + (j + 1) * LANES]
                rot = (t * cos + pltpu.roll(t, half, axis=1) * sinp
                       + pltpu.roll(t, LANES - half, axis=1) * sinm)
                out_ref[0, j, rows, :] = (rot * mult).astype(BF16)

        gate_b = zcv[:, :CONV_WIDTH]
        cu = zcv[:, CONV_WIDTH:2 * CONV_WIDTH] * zcv[:, 2 * CONV_WIDTH:]
        buf = jnp.concatenate([prev, cu], axis=0)
        cu_m1 = pltpu.roll(buf, 1, axis=0)[SUBLANES:, :]
        cu_m2 = pltpu.roll(buf, 2, axis=0)[SUBLANES:, :]
        prev = cu[ROW_SUBTILE - SUBLANES:, :]
        y = gate_b * (w[0:1, :] * cu_m2 + w[1:2, :] * cu_m1 + w[2:3, :] * cu)
        y2 = y * y
        y2_hi = y2.astype(BF16)
        y2_lo = (y2 - y2_hi.astype(F32)).astype(BF16)
        ms = (jnp.dot(y2_hi, gmat_ref[...], preferred_element_type=F32)
              + jnp.dot(y2_lo, gmat_ref[...], preferred_element_type=F32))
        y_ref[rows, :] = (y * lax.rsqrt(ms + NORM_EPS) * cng_ref[...]).astype(BF16)
    halo_ref[...] = prev


def _inproj(x2d, mod3, g, w_in, w_vt, cos, sinp, sinm, conv_w, conv_norm_g, gmat,
            rows_per_batch):
    m, d = x2d.shape
    tm = ROW_TILE
    tiles_per_batch = rows_per_batch // tm
    row = lambda i: (i, 0)
    fix = lambda i: (0, 0)
    pos = lambda i: (i % tiles_per_batch, 0)
    n_batch = m // rows_per_batch
    head_major = pl.BlockSpec((1, ATTN_HEADS, tm, LANES),
                              lambda i: (i // tiles_per_batch, 0, i % tiles_per_batch, 0))
    out_sd = jax.ShapeDtypeStruct((n_batch, ATTN_HEADS, rows_per_batch, LANES), BF16)
    return pl.pallas_call(
        functools.partial(_inproj_kernel, mod_row=3, tiles_per_batch=tiles_per_batch),
        grid=(m // tm,),
        in_specs=[pl.BlockSpec((tm, d), row),
                  pl.BlockSpec((1, N_MOD, d), lambda i: (i // tiles_per_batch, 0, 0)),
                  pl.BlockSpec((1, d), fix),
                  pl.BlockSpec(w_in.shape, fix, pipeline_mode=pl.Buffered(1)),
                  pl.BlockSpec(w_vt.shape, fix, pipeline_mode=pl.Buffered(1)),
                  pl.BlockSpec((tm, LANES), pos),
                  pl.BlockSpec((tm, LANES), pos),
                  pl.BlockSpec((tm, LANES), pos),
                  pl.BlockSpec((CONV_K, CONV_WIDTH), fix),
                  pl.BlockSpec((1, CONV_WIDTH), fix),
                  pl.BlockSpec((CONV_WIDTH, CONV_WIDTH), fix)],
        out_specs=[head_major, head_major,
                   pl.BlockSpec((1, ATTN_HEADS * VT_ROWS, tm),
                                lambda i: (i // tiles_per_batch, 0, i % tiles_per_batch)),
                   pl.BlockSpec((tm, CONV_WIDTH), row)],
        out_shape=[out_sd, out_sd,
                   jax.ShapeDtypeStruct((n_batch, ATTN_HEADS * VT_ROWS, rows_per_batch), BF16),
                   jax.ShapeDtypeStruct((m, CONV_WIDTH), BF16)],
        scratch_shapes=[pltpu.VMEM((SUBLANES, CONV_WIDTH), F32)],
        compiler_params=pltpu.CompilerParams(
            dimension_semantics=("arbitrary",), vmem_limit_bytes=VMEM_LIMIT),
        name="inproj",
    )(x2d, mod3, g.reshape(1, d), w_in, w_vt, cos, sinp, sinm, conv_w,
      conv_norm_g.reshape(1, CONV_WIDTH), gmat)


def _attn_kernel(q_ref, k_ref, vta_ref, lq1_ref, lk1_ref, lq2_ref, lk2_ref, sg_ref,
                 o_ref, sa_ref, sb_ref, cma_ref, cmb_ref, m_ref, acc_ref, *, lambda_init):
    seq = q_ref.shape[2]
    t = ATTN_TILE
    dv = 2 * HEAD_DIM
    n_tiles = seq // t
    heads = range(HEADS_PER_STEP)
    lam = (jnp.exp(jnp.sum(lq1_ref[...] * lk1_ref[...], axis=-1, keepdims=True))
           - jnp.exp(jnp.sum(lq2_ref[...] * lk2_ref[...], axis=-1, keepdims=True))
           + lambda_init)

    def stacked_queries(hd, i):
        q = q_ref[0, hd, pl.ds(pl.multiple_of(i * t, t), t), :]
        first_map = lax.broadcasted_iota(jnp.int32, (t, LANES), 1) < HEAD_DIM
        zero = jnp.zeros_like(q)
        return jnp.concatenate([jnp.where(first_map, q, zero),
                                jnp.where(first_map, zero, q)], axis=0)

    def scores_into(s_buf, cm_buf, qq, k0):
        for hd in heads:
            s = lax.dot_general(k_ref[0, hd, pl.ds(k0, t), :], qq[hd], NT_DIMS,
                                preferred_element_type=F32)
            s_buf[hd] = s
            cm_buf[hd] = jnp.max(s, axis=0, keepdims=True)

    def absorb(s_buf, cm_buf, k0, mask=False):
        for hd in heads:
            s = s_buf[hd]
            if mask:
                kpos = lax.broadcasted_iota(jnp.int32, (t, 2 * t), 0)
                qpos = lax.broadcasted_iota(jnp.int32, (t, 2 * t), 1)
                qpos = jnp.where(qpos >= t, qpos - t, qpos)
                s = jnp.where(kpos <= qpos, s, -jnp.inf)
                cm = jnp.max(s, axis=0, keepdims=True)
            else:
                cm = cm_buf[hd]
            m = m_ref[hd]
            m_new = jnp.maximum(m, cm)
            alpha = jnp.exp2(m - m_new)
            p = jnp.exp2(s - m_new).astype(BF16)
            m_ref[hd] = m_new
            pv = jnp.dot(vta_ref[0, hd * VT_ROWS:(hd + 1) * VT_ROWS, pl.ds(k0, t)], p,
                         preferred_element_type=F32)
            acc_ref[hd] = alpha * acc_ref[hd] + pv

    def q_tile(i, odd):
        q0 = pl.multiple_of(i * t, t)
        nxt = jnp.minimum(i + 1, n_tiles - 1)
        qq = [stacked_queries(hd, i) for hd in heads]
        qq_next = [stacked_queries(hd, nxt) for hd in heads]
        m_ref[...] = jnp.full(m_ref.shape, -jnp.inf, F32)
        acc_ref[...] = jnp.zeros_like(acc_ref)
        if not odd:
            scores_into(sb_ref, cmb_ref, [jnp.where(i == 0, b, a) for a, b in zip(qq, qq_next)], 0)
        absorb(sa_ref, cma_ref, q0, mask=True)

        def pair(jj, carry):
            k_even = pl.multiple_of(2 * jj * t, t)
            k_odd = pl.multiple_of(k_even + t, t)
            scores_into(sa_ref, cma_ref, qq, k_odd)
            absorb(sb_ref, cmb_ref, k_even)
            if odd:
                scores_into(sb_ref, cmb_ref, qq, pl.multiple_of(k_odd + t, t))
            else:
                done = 2 * jj + 2 >= i
                scores_into(sb_ref, cmb_ref,
                            [jnp.where(done, b, a) for a, b in zip(qq, qq_next)],
                            pl.multiple_of(jnp.where(done, 0, 2 * jj + 2) * t, t))
            absorb(sa_ref, cma_ref, k_odd)
            return carry

        lax.fori_loop(0, i // 2, pair, 0)
        scores_into(sa_ref, cma_ref, qq_next, pl.multiple_of(nxt * t, t))
        if odd:
            absorb(sb_ref, cmb_ref, pl.multiple_of((i - 1) * t, t))
        for hd in heads:
            acc = acc_ref[hd]
            o = acc[:dv, :] / acc[dv:dv + 1, :]
            o = o[:, :t] - lam * o[:, t:]
            o = o * lax.rsqrt(jnp.mean(o * o, axis=0, keepdims=True) + SUBLN_EPS)
            o = o * (sg_ref[...] * (1.0 - lambda_init))
            o_ref[0, hd, pl.ds(q0, t), :] = o.T.astype(BF16)

    def q_tile_pair(a, carry):
        q_tile(2 * a, odd=False)
        q_tile(2 * a + 1, odd=True)
        return carry

    scores_into(sa_ref, cma_ref, [stacked_queries(hd, 0) for hd in heads], 0)
    lax.fori_loop(0, n_tiles // 2, q_tile_pair, 0)


def _attn(q, k, vta, lq1, lk1, lq2, lk2, subln_g, lambda_init):
    bsz, n_heads, seq, _ = q.shape
    t = ATTN_TILE
    nh = HEADS_PER_STEP
    assert (seq // t) % 2 == 0 and n_heads % nh == 0 and vta.shape[1] == n_heads * VT_ROWS
    blk = pl.BlockSpec((1, nh, seq, LANES), lambda b, h: (b, h, 0, 0))
    vec = pl.BlockSpec((1, HEAD_DIM), lambda b, h: (0, 0))
    return pl.pallas_call(
        functools.partial(_attn_kernel, lambda_init=lambda_init),
        grid=(bsz, n_heads // nh),
        in_specs=[blk, blk,
                  pl.BlockSpec((1, nh * VT_ROWS, seq), lambda b, h: (b, h, 0)),
                  vec, vec, vec, vec,
                  pl.BlockSpec((2 * HEAD_DIM, 1), lambda b, h: (0, 0))],
        out_specs=blk,
        out_shape=jax.ShapeDtypeStruct((bsz, n_heads, seq, LANES), BF16),
        scratch_shapes=[pltpu.VMEM((nh, t, 2 * t), F32),
                        pltpu.VMEM((nh, t, 2 * t), F32),
                        pltpu.VMEM((nh, 1, 2 * t), F32),
                        pltpu.VMEM((nh, 1, 2 * t), F32),
                        pltpu.VMEM((nh, 1, 2 * t), F32),
                        pltpu.VMEM((nh, VT_ROWS, 2 * t), F32)],
        compiler_params=pltpu.CompilerParams(
            dimension_semantics=("arbitrary", "arbitrary"),
            vmem_limit_bytes=ATTN_VMEM_LIMIT),
        name="attn",
    )(q, k, vta, lq1.reshape(1, -1), lk1.reshape(1, -1), lq2.reshape(1, -1),
      lk2.reshape(1, -1), subln_g.reshape(-1, 1))


def _rope_tables(seq):
    inv = 1.0 / (ROPE_THETA ** (np.arange(0, HEAD_DIM, 2, dtype=np.float64) / HEAD_DIM))
    ang = np.arange(seq, dtype=np.float64)[:, None] * inv[None, :]
    ang = np.concatenate([ang, ang, ang, ang], axis=-1)
    cos, sin = np.cos(ang), np.sin(ang)
    upper = (np.arange(LANES) % HEAD_DIM) >= HEAD_DIM // 2
    as_f32 = lambda a: jnp.asarray(a.astype(np.float32))
    return as_f32(cos), as_f32(np.where(upper, sin, 0.0)), as_f32(np.where(upper, 0.0, -sin))


def kernel(x, c, w_ada, b_ada, w_ada_final, b_ada_final, g_ffn1, g_mix, g_ffn2, g_final,
           ffn1_w1, ffn1_w3, ffn1_w2, ffn2_w1, ffn2_w3, ffn2_w2, w_in,
           lambda_q1, lambda_k1, lambda_q2, lambda_k2, subln_g, conv_w, conv_norm_g, w_out):
    bsz, seq, d = x.shape
    depth = w_ada.shape[0]
    cos, sinp, sinm = _rope_tables(seq)
    gsz = CONV_WIDTH // CONV_GROUPS
    grp = np.arange(CONV_WIDTH) // gsz
    gmat = jnp.asarray(np.where(grp[:, None] == grp[None, :], 1.0 / gsz, 0.0), dtype=BF16)

    x2d = x.reshape(bsz * seq, d)
    for l in range(depth):
        lambda_init = 0.8 - 0.6 * math.exp(-0.3 * l)
        projections = [(w_ada[l], b_ada[l])] + ([(w_ada_final, b_ada_final)] if l == 0 else [])
        mods = _ada(c, projections, side_casts=(ffn1_w1[l], ffn1_w3[l], ffn1_w2[l]))
        w1a, w3a, w2a = mods[len(projections):]
        mod3 = mods[0].reshape(bsz, N_MOD, d)
        if l == 0:
            modf3 = mods[1].reshape(bsz, 2, d)
        x2d, w_in_b, w_out_b, w1b, w3b, w2b = _ffn(
            x2d, mod3, g_ffn1[l], w1a, w3a, w2a, 0, seq,
            side_casts=(w_in[l], w_out[l], ffn2_w1[l], ffn2_w3[l], ffn2_w2[l]))
        w_vt = w_in_b[:, 2 * ATTN_WIDTH:3 * ATTN_WIDTH].T
        q, k, vt, y = _inproj(x2d, mod3, g_mix[l], w_in_b, w_vt, cos, sinp, sinm,
                              conv_w[l], conv_norm_g[l], gmat, seq)
        a = _attn(q, k, vt, lambda_q1[l], lambda_k1[l], lambda_q2[l], lambda_k2[l],
                  subln_g[l], lambda_init)
        final_args = (modf3, g_final) if l == depth - 1 else None
        x2d, = _ffn(x2d, mod3, g_ffn2[l], w1b, w3b, w2b, 6, seq, (a, y, w_out_b), final_args)
    return x2d.reshape(bsz, seq, d)
```

```python
import functools
import math

import jax
import jax.numpy as jnp
import numpy as np
from jax import lax
from jax.experimental import pallas as pl
from jax.experimental.pallas import tpu as pltpu

F32 = jnp.float32
BF16 = jnp.bfloat16

LANES = 128
SUBLANES = 8
BF16_ROWS = 16

D_MODEL = 1024
ATTN_HEADS = 4
HEAD_DIM = 64
ATTN_WIDTH = ATTN_HEADS * 2 * HEAD_DIM
CONV_WIDTH = D_MODEL - ATTN_WIDTH
CONV_GROUPS = 8
CONV_K = 3
ROPE_THETA = 10000.0
NORM_EPS = 1e-6
SUBLN_EPS = 1e-5
N_MOD = 9

FF_CHUNK = 256
ROW_TILE = 1024
ROW_SUBTILE = 512
ATTN_TILE = 512
ONES_ROWS = BF16_ROWS
VT_ROWS = 2 * HEAD_DIM + ONES_ROWS
HEADS_PER_STEP = 4
VMEM_LIMIT = 56 * 1024 * 1024
ATTN_VMEM_LIMIT = 60 * 1024 * 1024
ADA_BLOCKS = 8
NT_DIMS = (((1,), (1,)), ((), ()))
Q_SCALE = math.log2(math.e) / math.sqrt(HEAD_DIM)


def _silu(a):
    return a / (1.0 + jnp.exp(-a))


def _rms(x, eps):
    return x * lax.rsqrt(jnp.mean(x * x, axis=-1, keepdims=True) + eps)


def _ada_kernel(*refs, n_proj):
    n_side = (len(refs) - 1 - 3 * n_proj) // 2
    c_act = _silu(refs[0][...])
    proj_in = refs[1:1 + 2 * n_proj]
    side_in = refs[1 + 2 * n_proj:1 + 2 * n_proj + n_side]
    outs = refs[1 + 2 * n_proj + n_side:]
    for p in range(n_proj):
        w_ref, b_ref = proj_in[2 * p:2 * p + 2]
        outs[p][...] = jnp.dot(c_act, w_ref[...], preferred_element_type=F32) + b_ref[...]
    for f_ref, bf_ref in zip(side_in, outs[n_proj:]):
        bf_ref[...] = f_ref[...].astype(bf_ref.dtype)


def _ada(c, projections, side_casts=()):
    bsz, d = c.shape
    col = lambda j: (0, j)
    in_specs = [pl.BlockSpec((bsz, d), lambda j: (0, 0))]
    args = [c]
    out_specs, out_shape = [], []
    for w, b in projections:
        tn = w.shape[1] // ADA_BLOCKS
        in_specs += [pl.BlockSpec((d, tn), col), pl.BlockSpec((1, tn), col)]
        args += [w, b.reshape(1, -1)]
        out_specs.append(pl.BlockSpec((bsz, tn), col))
        out_shape.append(jax.ShapeDtypeStruct((bsz, w.shape[1]), F32))
    side_specs = [pl.BlockSpec((s.shape[0] // ADA_BLOCKS, s.shape[1]), lambda j: (j, 0))
                  for s in side_casts]
    return pl.pallas_call(
        functools.partial(_ada_kernel, n_proj=len(projections)),
        grid=(ADA_BLOCKS,),
        in_specs=in_specs + side_specs,
        out_specs=out_specs + side_specs,
        out_shape=out_shape + [jax.ShapeDtypeStruct(s.shape, BF16) for s in side_casts],
        compiler_params=pltpu.CompilerParams(vmem_limit_bytes=VMEM_LIMIT),
        name="ada",
    )(*args, *side_casts)


def _ffn_kernel(*refs, mod_row, mixer, final, n_side):
    refs = list(refs)
    x_ref, mod_ref, g_ref = refs[:3]
    del refs[:3]
    if mixer:
        a_ref, y_ref, wo_ref = refs[:3]
        del refs[:3]
    w1_ref, w3_ref, w2_ref = refs[:3]
    del refs[:3]
    if final:
        modf_ref, gf_ref = refs[:2]
        del refs[:2]
    side_in = refs[:n_side]
    del refs[:n_side]
    o_ref = refs[0]
    side_out = refs[1:1 + n_side]
    act_ref = refs[1 + n_side]
    for w_ref, wb_ref in zip(side_in, side_out):
        wb_ref[...] = w_ref[...].astype(wb_ref.dtype)
    shift = mod_ref[0, mod_row:mod_row + 1, :]
    scale = mod_ref[0, mod_row + 1:mod_row + 2, :]
    gate = mod_ref[0, mod_row + 2:mod_row + 3, :]
    n_chunks = w2_ref.shape[0] // FF_CHUNK
    sub_tiles = [slice(r * ROW_SUBTILE, (r + 1) * ROW_SUBTILE)
                 for r in range(x_ref.shape[0] // ROW_SUBTILE)]
    hs = []
    for rows in sub_tiles:
        x = x_ref[rows, :]
        if mixer:
            attn = jnp.concatenate([a_ref[0, hd, rows, :] for hd in range(a_ref.shape[1])],
                                   axis=-1)
            mix = (jnp.dot(attn, wo_ref[:ATTN_WIDTH, :], preferred_element_type=F32)
                   + jnp.dot(y_ref[rows, :], wo_ref[ATTN_WIDTH:, :], preferred_element_type=F32))
            x = x + mod_ref[0, mod_row - 1:mod_row, :] * mix
            o_ref[rows, :] = x
        hs.append((_rms(x, NORM_EPS) * g_ref[...] * (1.0 + scale) + shift).astype(BF16))
    for rows, h in zip(sub_tiles, hs):
        for c in range(n_chunks):
            cols = slice(c * FF_CHUNK, (c + 1) * FF_CHUNK)
            a = jnp.dot(h, w1_ref[:, cols], preferred_element_type=F32)
            b = jnp.dot(h, w3_ref[:, cols], preferred_element_type=F32)
            act_ref[rows, cols] = (_silu(a) * b).astype(BF16)
        x = o_ref[rows, :] if mixer else x_ref[rows, :]
        y = x + 0.5 * gate * jnp.dot(act_ref[rows, :], w2_ref[...],
                                     preferred_element_type=F32)
        if final:
            y = (_rms(y, NORM_EPS) * gf_ref[...] * (1.0 + modf_ref[0, 1:2, :])
                 + modf_ref[0, 0:1, :])
        o_ref[rows, :] = y


def _ffn(x2d, mod3, g, w1, w3, w2, mod_row, rows_per_batch, mixer_args=None, final_args=None,
         side_casts=()):
    m, d = x2d.shape
    tm = ROW_TILE
    tiles_per_batch = rows_per_batch // tm
    n_steps = m // tm
    row = lambda i: (i, 0)
    fix = lambda i: (0, 0)
    per_batch = lambda i: (i // tiles_per_batch, 0, 0)
    resident = lambda w: pl.BlockSpec(w.shape, fix, pipeline_mode=pl.Buffered(1))
    in_specs = [pl.BlockSpec((tm, d), row),
                pl.BlockSpec((1, N_MOD, d), per_batch),
                pl.BlockSpec((1, d), fix)]
    args = [x2d, mod3, g.reshape(1, d)]
    if mixer_args is not None:
        a4d, y2d, w_out = mixer_args
        in_specs += [pl.BlockSpec((1, a4d.shape[1], tm, a4d.shape[3]),
                                  lambda i: (i // tiles_per_batch, 0, i % tiles_per_batch, 0)),
                     pl.BlockSpec((tm, y2d.shape[1]), row), resident(w_out)]
        args += [a4d, y2d, w_out]
    in_specs += [resident(w1), resident(w3), resident(w2)]
    args += [w1, w3, w2]
    if final_args is not None:
        modf3, gf = final_args
        in_specs += [pl.BlockSpec((1, 2, d), per_batch), pl.BlockSpec((1, d), fix)]
        args += [modf3, gf.reshape(1, d)]
    side_specs = []
    for w in side_casts:
        reps = 1
        while (w.shape[0] * reps // n_steps) % BF16_ROWS:
            reps *= 2
        side_specs.append(pl.BlockSpec((w.shape[0] * reps // n_steps, w.shape[1]),
                                       functools.partial(lambda i, r: (i // r, 0), r=reps)))
    in_specs += side_specs
    args += list(side_casts)
    return pl.pallas_call(
        functools.partial(_ffn_kernel, mod_row=mod_row, mixer=mixer_args is not None,
                          final=final_args is not None, n_side=len(side_casts)),
        grid=(n_steps,),
        in_specs=in_specs,
        out_specs=[pl.BlockSpec((tm, d), row)] + side_specs,
        out_shape=[jax.ShapeDtypeStruct((m, d), F32)]
        + [jax.ShapeDtypeStruct(w.shape, BF16) for w in side_casts],
        scratch_shapes=[pltpu.VMEM((tm, w2.shape[0]), BF16)],
        compiler_params=pltpu.CompilerParams(
            dimension_semantics=("arbitrary",), vmem_limit_bytes=VMEM_LIMIT),
        name="ffn" + ("_mixer" if mixer_args is not None else "")
        + ("_final" if final_args is not None else ""),
    )(*args)


def _inproj_kernel(x_ref, mod_ref, g_ref, w_ref, wvt_ref, cos_ref, sinp_ref, sinm_ref,
                   convw_ref, cng_ref, gmat_ref,
                   vt_ref, q_ref, k_ref, y_ref, halo_ref, *, mod_row, tiles_per_batch):
    shift = mod_ref[0, mod_row:mod_row + 1, :]
    scale = mod_ref[0, mod_row + 1:mod_row + 2, :]
    half = HEAD_DIM // 2
    w = convw_ref[...]

    @pl.when(pl.program_id(0) % tiles_per_batch == 0)
    def _():
        halo_ref[...] = jnp.zeros_like(halo_ref)

    prev = halo_ref[...]
    for r in range(x_ref.shape[0] // ROW_SUBTILE):
        rows = slice(r * ROW_SUBTILE, (r + 1) * ROW_SUBTILE)
        h = (_rms(x_ref[rows, :], NORM_EPS) * g_ref[...] * (1.0 + scale) + shift).astype(BF16)
        zqk = jnp.dot(h, w_ref[:, :2 * ATTN_WIDTH], preferred_element_type=F32)
        zcv = jnp.dot(h, w_ref[:, 3 * ATTN_WIDTH:], preferred_element_type=F32)
        vt = lax.dot_general(wvt_ref[...], h, NT_DIMS, preferred_element_type=F32).astype(BF16)
        for j in range(ATTN_HEADS):
            vt_ref[0, j * VT_ROWS:j * VT_ROWS + LANES, rows] = vt[j * LANES:(j + 1) * LANES, :]
            vt_ref[0, j * VT_ROWS + LANES:(j + 1) * VT_ROWS, rows] = jnp.ones(
                (ONES_ROWS, ROW_SUBTILE), BF16)
        cos = cos_ref[rows, :]
        sinp = sinp_ref[rows, :]
        sinm = sinm_ref[rows, :]
        for j in range(ATTN_WIDTH // LANES):
            for base, out_ref, mult in ((0, q_ref, Q_SCALE), (ATTN_WIDTH, k_ref, 1.0)):
                t = zqk[:, base + j * LANES: base + (j + 1) * LANES]
                rot = (t * cos + pltpu.roll(t, half, axis=1) * sinp
                       + pltpu.roll(t, LANES - half, axis=1) * sinm)
                out_ref[0, j, rows, :] = (rot * mult).astype(BF16)

        gate_b = zcv[:, :CONV_WIDTH]
        cu = zcv[:, CONV_WIDTH:2 * CONV_WIDTH] * zcv[:, 2 * CONV_WIDTH:]
        buf = jnp.concatenate([prev, cu], axis=0)
        cu_m1 = pltpu.roll(buf, 1, axis=0)[SUBLANES:, :]
        cu_m2 = pltpu.roll(buf, 2, axis=0)[SUBLANES:, :]
        prev = cu[ROW_SUBTILE - SUBLANES:, :]
        y = gate_b * (w[0:1, :] * cu_m2 + w[1:2, :] * cu_m1 + w[2:3, :] * cu)
        y2 = y * y
        y2_hi = y2.astype(BF16)
        y2_lo = (y2 - y2_hi.astype(F32)).astype(BF16)
        ms = (jnp.dot(y2_hi, gmat_ref[...], preferred_element_type=F32)
              + jnp.dot(y2_lo, gmat_ref[...], preferred_element_type=F32))
        y_ref[rows, :] = (y * lax.rsqrt(ms + NORM_EPS) * cng_ref[...]).astype(BF16)
    halo_ref[...] = prev


def _inproj(x2d, mod3, g, w_in, w_vt, cos, sinp, sinm, conv_w, conv_norm_g, gmat,
            rows_per_batch):
    m, d = x2d.shape
    tm = ROW_TILE
    tiles_per_batch = rows_per_batch // tm
    row = lambda i: (i, 0)
    fix = lambda i: (0, 0)
    pos = lambda i: (i % tiles_per_batch, 0)
    n_batch = m // rows_per_batch
    head_major = pl.BlockSpec((1, ATTN_HEADS, tm, LANES),
                              lambda i: (i // tiles_per_batch, 0, i % tiles_per_batch, 0))
    out_sd = jax.ShapeDtypeStruct((n_batch, ATTN_HEADS, rows_per_batch, LANES), BF16)
    return pl.pallas_call(
        functools.partial(_inproj_kernel, mod_row=3, tiles_per_batch=tiles_per_batch),
        grid=(m // tm,),
        in_specs=[pl.BlockSpec((tm, d), row),
                  pl.BlockSpec((1, N_MOD, d), lambda i: (i // tiles_per_batch, 0, 0)),
                  pl.BlockSpec((1, d), fix),
                  pl.BlockSpec(w_in.shape, fix, pipeline_mode=pl.Buffered(1)),
                  pl.BlockSpec(w_vt.shape, fix, pipeline_mode=pl.Buffered(1)),
                  pl.BlockSpec((tm, LANES), pos),
                  pl.BlockSpec((tm, LANES), pos),
                  pl.BlockSpec((tm, LANES), pos),
                  pl.BlockSpec((CONV_K, CONV_WIDTH), fix),
                  pl.BlockSpec((1, CONV_WIDTH), fix),
                  pl.BlockSpec((CONV_WIDTH, CONV_WIDTH), fix)],
        out_specs=[pl.BlockSpec((1, ATTN_HEADS * VT_ROWS, tm),
                                lambda i: (i // tiles_per_batch, 0, i % tiles_per_batch)),
                   head_major, head_major,
                   pl.BlockSpec((tm, CONV_WIDTH), row)],
        out_shape=[jax.ShapeDtypeStruct((n_batch, ATTN_HEADS * VT_ROWS, rows_per_batch), BF16),
                   out_sd, out_sd,
                   jax.ShapeDtypeStruct((m, CONV_WIDTH), BF16)],
        scratch_shapes=[pltpu.VMEM((SUBLANES, CONV_WIDTH), F32)],
        compiler_params=pltpu.CompilerParams(
            dimension_semantics=("arbitrary",), vmem_limit_bytes=VMEM_LIMIT),
        name="inproj",
    )(x2d, mod3, g.reshape(1, d), w_in, w_vt, cos, sinp, sinm, conv_w,
      conv_norm_g.reshape(1, CONV_WIDTH), gmat)


def _attn_kernel(q_ref, k_ref, vta_ref, lq1_ref, lk1_ref, lq2_ref, lk2_ref, sg_ref,
                 o_ref, sa_ref, sb_ref, cma_ref, cmb_ref, m_ref, acc_ref, *, lambda_init):
    seq = q_ref.shape[2]
    t = ATTN_TILE
    dv = 2 * HEAD_DIM
    n_tiles = seq // t
    heads = range(HEADS_PER_STEP)
    lam = (jnp.exp(jnp.sum(lq1_ref[...] * lk1_ref[...], axis=-1, keepdims=True))
           - jnp.exp(jnp.sum(lq2_ref[...] * lk2_ref[...], axis=-1, keepdims=True))
           + lambda_init)

    def stacked_queries(hd, i):
        q = q_ref[0, hd, pl.ds(pl.multiple_of(i * t, t), t), :]
        first_map = lax.broadcasted_iota(jnp.int32, (t, LANES), 1) < HEAD_DIM
        zero = jnp.zeros_like(q)
        return jnp.concatenate([jnp.where(first_map, q, zero),
                                jnp.where(first_map, zero, q)], axis=0)

    def scores_into(s_buf, cm_buf, qq, k0):
        for hd in heads:
            s = lax.dot_general(k_ref[0, hd, pl.ds(k0, t), :], qq[hd], NT_DIMS,
                                preferred_element_type=F32)
            s_buf[hd] = s
            cm_buf[hd] = jnp.max(s, axis=0, keepdims=True)

    def absorb(s_buf, cm_buf, k0, mask=False):
        for hd in heads:
            s = s_buf[hd]
            if mask:
                kpos = lax.broadcasted_iota(jnp.int32, (t, 2 * t), 0)
                qpos = lax.broadcasted_iota(jnp.int32, (t, 2 * t), 1)
                qpos = jnp.where(qpos >= t, qpos - t, qpos)
                s = jnp.where(kpos <= qpos, s, -jnp.inf)
                cm = jnp.max(s, axis=0, keepdims=True)
            else:
                cm = cm_buf[hd]
            m = m_ref[hd]
            m_new = jnp.maximum(m, cm)
            alpha = jnp.exp2(m - m_new)
            p = jnp.exp2(s - m_new).astype(BF16)
            m_ref[hd] = m_new
            pv = jnp.dot(vta_ref[0, hd * VT_ROWS:(hd + 1) * VT_ROWS, pl.ds(k0, t)], p,
                         preferred_element_type=F32)
            acc_ref[hd] = alpha * acc_ref[hd] + pv

    def q_tile(i, odd):
        q0 = pl.multiple_of(i * t, t)
        nxt = jnp.minimum(i + 1, n_tiles - 1)
        qq = [stacked_queries(hd, i) for hd in heads]
        qq_next = [stacked_queries(hd, nxt) for hd in heads]
        m_ref[...] = jnp.full(m_ref.shape, -jnp.inf, F32)
        acc_ref[...] = jnp.zeros_like(acc_ref)
        if not odd:
            scores_into(sb_ref, cmb_ref, [jnp.where(i == 0, b, a) for a, b in zip(qq, qq_next)], 0)
        absorb(sa_ref, cma_ref, q0, mask=True)

        def pair(jj, carry):
            k_even = pl.multiple_of(2 * jj * t, t)
            k_odd = pl.multiple_of(k_even + t, t)
            scores_into(sa_ref, cma_ref, qq, k_odd)
            absorb(sb_ref, cmb_ref, k_even)
            if odd:
                scores_into(sb_ref, cmb_ref, qq, pl.multiple_of(k_odd + t, t))
            else:
                done = 2 * jj + 2 >= i
                scores_into(sb_ref, cmb_ref,
                            [jnp.where(done, b, a) for a, b in zip(qq, qq_next)],
                            pl.multiple_of(jnp.where(done, 0, 2 * jj + 2) * t, t))
            absorb(sa_ref, cma_ref, k_odd)
            return carry

        lax.fori_loop(0, i // 2, pair, 0)
        scores_into(sa_ref, cma_ref, qq_next, pl.multiple_of(nxt * t, t))
        if odd:
            absorb(sb_ref, cmb_ref, pl.multiple_of((i - 1) * t, t))
        for hd in heads:
            acc = acc_ref[hd]
            o = acc[:dv, :] / acc[dv:dv + 1, :]
            o = o[:, :t] - lam * o[:, t:]
            o = o * lax.rsqrt(jnp.mean(o * o, axis=0, keepdims=True) + SUBLN_EPS)
            o = o * (sg_ref[...] * (1.0 - lambda_init))
            o_ref[0, hd, pl.ds(q0, t), :] = o.T.astype(BF16)

    def q_tile_pair(a, carry):
        q_tile(2 * a, odd=False)
        q_tile(2 * a + 1, odd=True)
        return carry

    scores_into(sa_ref, cma_ref, [stacked_queries(hd, 0) for hd in heads], 0)
    lax.fori_loop(0, n_tiles // 2, q_tile_pair, 0)


def _attn(q, k, vta, lq1, lk1, lq2, lk2, subln_g, lambda_init):
    bsz, n_heads, seq, _ = q.shape
    t = ATTN_TILE
    nh = HEADS_PER_STEP
    assert (seq // t) % 2 == 0 and n_heads % nh == 0 and vta.shape[1] == n_heads * VT_ROWS
    blk = pl.BlockSpec((1, nh, seq, LANES), lambda b, h: (b, h, 0, 0))
    vec = pl.BlockSpec((1, HEAD_DIM), lambda b, h: (0, 0))
    return pl.pallas_call(
        functools.partial(_attn_kernel, lambda_init=lambda_init),
        grid=(bsz, n_heads // nh),
        in_specs=[blk, blk,
                  pl.BlockSpec((1, nh * VT_ROWS, seq), lambda b, h: (b, h, 0)),
                  vec, vec, vec, vec,
                  pl.BlockSpec((2 * HEAD_DIM, 1), lambda b, h: (0, 0))],
        out_specs=blk,
        out_shape=jax.ShapeDtypeStruct((bsz, n_heads, seq, LANES), BF16),
        scratch_shapes=[pltpu.VMEM((nh, t, 2 * t), F32),
                        pltpu.VMEM((nh, t, 2 * t), F32),
                        pltpu.VMEM((nh, 1, 2 * t), F32),
                        pltpu.VMEM((nh, 1, 2 * t), F32),
                        pltpu.VMEM((nh, 1, 2 * t), F32),
                        pltpu.VMEM((nh, VT_ROWS, 2 * t), F32)],
        compiler_params=pltpu.CompilerParams(
            dimension_semantics=("arbitrary", "arbitrary"),
            vmem_limit_bytes=ATTN_VMEM_LIMIT),
        name="attn",
    )(q, k, vta, lq1.reshape(1, -1), lk1.reshape(1, -1), lq2.reshape(1, -1),
      lk2.reshape(1, -1), subln_g.reshape(-1, 1))


def _rope_tables(seq):
    inv = 1.0 / (ROPE_THETA ** (np.arange(0, HEAD_DIM, 2, dtype=np.float64) / HEAD_DIM))
    ang = np.arange(seq, dtype=np.float64)[:, None] * inv[None, :]
    ang = np.concatenate([ang, ang, ang, ang], axis=-1)
    cos, sin = np.cos(ang), np.sin(ang)
    upper = (np.arange(LANES) % HEAD_DIM) >= HEAD_DIM // 2
    as_f32 = lambda a: jnp.asarray(a.astype(np.float32))
    return as_f32(cos), as_f32(np.where(upper, sin, 0.0)), as_f32(np.where(upper, 0.0, -sin))


def kernel(x, c, w_ada, b_ada, w_ada_final, b_ada_final, g_ffn1, g_mix, g_ffn2, g_final,
           ffn1_w1, ffn1_w3, ffn1_w2, ffn2_w1, ffn2_w3, ffn2_w2, w_in,
           lambda_q1, lambda_k1, lambda_q2, lambda_k2, subln_g, conv_w, conv_norm_g, w_out):
    bsz, seq, d = x.shape
    depth = w_ada.shape[0]
    cos, sinp, sinm = _rope_tables(seq)
    gsz = CONV_WIDTH // CONV_GROUPS
    grp = np.arange(CONV_WIDTH) // gsz
    gmat = jnp.asarray(np.where(grp[:, None] == grp[None, :], 1.0 / gsz, 0.0), dtype=BF16)

    x2d = x.reshape(bsz * seq, d)
    for l in range(depth):
        lambda_init = 0.8 - 0.6 * math.exp(-0.3 * l)
        projections = [(w_ada[l], b_ada[l])] + ([(w_ada_final, b_ada_final)] if l == 0 else [])
        mods = _ada(c, projections, side_casts=(ffn1_w1[l], ffn1_w3[l], ffn1_w2[l]))
        w1a, w3a, w2a = mods[len(projections):]
        mod3 = mods[0].reshape(bsz, N_MOD, d)
        if l == 0:
            modf3 = mods[1].reshape(bsz, 2, d)
        x2d, w_in_b, w_out_b, w1b, w3b, w2b = _ffn(
            x2d, mod3, g_ffn1[l], w1a, w3a, w2a, 0, seq,
            side_casts=(w_in[l], w_out[l], ffn2_w1[l], ffn2_w3[l], ffn2_w2[l]))
        w_vt = w_in_b[:, 2 * ATTN_WIDTH:3 * ATTN_WIDTH].T
        vt, q, k, y = _inproj(x2d, mod3, g_mix[l], w_in_b, w_vt, cos, sinp, sinm,
                              conv_w[l], conv_norm_g[l], gmat, seq)
        a = _attn(q, k, vt, lambda_q1[l], lambda_k1[l], lambda_q2[l], lambda_k2[l],
                  subln_g[l], lambda_init)
        final_args = (modf3, g_final) if l == depth - 1 else None
        x2d, = _ffn(x2d, mod3, g_ffn2[l], w1b, w3b, w2b, 6, seq, (a, y, w_out_b), final_args)
    return x2d.reshape(bsz, seq, d)
```

```python
import functools
import math

import jax
import jax.numpy as jnp
import numpy as np
from jax import lax
from jax.experimental import pallas as pl
from jax.experimental.pallas import tpu as pltpu

F32 = jnp.float32
BF16 = jnp.bfloat16

LANES = 128
SUBLANES = 8
BF16_ROWS = 16

D_MODEL = 1024
ATTN_HEADS = 4
HEAD_DIM = 64
ATTN_WIDTH = ATTN_HEADS * 2 * HEAD_DIM
CONV_WIDTH = D_MODEL - ATTN_WIDTH
CONV_GROUPS = 8
CONV_K = 3
ROPE_THETA = 10000.0
NORM_EPS = 1e-6
SUBLN_EPS = 1e-5
N_MOD = 9

FF_CHUNK = 256
ROW_TILE = 1024
ROW_SUBTILE = 512
ATTN_TILE = 512
ONES_ROWS = BF16_ROWS
VT_ROWS = 2 * HEAD_DIM + ONES_ROWS
HEADS_PER_STEP = 4
VMEM_LIMIT = 56 * 1024 * 1024
ATTN_VMEM_LIMIT = 60 * 1024 * 1024
ADA_BLOCKS = 4
NT_DIMS = (((1,), (1,)), ((), ()))
Q_SCALE = math.log2(math.e) / math.sqrt(HEAD_DIM)


def _silu(a):
    return a / (1.0 + jnp.exp(-a))


def _rms(x, eps):
    return x * lax.rsqrt(jnp.mean(x * x, axis=-1, keepdims=True) + eps)


def _ada_kernel(*refs):
    n_side = (len(refs) - 4) // 2
    c_ref, w_ref, b_ref = refs[:3]
    side_in = refs[3:3 + n_side]
    o_ref = refs[3 + n_side]
    side_out = refs[4 + n_side:]
    o_ref[...] = jnp.dot(_silu(c_ref[...]), w_ref[...],
                         preferred_element_type=F32) + b_ref[...]
    for f_ref, bf_ref in zip(side_in, side_out):
        bf_ref[...] = f_ref[...].astype(bf_ref.dtype)


def _ada(c, w, b, side_casts=()):
    bsz, d = c.shape
    n = w.shape[1]
    tn = n // ADA_BLOCKS
    side_specs = [pl.BlockSpec((s.shape[0] // ADA_BLOCKS, s.shape[1]), lambda j: (j, 0))
                  for s in side_casts]
    return pl.pallas_call(
        _ada_kernel,
        grid=(ADA_BLOCKS,),
        in_specs=[pl.BlockSpec((bsz, d), lambda j: (0, 0)),
                  pl.BlockSpec((d, tn), lambda j: (0, j)),
                  pl.BlockSpec((1, tn), lambda j: (0, j))] + side_specs,
        out_specs=[pl.BlockSpec((bsz, tn), lambda j: (0, j))] + side_specs,
        out_shape=[jax.ShapeDtypeStruct((bsz, n), F32)]
        + [jax.ShapeDtypeStruct(s.shape, BF16) for s in side_casts],
        compiler_params=pltpu.CompilerParams(vmem_limit_bytes=VMEM_LIMIT),
        name="ada",
    )(c, w, b.reshape(1, n), *side_casts)


def _ffn_kernel(*refs, mod_row, mixer, final, n_side):
    refs = list(refs)
    x_ref, mod_ref, g_ref = refs[:3]
    del refs[:3]
    if mixer:
        a_ref, y_ref, wo_ref = refs[:3]
        del refs[:3]
    w1_ref, w3_ref, w2_ref = refs[:3]
    del refs[:3]
    if final:
        modf_ref, gf_ref = refs[:2]
        del refs[:2]
    side_in = refs[:n_side]
    del refs[:n_side]
    o_ref = refs[0]
    side_out = refs[1:1 + n_side]
    act_ref = refs[1 + n_side]
    for w_ref, wb_ref in zip(side_in, side_out):
        wb_ref[...] = w_ref[...].astype(wb_ref.dtype)
    shift = mod_ref[0, mod_row:mod_row + 1, :]
    scale = mod_ref[0, mod_row + 1:mod_row + 2, :]
    gate = mod_ref[0, mod_row + 2:mod_row + 3, :]
    n_chunks = w2_ref.shape[0] // FF_CHUNK
    sub_tiles = [slice(r * ROW_SUBTILE, (r + 1) * ROW_SUBTILE)
                 for r in range(x_ref.shape[0] // ROW_SUBTILE)]
    hs = []
    for rows in sub_tiles:
        x = x_ref[rows, :]
        if mixer:
            attn = jnp.concatenate([a_ref[0, hd, rows, :] for hd in range(a_ref.shape[1])],
                                   axis=-1)
            mix = (jnp.dot(attn, wo_ref[:ATTN_WIDTH, :], preferred_element_type=F32)
                   + jnp.dot(y_ref[rows, :], wo_ref[ATTN_WIDTH:, :], preferred_element_type=F32))
            x = x + mod_ref[0, mod_row - 1:mod_row, :] * mix
            o_ref[rows, :] = x
        hs.append((_rms(x, NORM_EPS) * g_ref[...] * (1.0 + scale) + shift).astype(BF16))
    for rows, h in zip(sub_tiles, hs):
        for c in range(n_chunks):
            cols = slice(c * FF_CHUNK, (c + 1) * FF_CHUNK)
            a = jnp.dot(h, w1_ref[:, cols], preferred_element_type=F32)
            b = jnp.dot(h, w3_ref[:, cols], preferred_element_type=F32)
            act_ref[rows, cols] = (_silu(a) * b).astype(BF16)
        x = o_ref[rows, :] if mixer else x_ref[rows, :]
        y = x + 0.5 * gate * jnp.dot(act_ref[rows, :], w2_ref[...],
                                     preferred_element_type=F32)
        if final:
            y = (_rms(y, NORM_EPS) * gf_ref[...] * (1.0 + modf_ref[0, 1:2, :])
                 + modf_ref[0, 0:1, :])
        o_ref[rows, :] = y


def _ffn(x2d, mod3, g, w1, w3, w2, mod_row, rows_per_batch, mixer_args=None, final_args=None,
         side_casts=()):
    m, d = x2d.shape
    tm = ROW_TILE
    tiles_per_batch = rows_per_batch // tm
    n_steps = m // tm
    row = lambda i: (i, 0)
    fix = lambda i: (0, 0)
    per_batch = lambda i: (i // tiles_per_batch, 0, 0)
    resident = lambda w: pl.BlockSpec(w.shape, fix, pipeline_mode=pl.Buffered(1))
    in_specs = [pl.BlockSpec((tm, d), row),
                pl.BlockSpec((1, N_MOD, d), per_batch),
                pl.BlockSpec((1, d), fix)]
    args = [x2d, mod3, g.reshape(1, d)]
    if mixer_args is not None:
        a4d, y2d, w_out = mixer_args
        in_specs += [pl.BlockSpec((1, a4d.shape[1], tm, a4d.shape[3]),
                                  lambda i: (i // tiles_per_batch, 0, i % tiles_per_batch, 0)),
                     pl.BlockSpec((tm, y2d.shape[1]), row), resident(w_out)]
        args += [a4d, y2d, w_out]
    in_specs += [resident(w1), resident(w3), resident(w2)]
    args += [w1, w3, w2]
    if final_args is not None:
        modf3, gf = final_args
        in_specs += [pl.BlockSpec((1, 2, d), per_batch), pl.BlockSpec((1, d), fix)]
        args += [modf3, gf.reshape(1, d)]
    side_specs = []
    for w in side_casts:
        reps = 1
        while (w.shape[0] * reps // n_steps) % BF16_ROWS:
            reps *= 2
        side_specs.append(pl.BlockSpec((w.shape[0] * reps // n_steps, w.shape[1]),
                                       functools.partial(lambda i, r: (i // r, 0), r=reps)))
    in_specs += side_specs
    args += list(side_casts)
    return pl.pallas_call(
        functools.partial(_ffn_kernel, mod_row=mod_row, mixer=mixer_args is not None,
                          final=final_args is not None, n_side=len(side_casts)),
        grid=(n_steps,),
        in_specs=in_specs,
        out_specs=[pl.BlockSpec((tm, d), row)] + side_specs,
        out_shape=[jax.ShapeDtypeStruct((m, d), F32)]
        + [jax.ShapeDtypeStruct(w.shape, BF16) for w in side_casts],
        scratch_shapes=[pltpu.VMEM((tm, w2.shape[0]), BF16)],
        compiler_params=pltpu.CompilerParams(
            dimension_semantics=("arbitrary",), vmem_limit_bytes=VMEM_LIMIT),
        name="ffn" + ("_mixer" if mixer_args is not None else "")
        + ("_final" if final_args is not None else ""),
    )(*args)


def _inproj_kernel(x_ref, mod_ref, g_ref, w_ref, wvt_ref, cos_ref, sinp_ref, sinm_ref,
                   convw_ref, cng_ref, gmat_ref,
                   q_ref, k_ref, vt_ref, y_ref, halo_ref, *, mod_row, tiles_per_batch):
    shift = mod_ref[0, mod_row:mod_row + 1, :]
    scale = mod_ref[0, mod_row + 1:mod_row + 2, :]
    half = HEAD_DIM // 2
    w = convw_ref[...]

    @pl.when(pl.program_id(0) % tiles_per_batch == 0)
    def _():
        halo_ref[...] = jnp.zeros_like(halo_ref)

    prev = halo_ref[...]
    for r in range(x_ref.shape[0] // ROW_SUBTILE):
        rows = slice(r * ROW_SUBTILE, (r + 1) * ROW_SUBTILE)
        h = (_rms(x_ref[rows, :], NORM_EPS) * g_ref[...] * (1.0 + scale) + shift).astype(BF16)
        zqk = jnp.dot(h, w_ref[:, :2 * ATTN_WIDTH], preferred_element_type=F32)
        zcv = jnp.dot(h, w_ref[:, 3 * ATTN_WIDTH:], preferred_element_type=F32)
        vt = lax.dot_general(wvt_ref[...], h, NT_DIMS, preferred_element_type=F32).astype(BF16)
        for j in range(ATTN_HEADS):
            vt_ref[0, j * VT_ROWS:j * VT_ROWS + LANES, rows] = vt[j * LANES:(j + 1) * LANES, :]
            vt_ref[0, j * VT_ROWS + LANES:(j + 1) * VT_ROWS, rows] = jnp.ones(
                (ONES_ROWS, ROW_SUBTILE), BF16)
        cos = cos_ref[rows, :]
        sinp = sinp_ref[rows, :]
        sinm = sinm_ref[rows, :]
        for j in range(ATTN_WIDTH // LANES):
            for base, out_ref, mult in ((0, q_ref, Q_SCALE), (ATTN_WIDTH, k_ref, 1.0)):
                t = zqk[:, base + j * LANES: base + (j + 1) * LANES]
                rot = (t * cos + pltpu.roll(t, half, axis=1) * sinp
                       + pltpu.roll(t, LANES - half, axis=1) * sinm)
                out_ref[0, j, rows, :] = (rot * mult).astype(BF16)

        gate_b = zcv[:, :CONV_WIDTH]
        cu = zcv[:, CONV_WIDTH:2 * CONV_WIDTH] * zcv[:, 2 * CONV_WIDTH:]
        buf = jnp.concatenate([prev, cu], axis=0)
        cu_m1 = pltpu.roll(buf, 1, axis=0)[SUBLANES:, :]
        cu_m2 = pltpu.roll(buf, 2, axis=0)[SUBLANES:, :]
        prev = cu[ROW_SUBTILE - SUBLANES:, :]
        y = gate_b * (w[0:1, :] * cu_m2 + w[1:2, :] * cu_m1 + w[2:3, :] * cu)
        y2 = y * y
        y2_hi = y2.astype(BF16)
        y2_lo = (y2 - y2_hi.astype(F32)).astype(BF16)
        ms = (jnp.dot(y2_hi, gmat_ref[...], preferred_element_type=F32)
              + jnp.dot(y2_lo, gmat_ref[...], preferred_element_type=F32))
        y_ref[rows, :] = (y * lax.rsqrt(ms + NORM_EPS) * cng_ref[...]).astype(BF16)
    halo_ref[...] = prev


def _inproj(x2d, mod3, g, w_in, w_vt, cos, sinp, sinm, conv_w, conv_norm_g, gmat,
            rows_per_batch):
    m, d = x2d.shape
    tm = ROW_TILE
    tiles_per_batch = rows_per_batch // tm
    row = lambda i: (i, 0)
    fix = lambda i: (0, 0)
    pos = lambda i: (i % tiles_per_batch, 0)
    n_batch = m // rows_per_batch
    head_major = pl.BlockSpec((1, ATTN_HEADS, tm, LANES),
                              lambda i: (i // tiles_per_batch, 0, i % tiles_per_batch, 0))
    out_sd = jax.ShapeDtypeStruct((n_batch, ATTN_HEADS, rows_per_batch, LANES), BF16)
    return pl.pallas_call(
        functools.partial(_inproj_kernel, mod_row=3, tiles_per_batch=tiles_per_batch),
        grid=(m // tm,),
        in_specs=[pl.BlockSpec((tm, d), row),
                  pl.BlockSpec((1, N_MOD, d), lambda i: (i // tiles_per_batch, 0, 0)),
                  pl.BlockSpec((1, d), fix),
                  pl.BlockSpec(w_in.shape, fix, pipeline_mode=pl.Buffered(1)),
                  pl.BlockSpec(w_vt.shape, fix, pipeline_mode=pl.Buffered(1)),
                  pl.BlockSpec((tm, LANES), pos),
                  pl.BlockSpec((tm, LANES), pos),
                  pl.BlockSpec((tm, LANES), pos),
                  pl.BlockSpec((CONV_K, CONV_WIDTH), fix),
                  pl.BlockSpec((1, CONV_WIDTH), fix),
                  pl.BlockSpec((CONV_WIDTH, CONV_WIDTH), fix)],
        out_specs=[head_major, head_major,
                   pl.BlockSpec((1, ATTN_HEADS * VT_ROWS, tm),
                                lambda i: (i // tiles_per_batch, 0, i % tiles_per_batch)),
                   pl.BlockSpec((tm, CONV_WIDTH), row)],
        out_shape=[out_sd, out_sd,
                   jax.ShapeDtypeStruct((n_batch, ATTN_HEADS * VT_ROWS, rows_per_batch), BF16),
                   jax.ShapeDtypeStruct((m, CONV_WIDTH), BF16)],
        scratch_shapes=[pltpu.VMEM((SUBLANES, CONV_WIDTH), F32)],
        compiler_params=pltpu.CompilerParams(
            dimension_semantics=("arbitrary",), vmem_limit_bytes=VMEM_LIMIT),
        name="inproj",
    )(x2d, mod3, g.reshape(1, d), w_in, w_vt, cos, sinp, sinm, conv_w,
      conv_norm_g.reshape(1, CONV_WIDTH), gmat)


def _attn_kernel(q_ref, k_ref, vta_ref, lq1_ref, lk1_ref, lq2_ref, lk2_ref, sg_ref,
                 o_ref, sa_ref, sb_ref, cma_ref, cmb_ref, m_ref, acc_ref, *, lambda_init):
    seq = q_ref.shape[2]
    t = ATTN_TILE
    dv = 2 * HEAD_DIM
    n_tiles = seq // t
    heads = range(HEADS_PER_STEP)
    lam = (jnp.exp(jnp.sum(lq1_ref[...] * lk1_ref[...], axis=-1, keepdims=True))
           - jnp.exp(jnp.sum(lq2_ref[...] * lk2_ref[...], axis=-1, keepdims=True))
           + lambda_init)

    def stacked_queries(hd, i):
        q = q_ref[0, hd, pl.ds(pl.multiple_of(i * t, t), t), :]
        first_map = lax.broadcasted_iota(jnp.int32, (t, LANES), 1) < HEAD_DIM
        zero = jnp.zeros_like(q)
        return jnp.concatenate([jnp.where(first_map, q, zero),
                                jnp.where(first_map, zero, q)], axis=0)

    def scores_into(s_buf, cm_buf, qq, k0):
        for hd in heads:
            s = lax.dot_general(k_ref[0, hd, pl.ds(k0, t), :], qq[hd], NT_DIMS,
                                preferred_element_type=F32)
            s_buf[hd] = s
            cm_buf[hd] = jnp.max(s, axis=0, keepdims=True)

    def absorb(s_buf, cm_buf, k0, mask=False):
        for hd in heads:
            s = s_buf[hd]
            if mask:
                kpos = lax.broadcasted_iota(jnp.int32, (t, 2 * t), 0)
                qpos = lax.broadcasted_iota(jnp.int32, (t, 2 * t), 1)
                qpos = jnp.where(qpos >= t, qpos - t, qpos)
                s = jnp.where(kpos <= qpos, s, -jnp.inf)
                cm = jnp.max(s, axis=0, keepdims=True)
            else:
                cm = cm_buf[hd]
            m = m_ref[hd]
            m_new = jnp.maximum(m, cm)
            alpha = jnp.exp2(m - m_new)
            p = jnp.exp2(s - m_new).astype(BF16)
            m_ref[hd] = m_new
            pv = jnp.dot(vta_ref[0, hd * VT_ROWS:(hd + 1) * VT_ROWS, pl.ds(k0, t)], p,
                         preferred_element_type=F32)
            acc_ref[hd] = alpha * acc_ref[hd] + pv

    def q_tile(i, odd):
        q0 = pl.multiple_of(i * t, t)
        nxt = jnp.minimum(i + 1, n_tiles - 1)
        qq = [stacked_queries(hd, i) for hd in heads]
        qq_next = [stacked_queries(hd, nxt) for hd in heads]
        m_ref[...] = jnp.full(m_ref.shape, -jnp.inf, F32)
        acc_ref[...] = jnp.zeros_like(acc_ref)
        if not odd:
            scores_into(sb_ref, cmb_ref, [jnp.where(i == 0, b, a) for a, b in zip(qq, qq_next)], 0)
        absorb(sa_ref, cma_ref, q0, mask=True)

        def pair(jj, carry):
            k_even = pl.multiple_of(2 * jj * t, t)
            k_odd = pl.multiple_of(k_even + t, t)
            scores_into(sa_ref, cma_ref, qq, k_odd)
            absorb(sb_ref, cmb_ref, k_even)
            if odd:
                scores_into(sb_ref, cmb_ref, qq, pl.multiple_of(k_odd + t, t))
            else:
                done = 2 * jj + 2 >= i
                scores_into(sb_ref, cmb_ref,
                            [jnp.where(done, b, a) for a, b in zip(qq, qq_next)],
                            pl.multiple_of(jnp.where(done, 0, 2 * jj + 2) * t, t))
            absorb(sa_ref, cma_ref, k_odd)
            return carry

        lax.fori_loop(0, i // 2, pair, 0)
        scores_into(sa_ref, cma_ref, qq_next, pl.multiple_of(nxt * t, t))
        if odd:
            absorb(sb_ref, cmb_ref, pl.multiple_of((i - 1) * t, t))
        for hd in heads:
            acc = acc_ref[hd]
            o = acc[:dv, :] / acc[dv:dv + 1, :]
            o = o[:, :t] - lam * o[:, t:]
            o = o * lax.rsqrt(jnp.mean(o * o, axis=0, keepdims=True) + SUBLN_EPS)
            o = o * (sg_ref[...] * (1.0 - lambda_init))
            o_ref[0, hd, pl.ds(q0, t), :] = o.T.astype(BF16)

    def q_tile_pair(a, carry):
        q_tile(2 * a, odd=False)
        q_tile(2 * a + 1, odd=True)
        return carry

    scores_into(sa_ref, cma_ref, [stacked_queries(hd, 0) for hd in heads], 0)
    lax.fori_loop(0, n_tiles // 2, q_tile_pair, 0)


def _attn(q, k, vta, lq1, lk1, lq2, lk2, subln_g, lambda_init):
    bsz, n_heads, seq, _ = q.shape
    t = ATTN_TILE
    nh = HEADS_PER_STEP
    assert (seq // t) % 2 == 0 and n_heads % nh == 0 and vta.shape[1] == n_heads * VT_ROWS
    blk = pl.BlockSpec((1, nh, seq, LANES), lambda b, h: (b, h, 0, 0))
    vec = pl.BlockSpec((1, HEAD_DIM), lambda b, h: (0, 0))
    return pl.pallas_call(
        functools.partial(_attn_kernel, lambda_init=lambda_init),
        grid=(bsz, n_heads // nh),
        in_specs=[blk, blk,
                  pl.BlockSpec((1, nh * VT_ROWS, seq), lambda b, h: (b, h, 0)),
                  vec, vec, vec, vec,
                  pl.BlockSpec((2 * HEAD_DIM, 1), lambda b, h: (0, 0))],
        out_specs=blk,
        out_shape=jax.ShapeDtypeStruct((bsz, n_heads, seq, LANES), BF16),
        scratch_shapes=[pltpu.VMEM((nh, t, 2 * t), F32),
                        pltpu.VMEM((nh, t, 2 * t), F32),
                        pltpu.VMEM((nh, 1, 2 * t), F32),
                        pltpu.VMEM((nh, 1, 2 * t), F32),
                        pltpu.VMEM((nh, 1, 2 * t), F32),
                        pltpu.VMEM((nh, VT_ROWS, 2 * t), F32)],
        compiler_params=pltpu.CompilerParams(
            dimension_semantics=("arbitrary", "arbitrary"),
            vmem_limit_bytes=ATTN_VMEM_LIMIT),
        name="attn",
    )(q, k, vta, lq1.reshape(1, -1), lk1.reshape(1, -1), lq2.reshape(1, -1),
      lk2.reshape(1, -1), subln_g.reshape(-1, 1))


def _rope_tables(seq):
    inv = 1.0 / (ROPE_THETA ** (np.arange(0, HEAD_DIM, 2, dtype=np.float64) / HEAD_DIM))
    ang = np.arange(seq, dtype=np.float64)[:, None] * inv[None, :]
    ang = np.concatenate([ang, ang, ang, ang], axis=-1)
    cos, sin = np.cos(ang), np.sin(ang)
    upper = (np.arange(LANES) % HEAD_DIM) >= HEAD_DIM // 2
    as_f32 = lambda a: jnp.asarray(a.astype(np.float32))
    return as_f32(cos), as_f32(np.where(upper, sin, 0.0)), as_f32(np.where(upper, 0.0, -sin))


def kernel(x, c, w_ada, b_ada, w_ada_final, b_ada_final, g_ffn1, g_mix, g_ffn2, g_final,
           ffn1_w1, ffn1_w3, ffn1_w2, ffn2_w1, ffn2_w3, ffn2_w2, w_in,
           lambda_q1, lambda_k1, lambda_q2, lambda_k2, subln_g, conv_w, conv_norm_g, w_out):
    bsz, seq, d = x.shape
    depth = w_ada.shape[0]
    cos, sinp, sinm = _rope_tables(seq)
    gsz = CONV_WIDTH // CONV_GROUPS
    grp = np.arange(CONV_WIDTH) // gsz
    gmat = jnp.asarray(np.where(grp[:, None] == grp[None, :], 1.0 / gsz, 0.0), dtype=BF16)
    modf3 = _ada(c, w_ada_final, b_ada_final)[0].reshape(bsz, 2, d)

    x2d = x.reshape(bsz * seq, d)
    for l in range(depth):
        lambda_init = 0.8 - 0.6 * math.exp(-0.3 * l)
        mod, w1a, w3a, w2a = _ada(c, w_ada[l], b_ada[l],
                                  side_casts=(ffn1_w1[l], ffn1_w3[l], ffn1_w2[l]))
        mod3 = mod.reshape(bsz, N_MOD, d)
        x2d, w_in_b, w_out_b, w1b, w3b, w2b = _ffn(
            x2d, mod3, g_ffn1[l], w1a, w3a, w2a, 0, seq,
            side_casts=(w_in[l], w_out[l], ffn2_w1[l], ffn2_w3[l], ffn2_w2[l]))
        w_vt = w_in_b[:, 2 * ATTN_WIDTH:3 * ATTN_WIDTH].T
        q, k, vt, y = _inproj(x2d, mod3, g_mix[l], w_in_b, w_vt, cos, sinp, sinm,
                              conv_w[l], conv_norm_g[l], gmat, seq)
        a = _attn(q, k, vt, lambda_q1[l], lambda_k1[l], lambda_q2[l], lambda_k2[l],
                  subln_g[l], lambda_init)
        final_args = (modf3, g_final) if l == depth - 1 else None
        x2d, = _ffn(x2d, mod3, g_ffn2[l], w1b, w3b, w2b, 6, seq, (a, y, w_out_b), final_args)
    return x2d.reshape(bsz, seq, d)
```
